```python
import math
import jax, jax.numpy as jnp
from jax import lax
import numpy as np

D_MODEL = 1024
BATCH = 1
SEQ = 16384
DEPTH = 1
DEC_BATCH = 16
DEC_SEQ = 2048
PAST_LEN = 128

HEAD_DIM_A = 64
V_DIM_A = 2 * HEAD_DIM_A
N_HEADS_A = D_MODEL // V_DIM_A
D_A = N_HEADS_A * V_DIM_A
QK_A = N_HEADS_A * 2 * HEAD_DIM_A
Q_BLOCK = 128
N_HEADS_B = 4
DK_B = D_MODEL // 2
DV_B = D_MODEL
KEY_DIM_B = DK_B // N_HEADS_B
V_DIM_B = DV_B // N_HEADS_B
GATE_RANK = 16
GATE_NORM = 16.0
CHUNK = 64
N_BUCKETS = 32
MAX_DISTANCE = 128
D_FF = 4 * D_MODEL
EPS = 1e-6

IN_SPLIT = [QK_A, QK_A, D_A, DK_B, DK_B, DV_B, DV_B, GATE_RANK, GATE_RANK, D_MODEL, D_MODEL]
IN_OFFSETS = [int(v) for v in np.cumsum(IN_SPLIT)[:-1]]
D_IN = int(sum(IN_SPLIT))

kernel_name = "hybrid_diffattn_gla_encoder"


def rms_norm(x, g):
    xf = x.astype(jnp.float32)
    y = xf * lax.rsqrt(jnp.mean(xf * xf, axis=-1, keepdims=True) + EPS)
    return (y * g.astype(jnp.float32)).astype(x.dtype)


def t5_bucket(rel):
    nb = N_BUCKETS // 2
    max_exact = nb // 2
    ret = (rel > 0).astype(jnp.int32) * nb
    n = jnp.abs(rel)
    nf = jnp.maximum(n, 1).astype(jnp.float32)
    large = max_exact + (jnp.log(nf / max_exact) / math.log(MAX_DISTANCE / max_exact)
                         * (nb - max_exact)).astype(jnp.int32)
    large = jnp.minimum(large, nb - 1)
    return ret + jnp.where(n < max_exact, n, large)


def diff_attention(q, k, v, lam, rel_bias):
    B, S = q.shape[0], q.shape[1]
    nblk = S // Q_BLOCK
    scale = HEAD_DIM_A ** -0.5
    qb = q.reshape(B, nblk, Q_BLOCK, N_HEADS_A, 2, HEAD_DIM_A).transpose(1, 0, 2, 3, 4, 5)
    kpos = jnp.arange(S, dtype=jnp.int32)
    vf = v.astype(jnp.float32)

    def one_block(args):
        q_blk, i = args
        qpos = i * Q_BLOCK + jnp.arange(Q_BLOCK, dtype=jnp.int32)
        bias = rel_bias[t5_bucket(kpos[None, :] - qpos[:, None])]
        bias = bias.transpose(2, 0, 1).astype(jnp.float32)
        s = jnp.einsum('bqhmd,bkhmd->bmhqk', q_blk, k).astype(jnp.float32) * scale + bias
        p = jax.nn.softmax(s, axis=-1)
        w = p[:, 0] - lam * p[:, 1]
        return jnp.einsum('bhqk,bkhd->bqhd', w, vf)

    o = lax.map(one_block, (qb, jnp.arange(nblk, dtype=jnp.int32)))
    return o.transpose(1, 0, 2, 3, 4).reshape(B, S, N_HEADS_A, V_DIM_A)


def gla_scan(q, k, v, g):
    B, S, H, dk = q.shape
    dv = v.shape[-1]
    nc = S // CHUNK

    def to_chunks(t):
        return t.reshape(B, nc, CHUNK, H, t.shape[-1]).transpose(1, 0, 3, 2, 4)

    mask = jnp.tril(jnp.ones((CHUNK, CHUNK), dtype=bool))[:, :, None]

    def step(state, inp):
        qc, kc, vc, gc = inp
        b = jnp.cumsum(gc, axis=2)
        o_inter = jnp.einsum('bhcd,bhde->bhce', qc * jnp.exp(b), state)
        diff = b[:, :, :, None, :] - b[:, :, None, :, :]
        decay = jnp.where(mask, jnp.exp(jnp.minimum(diff, 0.0)), 0.0)
        a = jnp.einsum('bhid,bhjd,bhijd->bhij', qc, kc, decay)
        o = o_inter + jnp.einsum('bhij,bhje->bhie', a, vc)
        b_last = b[:, :, -1:, :]
        state = (jnp.exp(b_last[:, :, 0, :, None]) * state
                 + jnp.einsum('bhjd,bhje->bhde', kc * jnp.exp(b_last - b), vc))
        return state, o

    s0 = jnp.zeros((B, H, dk, dv), jnp.float32)
    _, o = lax.scan(step, s0, (to_chunks(q), to_chunks(k), to_chunks(v), to_chunks(g)))
    return o.transpose(1, 0, 3, 2, 4).reshape(B, S, H, dv)


def bidirectional_gla(q, k, v, g_fwd, g_bwd):
    o_f = gla_scan(q, k, v, g_fwd)
    flip = lambda t: jnp.flip(t, axis=1)
    o_b = flip(gla_scan(flip(q), flip(k), flip(v), flip(g_bwd)))
    return o_f + o_b


def encoder_layer(x, c, layer_idx, rel_bias, w_ada, b_ada, norm1_g, w_in, q_norm_g, k_norm_g,
                  lam_q1, lam_k1, lam_q2, lam_k2, subln_g, w_gate_f, b_gate_f, w_gate_b, b_gate_b,
                  gla_norm_g, w_branch_a, w_branch_b, w_out, norm2_g, w_up, w_down):
    B, S, _ = x.shape
    mod = jax.nn.silu(c) @ w_ada + b_ada
    shift1, scale1, gate1, shift2, scale2, gate2 = jnp.split(mod, 6, axis=-1)

    h = rms_norm(x, norm1_g) * (1.0 + scale1[:, None]) + shift1[:, None]
    proj = h @ w_in
    qa, ka, va, qg, kg, vg, og, lr_f, lr_b, ga, gb = jnp.split(proj, IN_OFFSETS, axis=-1)

    lambda_init = 0.8 - 0.6 * math.exp(-0.3 * layer_idx)
    lam = (jnp.exp(jnp.sum(lam_q1.astype(jnp.float32) * lam_k1.astype(jnp.float32)))
           - jnp.exp(jnp.sum(lam_q2.astype(jnp.float32) * lam_k2.astype(jnp.float32))) + lambda_init)
    qa = rms_norm(qa.reshape(B, S, N_HEADS_A, 2, HEAD_DIM_A), q_norm_g)
    ka = rms_norm(ka.reshape(B, S, N_HEADS_A, 2, HEAD_DIM_A), k_norm_g)
    oa = diff_attention(qa, ka, va.reshape(B, S, N_HEADS_A, V_DIM_A), lam, rel_bias)
    oa = (rms_norm(oa, subln_g) * (1.0 - lambda_init)).astype(x.dtype)
    ya = oa.reshape(B, S, D_A) @ w_branch_a

    f32 = jnp.float32
    g_f = jax.nn.log_sigmoid((lr_f @ w_gate_f + b_gate_f).astype(f32)) / GATE_NORM
    g_b = jax.nn.log_sigmoid((lr_b @ w_gate_b + b_gate_b).astype(f32)) / GATE_NORM
    hk = lambda t: t.reshape(B, S, N_HEADS_B, KEY_DIM_B)
    ob = bidirectional_gla(hk(qg.astype(f32)) * KEY_DIM_B ** -0.5, hk(kg.astype(f32)),
                           vg.astype(f32).reshape(B, S, N_HEADS_B, V_DIM_B), hk(g_f), hk(g_b))
    ob = rms_norm(ob, gla_norm_g) * jax.nn.silu(og.astype(f32)).reshape(B, S, N_HEADS_B, V_DIM_B)
    yb = ob.astype(x.dtype).reshape(B, S, DV_B) @ w_branch_b

    merged = jax.nn.sigmoid(ga) * ya + jax.nn.sigmoid(gb) * yb
    x = x + gate1[:, None] * (merged @ w_out)

    h2 = rms_norm(x, norm2_g) * (1.0 + scale2[:, None]) + shift2[:, None]
    u = jax.nn.relu(h2 @ w_up)
    x = x + gate2[:, None] * ((u * u) @ w_down)
    return x


def setup_inputs(seed: int = 0) -> dict:
    key = jax.random.key(seed)
    ks = jax.random.split(key, 32)
    nrm = lambda k, shape, s: jax.random.normal(k, shape, jnp.float32) * s
    L = DEPTH
    return {
        "x_prompt": nrm(ks[0], (BATCH, SEQ, D_MODEL), 1.0),
        "x_sample": nrm(ks[1], (DEC_BATCH, DEC_SEQ, D_MODEL), 1.0),
        "c_prompt": nrm(ks[2], (BATCH, D_MODEL), 1.0),
        "c_sample": nrm(ks[3], (DEC_BATCH, D_MODEL), 1.0),
        "rel_bias": nrm(ks[4], (N_BUCKETS, N_HEADS_A), 0.5),
        "w_ada": nrm(ks[5], (L, D_MODEL, 6 * D_MODEL), 0.2 * D_MODEL ** -0.5),
        "b_ada": nrm(ks[6], (L, 6 * D_MODEL), 0.02),
        "norm1_g": 1.0 + nrm(ks[7], (L, D_MODEL), 0.02),
        "w_in": nrm(ks[8], (L, D_MODEL, D_IN), D_MODEL ** -0.5),
        "q_norm_g": 1.0 + nrm(ks[9], (L, HEAD_DIM_A), 0.02),
        "k_norm_g": 1.0 + nrm(ks[10], (L, HEAD_DIM_A), 0.02),
        "lam_q1": nrm(ks[11], (L, HEAD_DIM_A), 0.1),
        "lam_k1": nrm(ks[12], (L, HEAD_DIM_A), 0.1),
        "lam_q2": nrm(ks[13], (L, HEAD_DIM_A), 0.1),
        "lam_k2": nrm(ks[14], (L, HEAD_DIM_A), 0.1),
        "subln_g": 1.0 + nrm(ks[15], (L, V_DIM_A), 0.02),
        "w_gate_f": nrm(ks[16], (L, GATE_RANK, DK_B), GATE_RANK ** -0.5),
        "b_gate_f": nrm(ks[17], (L, DK_B), 0.1),
        "w_gate_b": nrm(ks[18], (L, GATE_RANK, DK_B), GATE_RANK ** -0.5),
        "b_gate_b": nrm(ks[19], (L, DK_B), 0.1),
        "gla_norm_g": 1.0 + nrm(ks[20], (L, V_DIM_B), 0.02),
        "w_branch_a": nrm(ks[21], (L, D_A, D_MODEL), D_A ** -0.5),
        "w_branch_b": nrm(ks[22], (L, DV_B, D_MODEL), DV_B ** -0.5),
        "w_out": nrm(ks[23], (L, D_MODEL, D_MODEL), D_MODEL ** -0.5),
        "norm2_g": 1.0 + nrm(ks[24], (L, D_MODEL), 0.02),
        "w_up": nrm(ks[25], (L, D_MODEL, D_FF), D_MODEL ** -0.5),
        "w_down": nrm(ks[26], (L, D_FF, D_MODEL), D_FF ** -0.5),
    }


def reference(x_prompt, x_sample, c_prompt, c_sample, rel_bias, w_ada, b_ada, norm1_g, w_in,
              q_norm_g, k_norm_g, lam_q1, lam_k1, lam_q2, lam_k2, subln_g, w_gate_f, b_gate_f,
              w_gate_b, b_gate_b, gla_norm_g, w_branch_a, w_branch_b, w_out, norm2_g, w_up, w_down):
    def run_trunk(x, c):
        for l in range(DEPTH):
            x = encoder_layer(x, c, l, rel_bias, w_ada[l], b_ada[l], norm1_g[l], w_in[l],
                              q_norm_g[l], k_norm_g[l], lam_q1[l], lam_k1[l], lam_q2[l], lam_k2[l],
                              subln_g[l], w_gate_f[l], b_gate_f[l], w_gate_b[l], b_gate_b[l],
                              gla_norm_g[l], w_branch_a[l], w_branch_b[l], w_out[l], norm2_g[l],
                              w_up[l], w_down[l])
        return x

    y_prompt = run_trunk(x_prompt, c_prompt)
    y_sample = run_trunk(x_sample, c_sample)
    return (y_prompt, y_sample)
```

```python
import functools
import math

import jax
import jax.numpy as jnp
import numpy as np
from jax import lax
from jax.experimental import pallas as pl
from jax.experimental.pallas import tpu as pltpu

F32 = jnp.float32
BF16 = jnp.bfloat16

D_MODEL = 1024
HEAD_DIM_A = 64
N_HEADS_A = 8
HEAD_W_A = 2 * HEAD_DIM_A
N_HEADS_B = 4
KEY_DIM_B = 128
V_DIM_B = 256
GATE_RANK = 16
GATE_NORM = 16.0
N_BUCKETS = 32
MAX_DISTANCE = 128
D_FF = 4 * D_MODEL
EPS = 1e-6
LOG2E = math.log2(math.e)

N_MAIN = 8 * D_MODEL
LR_W = 128

VMEM_LIMIT = 56 * 1024 * 1024
ROW_TILE = 512
ATT_TILE = 256
GLA_CHUNK = 128


def _cparams(sem):
    return pltpu.CompilerParams(dimension_semantics=sem, vmem_limit_bytes=VMEM_LIMIT)


def _ada_kernel(c_ref, w_ref, b_ref, o_ref):
    c = c_ref[...]
    a = c / (1.0 + jnp.exp(-c))
    o_ref[...] = jnp.dot(a, w_ref[...], preferred_element_type=F32,
                         precision=lax.Precision.HIGHEST) + b_ref[...]


def _ada(c_all, w_ada, b_ada):
    r = c_all.shape[0]
    tn = 1024
    return pl.pallas_call(
        _ada_kernel,
        grid=(6 * D_MODEL // tn,),
        in_specs=[pl.BlockSpec((r, D_MODEL), lambda n: (0, 0)),
                  pl.BlockSpec((D_MODEL, tn), lambda n: (0, n)),
                  pl.BlockSpec((1, tn), lambda n: (0, n))],
        out_specs=pl.BlockSpec((r, tn), lambda n: (0, n)),
        out_shape=jax.ShapeDtypeStruct((r, 6 * D_MODEL), F32),
        compiler_params=_cparams(("parallel",)),
        name="ada",
    )(c_all, w_ada, b_ada.reshape(1, -1))


def _modulated_norm(x, g, shift, scale):
    ms = jnp.mean(x * x, axis=-1, keepdims=True)
    return x * lax.rsqrt(ms + EPS) * g * (1.0 + scale) + shift


def _proj_kernel(x_ref, mod_ref, g_ref, w_ref, wlr_ref, o_ref, lr_ref, h_scr):
    @pl.when(pl.program_id(1) == 0)
    def _():
        h = _modulated_norm(x_ref[...], g_ref[...], mod_ref[0, 0:1, :], mod_ref[0, 1:2, :])
        hb = h.astype(BF16)
        h_scr[...] = hb
        lr_ref[...] = jnp.dot(hb, wlr_ref[...], preferred_element_type=F32)

    o_ref[...] = jnp.dot(h_scr[...], w_ref[...], preferred_element_type=F32).astype(o_ref.dtype)


def _proj(x2, mod8, g1, w_main, w_lr, seq):
    t = x2.shape[0]
    tm, tn = ROW_TILE, 1024
    return pl.pallas_call(
        _proj_kernel,
        grid=(t // tm, N_MAIN // tn),
        in_specs=[pl.BlockSpec((tm, D_MODEL), lambda i, n: (i, 0)),
                  pl.BlockSpec((1, 8, D_MODEL), lambda i, n: ((i * tm) // seq, 0, 0)),
                  pl.BlockSpec((1, D_MODEL), lambda i, n: (0, 0)),
                  pl.BlockSpec((D_MODEL, tn), lambda i, n: (0, n)),
                  pl.BlockSpec((D_MODEL, LR_W), lambda i, n: (0, 0))],
        out_specs=[pl.BlockSpec((tm, tn), lambda i, n: (i, n)),
                   pl.BlockSpec((tm, LR_W), lambda i, n: (i, 0))],
        out_shape=[jax.ShapeDtypeStruct((t, N_MAIN), BF16),
                   jax.ShapeDtypeStruct((t, LR_W), F32)],
        scratch_shapes=[pltpu.VMEM((tm, D_MODEL), BF16)],
        compiler_params=_cparams(("parallel", "arbitrary")),
        name="proj",
    )(x2, mod8, g1, w_main, w_lr)


def _pair_rms(x, g):
    lane = lax.broadcasted_iota(jnp.int32, x.shape, 1)
    x2 = x * x
    s_all = jnp.sum(x2, axis=-1, keepdims=True)
    s_lo = jnp.sum(jnp.where(lane < HEAD_DIM_A, x2, 0.0), axis=-1, keepdims=True)
    ss = jnp.where(lane < HEAD_DIM_A, s_lo, s_all - s_lo)
    return x * lax.rsqrt(ss * (1.0 / HEAD_DIM_A) + EPS) * g


def _prep_kernel(q_ref, k_ref, v_ref, qg_ref, kg_ref, qt_ref, kn_ref, vt_ref):
    qn = _pair_rms(q_ref[0].astype(F32), qg_ref[...]) * (HEAD_DIM_A ** -0.5 * LOG2E)
    qt_ref[0, 0] = qn.T.astype(BF16)
    kn_ref[0] = _pair_rms(k_ref[0].astype(F32), kg_ref[...]).astype(BF16)
    vt_ref[0, 0] = v_ref[0].astype(F32).T.astype(BF16)


def _prep(main3, qg2, kg2):
    b, s, _ = main3.shape
    ts = 512
    tile = lambda off: pl.BlockSpec((1, ts, HEAD_W_A), lambda bi, si, h: (bi, si, off + h))
    tspec = pl.BlockSpec((1, 1, HEAD_W_A, ts), lambda bi, si, h: (bi, h, 0, si))
    gspec = pl.BlockSpec((1, HEAD_W_A), lambda bi, si, h: (0, 0))
    return pl.pallas_call(
        _prep_kernel,
        grid=(b, s // ts, N_HEADS_A),
        in_specs=[tile(0), tile(N_HEADS_A), tile(2 * N_HEADS_A), gspec, gspec],
        out_specs=[tspec, tile(0), tspec],
        out_shape=[jax.ShapeDtypeStruct((b, N_HEADS_A, HEAD_W_A, s), BF16),
                   jax.ShapeDtypeStruct((b, s, D_MODEL), BF16),
                   jax.ShapeDtypeStruct((b, N_HEADS_A, HEAD_W_A, s), BF16)],
        compiler_params=_cparams(("parallel", "parallel", "parallel")),
        name="prep",
    )(main3, main3, main3, qg2, kg2)


def _t5_bucket(rel):
    nb = N_BUCKETS // 2
    max_exact = nb // 2
    ret = (rel > 0).astype(jnp.int32) * nb
    n = jnp.abs(rel)
    nf = jnp.maximum(n, 1).astype(F32)
    large = max_exact + (jnp.log(nf / max_exact) / math.log(MAX_DISTANCE / max_exact)
                         * (nb - max_exact)).astype(jnp.int32)
    large = jnp.minimum(large, nb - 1)
    return ret + jnp.where(n < max_exact, n, large)


def _bias_kernel(tbl_ref, bkt_ref, o_ref):
    h = pl.program_id(0)
    bkt = bkt_ref[...]
    acc = jnp.zeros(bkt.shape, F32)
    for i in range(N_BUCKETS):
        acc = jnp.where(bkt == i, tbl_ref[i, h], acc)
    o_ref[0] = acc * LOG2E


def _bias_tiles(rel_bias):
    t = ATT_TILE
    kk = jnp.arange(t, dtype=jnp.int32)[:, None]
    qq = jnp.arange(t, dtype=jnp.int32)[None, :]
    rel = jnp.stack([(d - 1) * t + kk - qq for d in range(3)])
    return pl.pallas_call(
        _bias_kernel,
        grid=(N_HEADS_A,),
        in_specs=[pl.BlockSpec(memory_space=pltpu.SMEM),
                  pl.BlockSpec((3, t, t), lambda h: (0, 0, 0))],
        out_specs=pl.BlockSpec((1, 3, t, t), lambda h: (h, 0, 0, 0)),
        out_shape=jax.ShapeDtypeStruct((N_HEADS_A, 3, t, t), F32),
        compiler_params=_cparams(("parallel",)),
        name="bias",
    )(rel_bias, _t5_bucket(rel))


def _attn_kernel(tbl_ref, qt_ref, k_ref, vt_ref, bias_ref, lamv_ref, subg_ref, o_ref,
                 w_scr, m_scr, l_scr, acc_scr, *, nk, lambda_init):
    t = ATT_TILE
    h = pl.program_id(1)
    qi = pl.program_id(2)

    q = qt_ref[0, 0]
    z = jnp.zeros((HEAD_DIM_A, t), BF16)
    w_scr[:, :t] = jnp.concatenate([q[:HEAD_DIM_A], z], axis=0)
    w_scr[:, t:] = jnp.concatenate([z, q[HEAD_DIM_A:]], axis=0)
    m_scr[...] = jnp.full(m_scr.shape, -jnp.inf, F32)
    l_scr[...] = jnp.zeros(l_scr.shape, F32)
    acc_scr[...] = jnp.zeros(acc_scr.shape, F32)

    def chunk(kj, bias):
        k0 = pl.multiple_of(kj * t, t)
        s = jnp.dot(k_ref[0, pl.ds(k0, t), :], w_scr[...], preferred_element_type=F32) + bias
        m_old = m_scr[...]
        m_new = jnp.maximum(m_old, jnp.max(s, axis=0, keepdims=True))
        alpha = jnp.exp2(m_old - m_new)
        p = jnp.exp2(s - m_new)
        l_scr[...] = alpha * l_scr[...] + jnp.sum(p, axis=0, keepdims=True)
        pv = jnp.dot(vt_ref[0, 0, :, pl.ds(k0, t)], p.astype(BF16), preferred_element_type=F32)
        acc_scr[...] = alpha * acc_scr[...] + pv
        m_scr[...] = m_new

    far_left = tbl_ref[N_BUCKETS // 2 - 1, h] * LOG2E
    far_right = tbl_ref[N_BUCKETS - 1, h] * LOG2E

    def left_body(kj, c):
        chunk(kj, far_left)
        return c

    def right_body(kj, c):
        chunk(kj, far_right)
        return c

    lax.fori_loop(0, jnp.maximum(qi - 1, 0), left_body, 0)
    for d in range(3):
        kj = qi + d - 1

        @pl.when(jnp.logical_and(kj >= 0, kj < nk))
        def _(kj=kj, d=d):
            b = bias_ref[0, d]
            chunk(kj, jnp.concatenate([b, b], axis=1))

    lax.fori_loop(qi + 2, nk, right_body, 0)

    lv = lamv_ref[...]
    lam = (jnp.exp(jnp.sum(lv[0:1] * lv[1:2], axis=-1, keepdims=True))
           - jnp.exp(jnp.sum(lv[2:3] * lv[3:4], axis=-1, keepdims=True)) + lambda_init)
    acc = acc_scr[...]
    inv = 1.0 / l_scr[...]
    o = acc[:, :t] * inv[:, :t] - lam * (acc[:, t:] * inv[:, t:])
    ms = jnp.mean(o * o, axis=0, keepdims=True)
    y = (o * lax.rsqrt(ms + EPS)).T
    o_ref[0] = (y * subg_ref[...] * (1.0 - lambda_init)).astype(o_ref.dtype)


def _attn(rel_bias, qt, kn, vt, bias_t, lamv, subg, lambda_init):
    b, s, _ = kn.shape
    t = ATT_TILE
    nk = s // t
    kern = functools.partial(_attn_kernel, nk=nk, lambda_init=lambda_init)
    return pl.pallas_call(
        kern,
        grid=(b, N_HEADS_A, s // t),
        in_specs=[pl.BlockSpec(memory_space=pltpu.SMEM),
                  pl.BlockSpec((1, 1, HEAD_W_A, t), lambda bi, h, qi: (bi, h, 0, qi)),
                  pl.BlockSpec((1, s, HEAD_W_A), lambda bi, h, qi: (bi, 0, h)),
                  pl.BlockSpec((1, 1, HEAD_W_A, s), lambda bi, h, qi: (bi, h, 0, 0)),
                  pl.BlockSpec((1, 3, t, t), lambda bi, h, qi: (h, 0, 0, 0)),
                  pl.BlockSpec((8, HEAD_W_A), lambda bi, h, qi: (0, 0)),
                  pl.BlockSpec((1, HEAD_W_A), lambda bi, h, qi: (0, 0))],
        out_specs=pl.BlockSpec((1, t, HEAD_W_A), lambda bi, h, qi: (bi, qi, h)),
        out_shape=jax.ShapeDtypeStruct((b, s, D_MODEL), BF16),
        scratch_shapes=[pltpu.VMEM((HEAD_W_A, 2 * t), BF16),
                        pltpu.VMEM((1, 2 * t), F32),
                        pltpu.VMEM((1, 2 * t), F32),
                        pltpu.VMEM((HEAD_W_A, 2 * t), F32)],
        compiler_params=_cparams(("parallel", "parallel", "parallel")),
        name="attn",
    )(rel_bias, qt, kn, vt, bias_t, lamv, subg)


def _log_gate(lr, wg_ref, bg_ref):
    x = jnp.dot(lr.astype(BF16), wg_ref[...], preferred_element_type=F32) + bg_ref[...]
    return (jnp.minimum(x, 0.0) - jnp.log1p(jnp.exp(-jnp.abs(x)))) * (1.0 / GATE_NORM)


def _gla_bwd_kernel(q_ref, k_ref, v_ref, lr_ref, wg_ref, bg_ref, o_ref, s_scr):
    c = GLA_CHUNK

    @pl.when(pl.program_id(1) == 0)
    def _():
        s_scr[...] = jnp.zeros(s_scr.shape, F32)

    g = _log_gate(lr_ref[0], wg_ref, bg_ref)
    row = lax.broadcasted_iota(jnp.int32, g.shape, 0)
    suf = g
    sh = 1
    while sh < c:
        suf = suf + jnp.where(row < c - sh, pltpu.roll(suf, c - sh, 0), 0.0)
        sh *= 2
    tot = suf[0:1, :]
    for h in range(N_HEADS_B):
        sl = slice(h * KEY_DIM_B, (h + 1) * KEY_DIM_B)
        vs = slice(h * V_DIM_B, (h + 1) * V_DIM_B)
        q = q_ref[0][:, sl].astype(F32) * KEY_DIM_B ** -0.5
        k = k_ref[0][:, sl].astype(F32)
        qd = (q * jnp.exp(suf[:, sl])).astype(BF16)
        kd = k * jnp.exp(tot[:, sl] - suf[:, sl])
        st = s_scr[h]
        o_ref[0, :, vs] = jnp.dot(qd, st.astype(BF16), preferred_element_type=F32)
        dec = jnp.broadcast_to(jnp.exp(tot[:, sl]), (c, KEY_DIM_B)).T[:, 0:1]
        s_scr[h] = st * dec + jnp.dot(kd.T.astype(BF16), v_ref[0][:, vs], preferred_element_type=F32)


def _gla_fwd_kernel(q_ref, k_ref, v_ref, og_ref, lr_ref, obi_ref, wgf_ref, bgf_ref, wgb_ref, bgb_ref,
                    gn_ref, o_ref, s_scr):
    c = GLA_CHUNK

    @pl.when(pl.program_id(1) == 0)
    def _():
        s_scr[...] = jnp.zeros(s_scr.shape, F32)

    lr = lr_ref[0]
    gf_all = _log_gate(lr, wgf_ref, bgf_ref)
    gb_all = _log_gate(lr, wgb_ref, bgb_ref)
    ri = lax.broadcasted_iota(jnp.int32, (c, c), 0)
    ci = lax.broadcasted_iota(jnp.int32, (c, c), 1)
    xr = jnp.bitwise_xor(ri, ci)
    nt = (((1,), (1,)), ((), ()))
    for h in range(N_HEADS_B):
        sl = slice(h * KEY_DIM_B, (h + 1) * KEY_DIM_B)
        vs = slice(h * V_DIM_B, (h + 1) * V_DIM_B)
        q = q_ref[0][:, sl].astype(F32) * KEY_DIM_B ** -0.5
        k = k_ref[0][:, sl].astype(F32)
        v = v_ref[0][:, vs]
        pre = gf_all[:, sl]
        tf = pre
        suf = gb_all[:, sl]
        tb = suf
        a = jnp.where(ri == ci, 2.0 * lax.dot_general(q.astype(BF16), k.astype(BF16), nt,
                                                      preferred_element_type=F32), 0.0)
        blk = 1
        while blk < c:
            level = jnp.logical_and(xr >= blk, xr < 2 * blk)
            rf = lax.dot_general((q * jnp.exp(pre)).astype(BF16), (k * jnp.exp(tf - pre)).astype(BF16),
                                 nt, preferred_element_type=F32)
            a = jnp.where(jnp.logical_and(level, ri > ci), rf, a)
            rb = lax.dot_general((q * jnp.exp(suf)).astype(BF16), (k * jnp.exp(tb - suf)).astype(BF16),
                                 nt, preferred_element_type=F32)
            a = jnp.where(jnp.logical_and(level, ri < ci), rb, a)
            hi = jnp.bitwise_and(ri, blk) != 0
            tf_dn, tf_up = pltpu.roll(tf, blk, 0), pltpu.roll(tf, c - blk, 0)
            tb_dn, tb_up = pltpu.roll(tb, blk, 0), pltpu.roll(tb, c - blk, 0)
            pre = pre + jnp.where(hi, tf_dn, 0.0)
            suf = suf + jnp.where(hi, 0.0, tb_up)
            tf = tf + jnp.where(hi, tf_dn, tf_up)
            tb = tb + jnp.where(hi, tb_dn, tb_up)
            blk *= 2
        st = s_scr[h]
        o = (jnp.dot(a.astype(BF16), v, preferred_element_type=F32)
             + jnp.dot((q * jnp.exp(pre)).astype(BF16), st.astype(BF16), preferred_element_type=F32)
             + obi_ref[0, :, vs])
        kd = k * jnp.exp(tf - pre)
        s_scr[h] = st * jnp.exp(tf).T[:, 0:1] + jnp.dot(kd.T.astype(BF16), v, preferred_element_type=F32)
        ms = jnp.mean(o * o, axis=-1, keepdims=True)
        og = og_ref[0][:, vs].astype(F32)
        o_ref[0, :, vs] = (o * lax.rsqrt(ms + EPS) * gn_ref[...] * (og / (1.0 + jnp.exp(-og)))
                           ).astype(o_ref.dtype)


def _gla(main3, lr3, wgf, bgf, wgb, bgb, gn):
    b, s, _ = main3.shape
    c = GLA_CHUNK
    nc = s // c
    hk = N_HEADS_B * KEY_DIM_B
    state = pltpu.VMEM((N_HEADS_B, KEY_DIM_B, V_DIM_B), F32)

    def specs(cidx):
        return [pl.BlockSpec((1, c, hk), lambda bi, i: (bi, cidx(i), 3 * D_MODEL // hk)),
                pl.BlockSpec((1, c, hk), lambda bi, i: (bi, cidx(i), 3 * D_MODEL // hk + 1)),
                pl.BlockSpec((1, c, D_MODEL), lambda bi, i: (bi, cidx(i), 4))]

    full = lambda shape: pl.BlockSpec(shape, lambda bi, i: (0,) * len(shape))
    rev = lambda i: nc - 1 - i
    obi = pl.pallas_call(
        _gla_bwd_kernel,
        grid=(b, nc),
        in_specs=specs(rev) + [pl.BlockSpec((1, c, LR_W), lambda bi, i: (bi, rev(i), 0)),
                               full((LR_W, hk)), full((1, hk))],
        out_specs=pl.BlockSpec((1, c, D_MODEL), lambda bi, i: (bi, rev(i), 0)),
        out_shape=jax.ShapeDtypeStruct((b, s, D_MODEL), F32),
        scratch_shapes=[state],
        compiler_params=_cparams(("parallel", "arbitrary")),
        name="gla_bwd",
    )(main3, main3, main3, lr3, wgb, bgb)

    fwd = lambda i: i
    return pl.pallas_call(
        _gla_fwd_kernel,
        grid=(b, nc),
        in_specs=specs(fwd) + [pl.BlockSpec((1, c, D_MODEL), lambda bi, i: (bi, i, 5)),
                               pl.BlockSpec((1, c, LR_W), lambda bi, i: (bi, i, 0)),
                               pl.BlockSpec((1, c, D_MODEL), lambda bi, i: (bi, i, 0)),
                               full((LR_W, hk)), full((1, hk)), full((LR_W, hk)), full((1, hk)),
                               full((1, V_DIM_B))],
        out_specs=pl.BlockSpec((1, c, D_MODEL), lambda bi, i: (bi, i, 0)),
        out_shape=jax.ShapeDtypeStruct((b, s, D_MODEL), BF16),
        scratch_shapes=[state],
        compiler_params=_cparams(("parallel", "arbitrary")),
        name="gla_fwd",
    )(main3, main3, main3, main3, lr3, obi, wgf, bgf, wgb, bgb, gn)


def _mix_kernel(x_ref, oa_ref, ob_ref, ga_ref, gb_ref, mod_ref, wa_ref, wb_ref, wo_ref, o_ref):
    ya = jnp.dot(oa_ref[...], wa_ref[...], preferred_element_type=F32)
    yb = jnp.dot(ob_ref[...], wb_ref[...], preferred_element_type=F32)
    sig = lambda r: 1.0 / (1.0 + jnp.exp(-r[...].astype(F32)))
    merged = sig(ga_ref) * ya + sig(gb_ref) * yb
    o_ref[...] = x_ref[...] + mod_ref[0, 2:3, :] * jnp.dot(merged.astype(BF16), wo_ref[...],
                                                           preferred_element_type=F32)


def _mix(x2, oa2, ob2, main2, mod8, wa, wb, wo, seq):
    t = x2.shape[0]
    tm = ROW_TILE
    row = lambda j: pl.BlockSpec((tm, D_MODEL), lambda i: (i, j))
    wspec = pl.BlockSpec((D_MODEL, D_MODEL), lambda i: (0, 0))
    return pl.pallas_call(
        _mix_kernel,
        grid=(t // tm,),
        in_specs=[row(0), row(0), row(0), row(6), row(7),
                  pl.BlockSpec((1, 8, D_MODEL), lambda i: ((i * tm) // seq, 0, 0)),
                  wspec, wspec, wspec],
        out_specs=row(0),
        out_shape=jax.ShapeDtypeStruct((t, D_MODEL), F32),
        compiler_params=_cparams(("parallel",)),
        name="mix",
    )(x2, oa2, ob2, main2, main2, mod8, wa, wb, wo)


def _mlp_kernel(x_ref, mod_ref, g_ref, wu_ref, wd_ref, o_ref):
    x = x_ref[...]
    h = _modulated_norm(x, g_ref[...], mod_ref[0, 3:4, :], mod_ref[0, 4:5, :]).astype(BF16)
    acc = jnp.zeros(x.shape, F32)
    for j in range(D_FF // D_MODEL):
        cs = slice(j * D_MODEL, (j + 1) * D_MODEL)
        u = jnp.maximum(jnp.dot(h, wu_ref[:, cs], preferred_element_type=F32), 0.0)
        acc = acc + jnp.dot((u * u).astype(BF16), wd_ref[cs, :], preferred_element_type=F32)
    o_ref[...] = x + mod_ref[0, 5:6, :] * acc


def _mlp(x2, mod8, g2, wu, wd, seq):
    t = x2.shape[0]
    tm = ROW_TILE
    once = pl.Buffered(1)
    return pl.pallas_call(
        _mlp_kernel,
        grid=(t // tm,),
        in_specs=[pl.BlockSpec((tm, D_MODEL), lambda i: (i, 0)),
                  pl.BlockSpec((1, 8, D_MODEL), lambda i: ((i * tm) // seq, 0, 0)),
                  pl.BlockSpec((1, D_MODEL), lambda i: (0, 0)),
                  pl.BlockSpec((D_MODEL, D_FF), lambda i: (0, 0), pipeline_mode=once),
                  pl.BlockSpec((D_FF, D_MODEL), lambda i: (0, 0), pipeline_mode=once)],
        out_specs=pl.BlockSpec((tm, D_MODEL), lambda i: (i, 0)),
        out_shape=jax.ShapeDtypeStruct((t, D_MODEL), F32),
        compiler_params=_cparams(("parallel",)),
        name="mlp",
    )(x2, mod8, g2, wu, wd)


def _split_w_in(w_in):
    o = np.cumsum([0, 1024, 1024, 1024, 512, 512, 1024, 1024, GATE_RANK, GATE_RANK, 1024, 1024])
    main = jnp.concatenate([w_in[:, o[0]:o[7]], w_in[:, o[9]:o[11]]], axis=1)
    lr = jnp.pad(w_in[:, o[7]:o[9]], ((0, 0), (0, LR_W - 2 * GATE_RANK)))
    return main.astype(BF16), lr.astype(BF16)


def _pad_gate_w(w_gate, row0):
    return jnp.pad(w_gate, ((row0, LR_W - GATE_RANK - row0), (0, 0))).astype(BF16)


def _layer(x, mod8, layer_idx, p):
    b, s, _ = x.shape
    t = b * s
    lambda_init = 0.8 - 0.6 * math.exp(-0.3 * layer_idx)
    x2 = x.reshape(t, D_MODEL)
    row = lambda v: v.reshape(1, -1).astype(F32)

    main2, lr2 = _proj(x2, mod8, row(p["norm1_g"]), p["w_main"], p["w_lr"], s)
    main3 = main2.reshape(b, s, N_MAIN)
    lr3 = lr2.reshape(b, s, LR_W)

    qt, kn, vt = _prep(main3, row(jnp.tile(p["q_norm_g"], 2)), row(jnp.tile(p["k_norm_g"], 2)))
    lamv = jnp.zeros((8, HEAD_W_A), F32).at[0:4, :HEAD_DIM_A].set(
        jnp.stack([p["lam_q1"], p["lam_k1"], p["lam_q2"], p["lam_k2"]]).astype(F32))
    oa = _attn(p["rel_bias"], qt, kn, vt, p["bias_t"], lamv, row(p["subln_g"]), lambda_init)

    ob = _gla(main3, lr3, p["wgf"], row(p["b_gate_f"]), p["wgb"], row(p["b_gate_b"]), row(p["gla_norm_g"]))

    x1 = _mix(x2, oa.reshape(t, D_MODEL), ob.reshape(t, D_MODEL), main2, mod8,
              p["w_branch_a"], p["w_branch_b"], p["w_out"], s)
    y = _mlp(x1, mod8, row(p["norm2_g"]), p["w_up"], p["w_down"], s)
    return y.reshape(b, s, D_MODEL)


def kernel(x_prompt, x_sample, c_prompt, c_sample, rel_bias, w_ada, b_ada, norm1_g, w_in, q_norm_g, k_norm_g, lam_q1, lam_k1, lam_q2, lam_k2, subln_g, w_gate_f, b_gate_f, w_gate_b, b_gate_b, gla_norm_g, w_branch_a, w_branch_b, w_out, norm2_g, w_up, w_down):
    depth = w_in.shape[0]
    nb_p, nb_s = c_prompt.shape[0], c_sample.shape[0]
    rows = -(-(nb_p + nb_s) // 8) * 8
    c_all = jnp.pad(jnp.concatenate([c_prompt, c_sample], axis=0), ((0, rows - nb_p - nb_s), (0, 0)))
    bias_t = _bias_tiles(rel_bias.astype(F32))

    xp, xs = x_prompt, x_sample
    for l in range(depth):
        mod = _ada(c_all, w_ada[l], b_ada[l]).reshape(rows, 6, D_MODEL)
        mod8 = jnp.pad(mod, ((0, 0), (0, 2), (0, 0)))
        w_main, w_lr = _split_w_in(w_in[l])
        p = dict(rel_bias=rel_bias.astype(F32), bias_t=bias_t, norm1_g=norm1_g[l], w_main=w_main, w_lr=w_lr,
                 q_norm_g=q_norm_g[l], k_norm_g=k_norm_g[l], lam_q1=lam_q1[l], lam_k1=lam_k1[l],
                 lam_q2=lam_q2[l], lam_k2=lam_k2[l], subln_g=subln_g[l],
                 wgf=_pad_gate_w(w_gate_f[l], 0), b_gate_f=b_gate_f[l],
                 wgb=_pad_gate_w(w_gate_b[l], GATE_RANK), b_gate_b=b_gate_b[l],
                 gla_norm_g=gla_norm_g[l], w_branch_a=w_branch_a[l].astype(BF16),
                 w_branch_b=w_branch_b[l].astype(BF16), w_out=w_out[l].astype(BF16),
                 norm2_g=norm2_g[l], w_up=w_up[l].astype(BF16), w_down=w_down[l].astype(BF16))
        xp = _layer(xp, mod8[:nb_p], l, p)
        xs = _layer(xs, mod8[nb_p:nb_p + nb_s], l, p)
    return (xp, xs)
```

```python
import functools
import math

import jax
import jax.numpy as jnp
import numpy as np
from jax import lax
from jax.experimental import pallas as pl
from jax.experimental.pallas import tpu as pltpu

F32 = jnp.float32
BF16 = jnp.bfloat16

D_MODEL = 1024
HEAD_DIM_A = 64
N_HEADS_A = 8
HEAD_W_A = 2 * HEAD_DIM_A
N_HEADS_B = 4
KEY_DIM_B = 128
V_DIM_B = 256
GATE_RANK = 16
GATE_NORM = 16.0
N_BUCKETS = 32
MAX_DISTANCE = 128
D_FF = 4 * D_MODEL
EPS = 1e-6
LOG2E = math.log2(math.e)

N_MAIN = 8 * D_MODEL
LR_W = 128

VMEM_LIMIT = 56 * 1024 * 1024
ROW_TILE = 512
ATT_TILE = 256
ATT_BLOCK = 4
EXP2_RANGE = 100.0
GLA_CHUNK = 128


def _cparams(sem):
    return pltpu.CompilerParams(dimension_semantics=sem, vmem_limit_bytes=VMEM_LIMIT)


def _ada_kernel(c_ref, w_ref, b_ref, o_ref):
    c = c_ref[...]
    a = c / (1.0 + jnp.exp(-c))
    o_ref[...] = jnp.dot(a, w_ref[...], preferred_element_type=F32,
                         precision=lax.Precision.HIGHEST) + b_ref[...]


def _ada(c_all, w_ada, b_ada):
    r = c_all.shape[0]
    tn = 1024
    return pl.pallas_call(
        _ada_kernel,
        grid=(6 * D_MODEL // tn,),
        in_specs=[pl.BlockSpec((r, D_MODEL), lambda n: (0, 0)),
                  pl.BlockSpec((D_MODEL, tn), lambda n: (0, n)),
                  pl.BlockSpec((1, tn), lambda n: (0, n))],
        out_specs=pl.BlockSpec((r, tn), lambda n: (0, n)),
        out_shape=jax.ShapeDtypeStruct((r, 6 * D_MODEL), F32),
        compiler_params=_cparams(("parallel",)),
        name="ada",
    )(c_all, w_ada, b_ada.reshape(1, -1))


def _modulated_norm(x, g, shift, scale):
    ms = jnp.mean(x * x, axis=-1, keepdims=True)
    return x * lax.rsqrt(ms + EPS) * g * (1.0 + scale) + shift


def _proj_kernel(x_ref, mod_ref, g_ref, w_ref, wlr_ref, o_ref, lr_ref, h_scr):
    @pl.when(pl.program_id(1) == 0)
    def _():
        h = _modulated_norm(x_ref[...], g_ref[...], mod_ref[0, 0:1, :], mod_ref[0, 1:2, :])
        hb = h.astype(BF16)
        h_scr[...] = hb
        lr_ref[...] = jnp.dot(hb, wlr_ref[...], preferred_element_type=F32)

    o_ref[...] = jnp.dot(h_scr[...], w_ref[...], preferred_element_type=F32).astype(o_ref.dtype)


def _proj(x2, mod8, g1, w_main, w_lr, seq):
    t = x2.shape[0]
    tm, tn = ROW_TILE, 1024
    return pl.pallas_call(
        _proj_kernel,
        grid=(t // tm, N_MAIN // tn),
        in_specs=[pl.BlockSpec((tm, D_MODEL), lambda i, n: (i, 0)),
                  pl.BlockSpec((1, 8, D_MODEL), lambda i, n: ((i * tm) // seq, 0, 0)),
                  pl.BlockSpec((1, D_MODEL), lambda i, n: (0, 0)),
                  pl.BlockSpec((D_MODEL, tn), lambda i, n: (0, n)),
                  pl.BlockSpec((D_MODEL, LR_W), lambda i, n: (0, 0))],
        out_specs=[pl.BlockSpec((tm, tn), lambda i, n: (i, n)),
                   pl.BlockSpec((tm, LR_W), lambda i, n: (i, 0))],
        out_shape=[jax.ShapeDtypeStruct((t, N_MAIN), BF16),
                   jax.ShapeDtypeStruct((t, LR_W), F32)],
        scratch_shapes=[pltpu.VMEM((tm, D_MODEL), BF16)],
        compiler_params=_cparams(("parallel", "arbitrary")),
        name="proj",
    )(x2, mod8, g1, w_main, w_lr)


def _pair_rms(x, g):
    lane = lax.broadcasted_iota(jnp.int32, x.shape, 1)
    x2 = x * x
    s_all = jnp.sum(x2, axis=-1, keepdims=True)
    s_lo = jnp.sum(jnp.where(lane < HEAD_DIM_A, x2, 0.0), axis=-1, keepdims=True)
    ss = jnp.where(lane < HEAD_DIM_A, s_lo, s_all - s_lo)
    return x * lax.rsqrt(ss * (1.0 / HEAD_DIM_A) + EPS) * g


def _prep_kernel(q_ref, k_ref, v_ref, qg_ref, kg_ref, qt_ref, kn_ref, vt_ref):
    qn = _pair_rms(q_ref[0].astype(F32), qg_ref[...]) * (HEAD_DIM_A ** -0.5 * LOG2E)
    qt_ref[0, 0] = qn.T.astype(BF16)
    kn_ref[0] = _pair_rms(k_ref[0].astype(F32), kg_ref[...]).astype(BF16)
    vt_ref[0, 0] = v_ref[0].astype(F32).T.astype(BF16)


def _prep(main3, qg2, kg2):
    b, s, _ = main3.shape
    ts = 512
    tile = lambda off: pl.BlockSpec((1, ts, HEAD_W_A), lambda bi, si, h: (bi, si, off + h))
    tspec = pl.BlockSpec((1, 1, HEAD_W_A, ts), lambda bi, si, h: (bi, h, 0, si))
    gspec = pl.BlockSpec((1, HEAD_W_A), lambda bi, si, h: (0, 0))
    return pl.pallas_call(
        _prep_kernel,
        grid=(b, s // ts, N_HEADS_A),
        in_specs=[tile(0), tile(N_HEADS_A), tile(2 * N_HEADS_A), gspec, gspec],
        out_specs=[tspec, tile(0), tspec],
        out_shape=[jax.ShapeDtypeStruct((b, N_HEADS_A, HEAD_W_A, s), BF16),
                   jax.ShapeDtypeStruct((b, s, D_MODEL), BF16),
                   jax.ShapeDtypeStruct((b, N_HEADS_A, HEAD_W_A, s), BF16)],
        compiler_params=_cparams(("parallel", "parallel", "parallel")),
        name="prep",
    )(main3, main3, main3, qg2, kg2)


def _t5_bucket(rel):
    nb = N_BUCKETS // 2
    max_exact = nb // 2
    ret = (rel > 0).astype(jnp.int32) * nb
    n = jnp.abs(rel)
    nf = jnp.maximum(n, 1).astype(F32)
    large = max_exact + (jnp.log(nf / max_exact) / math.log(MAX_DISTANCE / max_exact)
                         * (nb - max_exact)).astype(jnp.int32)
    large = jnp.minimum(large, nb - 1)
    return ret + jnp.where(n < max_exact, n, large)


def _bias_kernel(tbl_ref, bkt_ref, o_ref):
    h = pl.program_id(0)
    bkt = bkt_ref[...]
    acc = jnp.zeros(bkt.shape, F32)
    for i in range(N_BUCKETS):
        acc = jnp.where(bkt == i, tbl_ref[i, h], acc)
    o_ref[0] = acc * LOG2E


def _bias_tiles(rel_bias):
    t = ATT_TILE
    kk = jnp.arange(t, dtype=jnp.int32)[:, None]
    qq = jnp.arange(t, dtype=jnp.int32)[None, :]
    rel = jnp.stack([(d - 1) * t + kk - qq for d in range(3)])
    return pl.pallas_call(
        _bias_kernel,
        grid=(N_HEADS_A,),
        in_specs=[pl.BlockSpec(memory_space=pltpu.SMEM),
                  pl.BlockSpec((3, t, t), lambda h: (0, 0, 0))],
        out_specs=pl.BlockSpec((1, 3, t, t), lambda h: (h, 0, 0, 0)),
        out_shape=jax.ShapeDtypeStruct((N_HEADS_A, 3, t, t), F32),
        compiler_params=_cparams(("parallel",)),
        name="bias",
    )(rel_bias, _t5_bucket(rel))


def _attn_kernel(tbl_ref, qt_ref, k_ref, vt_ref, bias_ref, kg_ref, lamv_ref, subg_ref, o_ref,
                 w_scr, m_scr, l_scr, acc_scr, l8a_scr, l8b_scr, acc2_scr, *, nk, lambda_init):
    t = ATT_TILE
    blk = min(ATT_BLOCK, nk)
    h = pl.program_id(1)
    qi = pl.program_id(2)

    q = qt_ref[0, 0]
    z = jnp.zeros((HEAD_DIM_A, t), BF16)
    w_scr[:, :t] = jnp.concatenate([q[:HEAD_DIM_A], z], axis=0)
    w_scr[:, t:] = jnp.concatenate([z, q[HEAD_DIM_A:]], axis=0)

    far_left = tbl_ref[N_BUCKETS // 2 - 1, h] * LOG2E
    far_right = tbl_ref[N_BUCKETS - 1, h] * LOG2E

    wf = w_scr[...].astype(F32)
    qn = jnp.sqrt(jnp.max(jnp.sum(wf * wf, axis=0, keepdims=True), axis=1, keepdims=True))
    kb = jnp.max(jnp.abs(kg_ref[...]), axis=1, keepdims=True) * (1.01 * HEAD_DIM_A ** 0.5)
    babs = jnp.abs(tbl_ref[0, h])
    for i in range(1, N_BUCKETS):
        babs = jnp.maximum(babs, jnp.abs(tbl_ref[i, h]))
    fast = (qn * kb)[0, 0] + babs * LOG2E <= EXP2_RANGE

    def near_bias(d):
        b = bias_ref[0, d]
        return jnp.concatenate([b, b], axis=1)

    def fast_step(k0, nkeys, bias, l_ref, acc_ref):
        s = jnp.dot(k_ref[0, pl.ds(k0, nkeys), :], w_scr[...], preferred_element_type=F32)
        if bias is not None:
            s = s + bias
        p = jnp.exp2(s)
        l_ref[...] += jnp.sum(p.reshape(nkeys // 8, 8, 2 * t), axis=0)
        acc_ref[...] += jnp.dot(vt_ref[0, 0, :, pl.ds(k0, nkeys)], p.astype(BF16),
                                preferred_element_type=F32)

    def far_side(first, count, l_ref, acc_ref):
        nbig = count // blk

        def big(i, c):
            fast_step(pl.multiple_of((first + i * blk) * t, t), blk * t, None, l_ref, acc_ref)
            return c

        def small(j, c):
            fast_step(pl.multiple_of(j * t, t), t, None, l_ref, acc_ref)
            return c

        lax.fori_loop(0, nbig, big, 0)
        lax.fori_loop(first + nbig * blk, first + count, small, 0)

    @pl.when(fast)
    def _():
        for r in (l8a_scr, l8b_scr, acc_scr, acc2_scr):
            r[...] = jnp.zeros(r.shape, F32)
        far_side(0, jnp.maximum(qi - 1, 0), l8a_scr, acc_scr)
        sc = jnp.exp2(jnp.full((1, 2 * t), far_left, F32))
        acc_scr[...] *= sc
        l8a_scr[...] *= sc
        for d in range(3):
            kj = qi + d - 1

            @pl.when(jnp.logical_and(kj >= 0, kj < nk))
            def _(kj=kj, d=d):
                fast_step(pl.multiple_of(kj * t, t), t, near_bias(d), l8a_scr, acc_scr)

        far_side(qi + 2, jnp.maximum(nk - qi - 2, 0), l8b_scr, acc2_scr)
        sc = jnp.exp2(jnp.full((1, 2 * t), far_right, F32))
        acc_scr[...] += sc * acc2_scr[...]
        l_scr[...] = jnp.sum(l8a_scr[...] + sc * l8b_scr[...], axis=0, keepdims=True)

    def online_step(kj, bias):
        k0 = pl.multiple_of(kj * t, t)
        s = jnp.dot(k_ref[0, pl.ds(k0, t), :], w_scr[...], preferred_element_type=F32) + bias
        m_old = m_scr[...]
        m_new = jnp.maximum(m_old, jnp.max(s, axis=0, keepdims=True))
        alpha = jnp.exp2(m_old - m_new)
        p = jnp.exp2(s - m_new)
        l_scr[...] = alpha * l_scr[...] + jnp.sum(p, axis=0, keepdims=True)
        pv = jnp.dot(vt_ref[0, 0, :, pl.ds(k0, t)], p.astype(BF16), preferred_element_type=F32)
        acc_scr[...] = alpha * acc_scr[...] + pv
        m_scr[...] = m_new

    @pl.when(jnp.logical_not(fast))
    def _():
        m_scr[...] = jnp.full(m_scr.shape, -jnp.inf, F32)
        l_scr[...] = jnp.zeros(l_scr.shape, F32)
        acc_scr[...] = jnp.zeros(acc_scr.shape, F32)

        def left_body(kj, c):
            online_step(kj, far_left)
            return c

        def right_body(kj, c):
            online_step(kj, far_right)
            return c

        lax.fori_loop(0, jnp.maximum(qi - 1, 0), left_body, 0)
        for d in range(3):
            kj = qi + d - 1

            @pl.when(jnp.logical_and(kj >= 0, kj < nk))
            def _(kj=kj, d=d):
                online_step(kj, near_bias(d))

        lax.fori_loop(qi + 2, nk, right_body, 0)

    lv = lamv_ref[...]
    lam = (jnp.exp(jnp.sum(lv[0:1] * lv[1:2], axis=-1, keepdims=True))
           - jnp.exp(jnp.sum(lv[2:3] * lv[3:4], axis=-1, keepdims=True)) + lambda_init)
    acc = acc_scr[...]
    inv = 1.0 / l_scr[...]
    o = acc[:, :t] * inv[:, :t] - lam * (acc[:, t:] * inv[:, t:])
    ms = jnp.mean(o * o, axis=0, keepdims=True)
    y = (o * lax.rsqrt(ms + EPS)).T
    o_ref[0] = (y * subg_ref[...] * (1.0 - lambda_init)).astype(o_ref.dtype)


def _attn(rel_bias, qt, kn, vt, bias_t, kg2, lamv, subg, lambda_init):
    b, s, _ = kn.shape
    t = ATT_TILE
    nk = s // t
    kern = functools.partial(_attn_kernel, nk=nk, lambda_init=lambda_init)
    const = lambda shape: pl.BlockSpec(shape, lambda bi, h, qi: (0,) * len(shape))
    return pl.pallas_call(
        kern,
        grid=(b, N_HEADS_A, s // t),
        in_specs=[pl.BlockSpec(memory_space=pltpu.SMEM),
                  pl.BlockSpec((1, 1, HEAD_W_A, t), lambda bi, h, qi: (bi, h, 0, qi)),
                  pl.BlockSpec((1, s, HEAD_W_A), lambda bi, h, qi: (bi, 0, h)),
                  pl.BlockSpec((1, 1, HEAD_W_A, s), lambda bi, h, qi: (bi, h, 0, 0)),
                  pl.BlockSpec((1, 3, t, t), lambda bi, h, qi: (h, 0, 0, 0)),
                  const((1, HEAD_W_A)), const((8, HEAD_W_A)), const((1, HEAD_W_A))],
        out_specs=pl.BlockSpec((1, t, HEAD_W_A), lambda bi, h, qi: (bi, qi, h)),
        out_shape=jax.ShapeDtypeStruct((b, s, D_MODEL), BF16),
        scratch_shapes=[pltpu.VMEM((HEAD_W_A, 2 * t), BF16),
                        pltpu.VMEM((1, 2 * t), F32),
                        pltpu.VMEM((1, 2 * t), F32),
                        pltpu.VMEM((HEAD_W_A, 2 * t), F32),
                        pltpu.VMEM((8, 2 * t), F32),
                        pltpu.VMEM((8, 2 * t), F32),
                        pltpu.VMEM((HEAD_W_A, 2 * t), F32)],
        compiler_params=_cparams(("parallel", "parallel", "parallel")),
        name="attn",
    )(rel_bias, qt, kn, vt, bias_t, kg2, lamv, subg)


def _log_gate(lr, wg_ref, bg_ref):
    x = jnp.dot(lr.astype(BF16), wg_ref[...], preferred_element_type=F32) + bg_ref[...]
    return (jnp.minimum(x, 0.0) - jnp.log1p(jnp.exp(-jnp.abs(x)))) * (1.0 / GATE_NORM)


def _gla_bwd_kernel(q_ref, k_ref, v_ref, lr_ref, wg_ref, bg_ref, o_ref, s_scr):
    c = GLA_CHUNK

    @pl.when(pl.program_id(1) == 0)
    def _():
        s_scr[...] = jnp.zeros(s_scr.shape, F32)

    g = _log_gate(lr_ref[0], wg_ref, bg_ref)
    row = lax.broadcasted_iota(jnp.int32, g.shape, 0)
    suf = g
    sh = 1
    while sh < c:
        suf = suf + jnp.where(row < c - sh, pltpu.roll(suf, c - sh, 0), 0.0)
        sh *= 2
    tot = suf[0:1, :]
    for h in range(N_HEADS_B):
        sl = slice(h * KEY_DIM_B, (h + 1) * KEY_DIM_B)
        vs = slice(h * V_DIM_B, (h + 1) * V_DIM_B)
        q = q_ref[0][:, sl].astype(F32) * KEY_DIM_B ** -0.5
        k = k_ref[0][:, sl].astype(F32)
        qd = (q * jnp.exp(suf[:, sl])).astype(BF16)
        kd = k * jnp.exp(tot[:, sl] - suf[:, sl])
        st = s_scr[h]
        o_ref[0, :, vs] = jnp.dot(qd, st.astype(BF16), preferred_element_type=F32)
        dec = jnp.broadcast_to(jnp.exp(tot[:, sl]), (c, KEY_DIM_B)).T[:, 0:1]
        s_scr[h] = st * dec + jnp.dot(kd.T.astype(BF16), v_ref[0][:, vs], preferred_element_type=F32)


def _gla_fwd_kernel(q_ref, k_ref, v_ref, og_ref, lr_ref, obi_ref, wgf_ref, bgf_ref, wgb_ref, bgb_ref,
                    gn_ref, o_ref, s_scr):
    c = GLA_CHUNK

    @pl.when(pl.program_id(1) == 0)
    def _():
        s_scr[...] = jnp.zeros(s_scr.shape, F32)

    lr = lr_ref[0]
    gf_all = _log_gate(lr, wgf_ref, bgf_ref)
    gb_all = _log_gate(lr, wgb_ref, bgb_ref)
    ri = lax.broadcasted_iota(jnp.int32, (c, c), 0)
    ci = lax.broadcasted_iota(jnp.int32, (c, c), 1)
    xr = jnp.bitwise_xor(ri, ci)
    nt = (((1,), (1,)), ((), ()))
    for h in range(N_HEADS_B):
        sl = slice(h * KEY_DIM_B, (h + 1) * KEY_DIM_B)
        vs = slice(h * V_DIM_B, (h + 1) * V_DIM_B)
        q = q_ref[0][:, sl].astype(F32) * KEY_DIM_B ** -0.5
        k = k_ref[0][:, sl].astype(F32)
        v = v_ref[0][:, vs]
        pre = gf_all[:, sl]
        tf = pre
        suf = gb_all[:, sl]
        tb = suf
        a = jnp.where(ri == ci, 2.0 * lax.dot_general(q.astype(BF16), k.astype(BF16), nt,
                                                      preferred_element_type=F32), 0.0)
        blk = 1
        while blk < c:
            level = jnp.logical_and(xr >= blk, xr < 2 * blk)
            rf = lax.dot_general((q * jnp.exp(pre)).astype(BF16), (k * jnp.exp(tf - pre)).astype(BF16),
                                 nt, preferred_element_type=F32)
            a = jnp.where(jnp.logical_and(level, ri > ci), rf, a)
            rb = lax.dot_general((q * jnp.exp(suf)).astype(BF16), (k * jnp.exp(tb - suf)).astype(BF16),
                                 nt, preferred_element_type=F32)
            a = jnp.where(jnp.logical_and(level, ri < ci), rb, a)
            hi = jnp.bitwise_and(ri, blk) != 0
            tf_dn, tf_up = pltpu.roll(tf, blk, 0), pltpu.roll(tf, c - blk, 0)
            tb_dn, tb_up = pltpu.roll(tb, blk, 0), pltpu.roll(tb, c - blk, 0)
            pre = pre + jnp.where(hi, tf_dn, 0.0)
            suf = suf + jnp.where(hi, 0.0, tb_up)
            tf = tf + jnp.where(hi, tf_dn, tf_up)
            tb = tb + jnp.where(hi, tb_dn, tb_up)
            blk *= 2
        st = s_scr[h]
        o = (jnp.dot(a.astype(BF16), v, preferred_element_type=F32)
             + jnp.dot((q * jnp.exp(pre)).astype(BF16), st.astype(BF16), preferred_element_type=F32)
             + obi_ref[0, :, vs])
        kd = k * jnp.exp(tf - pre)
        s_scr[h] = st * jnp.exp(tf).T[:, 0:1] + jnp.dot(kd.T.astype(BF16), v, preferred_element_type=F32)
        ms = jnp.mean(o * o, axis=-1, keepdims=True)
        og = og_ref[0][:, vs].astype(F32)
        o_ref[0, :, vs] = (o * lax.rsqrt(ms + EPS) * gn_ref[...] * (og / (1.0 + jnp.exp(-og)))
                           ).astype(o_ref.dtype)


def _gla(main3, lr3, wgf, bgf, wgb, bgb, gn):
    b, s, _ = main3.shape
    c = GLA_CHUNK
    nc = s // c
    hk = N_HEADS_B * KEY_DIM_B
    state = pltpu.VMEM((N_HEADS_B, KEY_DIM_B, V_DIM_B), F32)

    def specs(cidx):
        return [pl.BlockSpec((1, c, hk), lambda bi, i: (bi, cidx(i), 3 * D_MODEL // hk)),
                pl.BlockSpec((1, c, hk), lambda bi, i: (bi, cidx(i), 3 * D_MODEL // hk + 1)),
                pl.BlockSpec((1, c, D_MODEL), lambda bi, i: (bi, cidx(i), 4))]

    full = lambda shape: pl.BlockSpec(shape, lambda bi, i: (0,) * len(shape))
    rev = lambda i: nc - 1 - i
    obi = pl.pallas_call(
        _gla_bwd_kernel,
        grid=(b, nc),
        in_specs=specs(rev) + [pl.BlockSpec((1, c, LR_W), lambda bi, i: (bi, rev(i), 0)),
                               full((LR_W, hk)), full((1, hk))],
        out_specs=pl.BlockSpec((1, c, D_MODEL), lambda bi, i: (bi, rev(i), 0)),
        out_shape=jax.ShapeDtypeStruct((b, s, D_MODEL), F32),
        scratch_shapes=[state],
        compiler_params=_cparams(("parallel", "arbitrary")),
        name="gla_bwd",
    )(main3, main3, main3, lr3, wgb, bgb)

    fwd = lambda i: i
    return pl.pallas_call(
        _gla_fwd_kernel,
        grid=(b, nc),
        in_specs=specs(fwd) + [pl.BlockSpec((1, c, D_MODEL), lambda bi, i: (bi, i, 5)),
                               pl.BlockSpec((1, c, LR_W), lambda bi, i: (bi, i, 0)),
                               pl.BlockSpec((1, c, D_MODEL), lambda bi, i: (bi, i, 0)),
                               full((LR_W, hk)), full((1, hk)), full((LR_W, hk)), full((1, hk)),
                               full((1, V_DIM_B))],
        out_specs=pl.BlockSpec((1, c, D_MODEL), lambda bi, i: (bi, i, 0)),
        out_shape=jax.ShapeDtypeStruct((b, s, D_MODEL), BF16),
        scratch_shapes=[state],
        compiler_params=_cparams(("parallel", "arbitrary")),
        name="gla_fwd",
    )(main3, main3, main3, main3, lr3, obi, wgf, bgf, wgb, bgb, gn)


def _mix_kernel(x_ref, oa_ref, ob_ref, ga_ref, gb_ref, mod_ref, wa_ref, wb_ref, wo_ref, o_ref):
    ya = jnp.dot(oa_ref[...], wa_ref[...], preferred_element_type=F32)
    yb = jnp.dot(ob_ref[...], wb_ref[...], preferred_element_type=F32)
    sig = lambda r: 1.0 / (1.0 + jnp.exp(-r[...].astype(F32)))
    merged = sig(ga_ref) * ya + sig(gb_ref) * yb
    o_ref[...] = x_ref[...] + mod_ref[0, 2:3, :] * jnp.dot(merged.astype(BF16), wo_ref[...],
                                                           preferred_element_type=F32)


def _mix(x2, oa2, ob2, main2, mod8, wa, wb, wo, seq):
    t = x2.shape[0]
    tm = ROW_TILE
    row = lambda j: pl.BlockSpec((tm, D_MODEL), lambda i: (i, j))
    wspec = pl.BlockSpec((D_MODEL, D_MODEL), lambda i: (0, 0))
    return pl.pallas_call(
        _mix_kernel,
        grid=(t // tm,),
        in_specs=[row(0), row(0), row(0), row(6), row(7),
                  pl.BlockSpec((1, 8, D_MODEL), lambda i: ((i * tm) // seq, 0, 0)),
                  wspec, wspec, wspec],
        out_specs=row(0),
        out_shape=jax.ShapeDtypeStruct((t, D_MODEL), F32),
        compiler_params=_cparams(("parallel",)),
        name="mix",
    )(x2, oa2, ob2, main2, main2, mod8, wa, wb, wo)


def _mlp_kernel(x_ref, mod_ref, g_ref, wu_ref, wd_ref, o_ref):
    x = x_ref[...]
    h = _modulated_norm(x, g_ref[...], mod_ref[0, 3:4, :], mod_ref[0, 4:5, :]).astype(BF16)
    acc = jnp.zeros(x.shape, F32)
    for j in range(D_FF // D_MODEL):
        cs = slice(j * D_MODEL, (j + 1) * D_MODEL)
        u = jnp.maximum(jnp.dot(h, wu_ref[:, cs], preferred_element_type=F32), 0.0)
        acc = acc + jnp.dot((u * u).astype(BF16), wd_ref[cs, :], preferred_element_type=F32)
    o_ref[...] = x + mod_ref[0, 5:6, :] * acc


def _mlp(x2, mod8, g2, wu, wd, seq):
    t = x2.shape[0]
    tm = ROW_TILE
    once = pl.Buffered(1)
    return pl.pallas_call(
        _mlp_kernel,
        grid=(t // tm,),
        in_specs=[pl.BlockSpec((tm, D_MODEL), lambda i: (i, 0)),
                  pl.BlockSpec((1, 8, D_MODEL), lambda i: ((i * tm) // seq, 0, 0)),
                  pl.BlockSpec((1, D_MODEL), lambda i: (0, 0)),
                  pl.BlockSpec((D_MODEL, D_FF), lambda i: (0, 0), pipeline_mode=once),
                  pl.BlockSpec((D_FF, D_MODEL), lambda i: (0, 0), pipeline_mode=once)],
        out_specs=pl.BlockSpec((tm, D_MODEL), lambda i: (i, 0)),
        out_shape=jax.ShapeDtypeStruct((t, D_MODEL), F32),
        compiler_params=_cparams(("parallel",)),
        name="mlp",
    )(x2, mod8, g2, wu, wd)


def _split_w_in(w_in):
    o = np.cumsum([0, 1024, 1024, 1024, 512, 512, 1024, 1024, GATE_RANK, GATE_RANK, 1024, 1024])
    main = jnp.concatenate([w_in[:, o[0]:o[7]], w_in[:, o[9]:o[11]]], axis=1)
    lr = jnp.pad(w_in[:, o[7]:o[9]], ((0, 0), (0, LR_W - 2 * GATE_RANK)))
    return main.astype(BF16), lr.astype(BF16)


def _pad_gate_w(w_gate, row0):
    return jnp.pad(w_gate, ((row0, LR_W - GATE_RANK - row0), (0, 0))).astype(BF16)


def _layer(x, mod8, layer_idx, p):
    b, s, _ = x.shape
    t = b * s
    lambda_init = 0.8 - 0.6 * math.exp(-0.3 * layer_idx)
    x2 = x.reshape(t, D_MODEL)
    row = lambda v: v.reshape(1, -1).astype(F32)

    main2, lr2 = _proj(x2, mod8, row(p["norm1_g"]), p["w_main"], p["w_lr"], s)
    main3 = main2.reshape(b, s, N_MAIN)
    lr3 = lr2.reshape(b, s, LR_W)

    kg2 = row(jnp.tile(p["k_norm_g"], 2))
    qt, kn, vt = _prep(main3, row(jnp.tile(p["q_norm_g"], 2)), kg2)
    lamv = jnp.zeros((8, HEAD_W_A), F32).at[0:4, :HEAD_DIM_A].set(
        jnp.stack([p["lam_q1"], p["lam_k1"], p["lam_q2"], p["lam_k2"]]).astype(F32))
    oa = _attn(p["rel_bias"], qt, kn, vt, p["bias_t"], kg2, lamv, row(p["subln_g"]), lambda_init)

    ob = _gla(main3, lr3, p["wgf"], row(p["b_gate_f"]), p["wgb"], row(p["b_gate_b"]), row(p["gla_norm_g"]))

    x1 = _mix(x2, oa.reshape(t, D_MODEL), ob.reshape(t, D_MODEL), main2, mod8,
              p["w_branch_a"], p["w_branch_b"], p["w_out"], s)
    y = _mlp(x1, mod8, row(p["norm2_g"]), p["w_up"], p["w_down"], s)
    return y.reshape(b, s, D_MODEL)


def kernel(x_prompt, x_sample, c_prompt, c_sample, rel_bias, w_ada, b_ada, norm1_g, w_in, q_norm_g, k_norm_g, lam_q1, lam_k1, lam_q2, lam_k2, subln_g, w_gate_f, b_gate_f, w_gate_b, b_gate_b, gla_norm_g, w_branch_a, w_branch_b, w_out, norm2_g, w_up, w_down):
    depth = w_in.shape[0]
    nb_p, nb_s = c_prompt.shape[0], c_sample.shape[0]
    rows = -(-(nb_p + nb_s) // 8) * 8
    c_all = jnp.pad(jnp.concatenate([c_prompt, c_sample], axis=0), ((0, rows - nb_p - nb_s), (0, 0)))
    bias_t = _bias_tiles(rel_bias.astype(F32))

    xp, xs = x_prompt, x_sample
    for l in range(depth):
        mod = _ada(c_all, w_ada[l], b_ada[l]).reshape(rows, 6, D_MODEL)
        mod8 = jnp.pad(mod, ((0, 0), (0, 2), (0, 0)))
        w_main, w_lr = _split_w_in(w_in[l])
        p = dict(rel_bias=rel_bias.astype(F32), bias_t=bias_t, norm1_g=norm1_g[l], w_main=w_main, w_lr=w_lr,
                 q_norm_g=q_norm_g[l], k_norm_g=k_norm_g[l], lam_q1=lam_q1[l], lam_k1=lam_k1[l],
                 lam_q2=lam_q2[l], lam_k2=lam_k2[l], subln_g=subln_g[l],
                 wgf=_pad_gate_w(w_gate_f[l], 0), b_gate_f=b_gate_f[l],
                 wgb=_pad_gate_w(w_gate_b[l], GATE_RANK), b_gate_b=b_gate_b[l],
                 gla_norm_g=gla_norm_g[l], w_branch_a=w_branch_a[l].astype(BF16),
                 w_branch_b=w_branch_b[l].astype(BF16), w_out=w_out[l].astype(BF16),
                 norm2_g=norm2_g[l], w_up=w_up[l].astype(BF16), w_down=w_down[l].astype(BF16))
        xp = _layer(xp, mod8[:nb_p], l, p)
        xs = _layer(xs, mod8[nb_p:nb_p + nb_s], l, p)
    return (xp, xs)
```

```python
import functools
import math

import jax
import jax.numpy as jnp
import numpy as np
from jax import lax
from jax.experimental import pallas as pl
from jax.experimental.pallas import tpu as pltpu

F32 = jnp.float32
BF16 = jnp.bfloat16

D_MODEL = 1024
HEAD_DIM_A = 64
N_HEADS_A = 8
HEAD_W_A = 2 * HEAD_DIM_A
N_HEADS_B = 4
KEY_DIM_B = 128
V_DIM_B = 256
GATE_RANK = 16
GATE_NORM = 16.0
N_BUCKETS = 32
MAX_DISTANCE = 128
D_FF = 4 * D_MODEL
EPS = 1e-6
LOG2E = math.log2(math.e)

N_MAIN = 8 * D_MODEL
LR_W = 128

VMEM_LIMIT = 56 * 1024 * 1024
ROW_TILE = 512
ATT_TILE = 256
ATT_BLOCK = 4
NEAR_TILES = 5
EXP2_RANGE = 100.0
GLA_CHUNK = 128


def _cparams(sem):
    return pltpu.CompilerParams(dimension_semantics=sem, vmem_limit_bytes=VMEM_LIMIT)


def _ada_kernel(c_ref, w_ref, b_ref, o_ref):
    c = c_ref[...]
    a = c / (1.0 + jnp.exp(-c))
    o_ref[...] = jnp.dot(a, w_ref[...], preferred_element_type=F32,
                         precision=lax.Precision.HIGHEST) + b_ref[...]


def _ada(c_all, w_ada, b_ada):
    r = c_all.shape[0]
    tn = 1024
    return pl.pallas_call(
        _ada_kernel,
        grid=(6 * D_MODEL // tn,),
        in_specs=[pl.BlockSpec((r, D_MODEL), lambda n: (0, 0)),
                  pl.BlockSpec((D_MODEL, tn), lambda n: (0, n)),
                  pl.BlockSpec((1, tn), lambda n: (0, n))],
        out_specs=pl.BlockSpec((r, tn), lambda n: (0, n)),
        out_shape=jax.ShapeDtypeStruct((r, 6 * D_MODEL), F32),
        compiler_params=_cparams(("parallel",)),
        name="ada",
    )(c_all, w_ada, b_ada.reshape(1, -1))


def _modulated_norm(x, g, shift, scale):
    ms = jnp.mean(x * x, axis=-1, keepdims=True)
    return x * lax.rsqrt(ms + EPS) * g * (1.0 + scale) + shift


def _proj_kernel(x_ref, mod_ref, g_ref, w_ref, wlr_ref, o_ref, lr_ref, h_scr):
    @pl.when(pl.program_id(1) == 0)
    def _():
        h = _modulated_norm(x_ref[...], g_ref[...], mod_ref[0, 0:1, :], mod_ref[0, 1:2, :])
        hb = h.astype(BF16)
        h_scr[...] = hb
        lr_ref[...] = jnp.dot(hb, wlr_ref[...], preferred_element_type=F32)

    o_ref[...] = jnp.dot(h_scr[...], w_ref[...], preferred_element_type=F32).astype(o_ref.dtype)


def _proj(x2, mod8, g1, w_main, w_lr, seq):
    t = x2.shape[0]
    tm, tn = ROW_TILE, 1024
    return pl.pallas_call(
        _proj_kernel,
        grid=(t // tm, N_MAIN // tn),
        in_specs=[pl.BlockSpec((tm, D_MODEL), lambda i, n: (i, 0)),
                  pl.BlockSpec((1, 8, D_MODEL), lambda i, n: ((i * tm) // seq, 0, 0)),
                  pl.BlockSpec((1, D_MODEL), lambda i, n: (0, 0)),
                  pl.BlockSpec((D_MODEL, tn), lambda i, n: (0, n)),
                  pl.BlockSpec((D_MODEL, LR_W), lambda i, n: (0, 0))],
        out_specs=[pl.BlockSpec((tm, tn), lambda i, n: (i, n)),
                   pl.BlockSpec((tm, LR_W), lambda i, n: (i, 0))],
        out_shape=[jax.ShapeDtypeStruct((t, N_MAIN), BF16),
                   jax.ShapeDtypeStruct((t, LR_W), F32)],
        scratch_shapes=[pltpu.VMEM((tm, D_MODEL), BF16)],
        compiler_params=_cparams(("parallel", "arbitrary")),
        name="proj",
    )(x2, mod8, g1, w_main, w_lr)


def _pair_rms(x, g):
    lane = lax.broadcasted_iota(jnp.int32, x.shape, 1)
    x2 = x * x
    s_all = jnp.sum(x2, axis=-1, keepdims=True)
    s_lo = jnp.sum(jnp.where(lane < HEAD_DIM_A, x2, 0.0), axis=-1, keepdims=True)
    ss = jnp.where(lane < HEAD_DIM_A, s_lo, s_all - s_lo)
    return x * lax.rsqrt(ss * (1.0 / HEAD_DIM_A) + EPS) * g


def _prep_kernel(q_ref, k_ref, v_ref, qg_ref, kg_ref, qt_ref, kn_ref, vt_ref):
    qn = _pair_rms(q_ref[0].astype(F32), qg_ref[...]) * (HEAD_DIM_A ** -0.5 * LOG2E)
    qt_ref[0, 0] = qn.T.astype(BF16)
    kn_ref[0] = _pair_rms(k_ref[0].astype(F32), kg_ref[...]).astype(BF16)
    vt_ref[0, 0] = v_ref[0].astype(F32).T.astype(BF16)


def _prep(main3, qg2, kg2):
    b, s, _ = main3.shape
    ts = 512
    tile = lambda off: pl.BlockSpec((1, ts, HEAD_W_A), lambda bi, si, h: (bi, si, off + h))
    tspec = pl.BlockSpec((1, 1, HEAD_W_A, ts), lambda bi, si, h: (bi, h, 0, si))
    gspec = pl.BlockSpec((1, HEAD_W_A), lambda bi, si, h: (0, 0))
    return pl.pallas_call(
        _prep_kernel,
        grid=(b, s // ts, N_HEADS_A),
        in_specs=[tile(0), tile(N_HEADS_A), tile(2 * N_HEADS_A), gspec, gspec],
        out_specs=[tspec, tile(0), tspec],
        out_shape=[jax.ShapeDtypeStruct((b, N_HEADS_A, HEAD_W_A, s), BF16),
                   jax.ShapeDtypeStruct((b, s, D_MODEL), BF16),
                   jax.ShapeDtypeStruct((b, N_HEADS_A, HEAD_W_A, s), BF16)],
        compiler_params=_cparams(("parallel", "parallel", "parallel")),
        name="prep",
    )(main3, main3, main3, qg2, kg2)


def _t5_bucket(rel):
    nb = N_BUCKETS // 2
    max_exact = nb // 2
    ret = (rel > 0).astype(jnp.int32) * nb
    n = jnp.abs(rel)
    nf = jnp.maximum(n, 1).astype(F32)
    large = max_exact + (jnp.log(nf / max_exact) / math.log(MAX_DISTANCE / max_exact)
                         * (nb - max_exact)).astype(jnp.int32)
    large = jnp.minimum(large, nb - 1)
    return ret + jnp.where(n < max_exact, n, large)


def _bias_kernel(tbl_ref, bkt_ref, o_ref):
    h = pl.program_id(0)
    bkt = bkt_ref[...]
    acc = jnp.zeros(bkt.shape, F32)
    for i in range(N_BUCKETS):
        acc = jnp.where(bkt == i, tbl_ref[i, h], acc)
    o_ref[0] = acc * LOG2E


def _bias_tiles(rel_bias):
    t = ATT_TILE
    kk = jnp.arange(t, dtype=jnp.int32)[:, None]
    qq = jnp.arange(t, dtype=jnp.int32)[None, :]
    rel = jnp.stack([(d - NEAR_TILES // 2) * t + kk - qq for d in range(NEAR_TILES)])
    return pl.pallas_call(
        _bias_kernel,
        grid=(N_HEADS_A,),
        in_specs=[pl.BlockSpec(memory_space=pltpu.SMEM),
                  pl.BlockSpec((NEAR_TILES, t, t), lambda h: (0, 0, 0))],
        out_specs=pl.BlockSpec((1, NEAR_TILES, t, t), lambda h: (h, 0, 0, 0)),
        out_shape=jax.ShapeDtypeStruct((N_HEADS_A, NEAR_TILES, t, t), F32),
        compiler_params=_cparams(("parallel",)),
        name="bias",
    )(rel_bias, _t5_bucket(rel))


def _attn_kernel(tbl_ref, qt_ref, k_ref, vt_ref, bias_ref, kg_ref, lamv_ref, subg_ref, o_ref,
                 w_scr, m_scr, l_scr, acc_scr, l8_scr, *, nk, lambda_init):
    t = ATT_TILE
    blk = min(ATT_BLOCK, nk)
    h = pl.program_id(1)
    qi = pl.program_id(2)

    q = qt_ref[0, 0]
    z = jnp.zeros((HEAD_DIM_A, t), BF16)
    w_scr[:, :t] = jnp.concatenate([q[:HEAD_DIM_A], z], axis=0)
    w_scr[:, t:] = jnp.concatenate([z, q[HEAD_DIM_A:]], axis=0)

    wf = w_scr[...].astype(F32)
    qn = jnp.sqrt(jnp.max(jnp.sum(wf * wf, axis=0, keepdims=True), axis=1, keepdims=True))
    kb = jnp.max(jnp.abs(kg_ref[...]), axis=1, keepdims=True) * (1.01 * HEAD_DIM_A ** 0.5)
    babs = jnp.abs(tbl_ref[0, h])
    for i in range(1, N_BUCKETS):
        babs = jnp.maximum(babs, jnp.abs(tbl_ref[i, h]))
    fast = (qn * kb)[0, 0] + babs * LOG2E <= EXP2_RANGE

    def chunk_bias(kj):
        return bias_ref[0, jnp.clip(kj - qi, -2, 2) + 2]

    def scores(kj, m):
        k0 = pl.multiple_of(kj * t, t)
        return jnp.dot(k_ref[0, pl.ds(k0, t), :], w_scr[:, m * t:(m + 1) * t],
                       preferred_element_type=F32) + chunk_bias(kj)

    def values(kj):
        return vt_ref[0, 0, :, pl.ds(pl.multiple_of(kj * t, t), t)]

    @pl.when(fast)
    def _():
        l8_scr[...] = jnp.zeros(l8_scr.shape, F32)
        acc_scr[...] = jnp.zeros(acc_scr.shape, F32)

        def block(j, carry):
            k0 = pl.multiple_of(j * (blk * t), blk * t)
            s = jnp.dot(k_ref[0, pl.ds(k0, blk * t), :], w_scr[...], preferred_element_type=F32)
            p = []
            for c in range(blk):
                b = chunk_bias(j * blk + c)
                p.append(jnp.exp2(s[c * t:(c + 1) * t] + jnp.concatenate([b, b], axis=1)))
            p = jnp.concatenate(p, axis=0)
            l8_scr[...] += jnp.sum(p.reshape(blk * t // 8, 8, 2 * t), axis=0)
            acc_scr[...] += jnp.dot(vt_ref[0, 0, :, pl.ds(k0, blk * t)], p.astype(BF16),
                                    preferred_element_type=F32)
            return carry

        lax.fori_loop(0, nk // blk, block, 0)
        l_scr[...] = jnp.sum(l8_scr[...], axis=0, keepdims=True)

    @pl.when(jnp.logical_not(fast))
    def _():
        m_scr[...] = jnp.full(m_scr.shape, -jnp.inf, F32)
        l_scr[...] = jnp.zeros(l_scr.shape, F32)
        acc_scr[...] = jnp.zeros(acc_scr.shape, F32)

        def online_step(kj, c):
            for m in range(2):
                cols = slice(m * t, (m + 1) * t)
                s = scores(kj, m)
                m_old = m_scr[:, cols]
                m_new = jnp.maximum(m_old, jnp.max(s, axis=0, keepdims=True))
                alpha = jnp.exp2(m_old - m_new)
                p = jnp.exp2(s - m_new)
                l_scr[:, cols] = alpha * l_scr[:, cols] + jnp.sum(p, axis=0, keepdims=True)
                pv = jnp.dot(values(kj), p.astype(BF16), preferred_element_type=F32)
                acc_scr[:, cols] = alpha * acc_scr[:, cols] + pv
                m_scr[:, cols] = m_new
            return c

        lax.fori_loop(0, nk, online_step, 0)

    lv = lamv_ref[...]
    lam = (jnp.exp(jnp.sum(lv[0:1] * lv[1:2], axis=-1, keepdims=True))
           - jnp.exp(jnp.sum(lv[2:3] * lv[3:4], axis=-1, keepdims=True)) + lambda_init)
    acc = acc_scr[...]
    inv = 1.0 / l_scr[...]
    o = acc[:, :t] * inv[:, :t] - lam * (acc[:, t:] * inv[:, t:])
    ms = jnp.mean(o * o, axis=0, keepdims=True)
    y = (o * lax.rsqrt(ms + EPS)).T
    o_ref[0] = (y * subg_ref[...] * (1.0 - lambda_init)).astype(o_ref.dtype)


def _attn(rel_bias, qt, kn, vt, bias_t, kg2, lamv, subg, lambda_init):
    b, s, _ = kn.shape
    t = ATT_TILE
    nk = s // t
    assert nk % min(ATT_BLOCK, nk) == 0 and t >= MAX_DISTANCE
    kern = functools.partial(_attn_kernel, nk=nk, lambda_init=lambda_init)
    const = lambda shape: pl.BlockSpec(shape, lambda bi, h, qi: (0,) * len(shape))
    return pl.pallas_call(
        kern,
        grid=(b, N_HEADS_A, s // t),
        in_specs=[pl.BlockSpec(memory_space=pltpu.SMEM),
                  pl.BlockSpec((1, 1, HEAD_W_A, t), lambda bi, h, qi: (bi, h, 0, qi)),
                  pl.BlockSpec((1, s, HEAD_W_A), lambda bi, h, qi: (bi, 0, h)),
                  pl.BlockSpec((1, 1, HEAD_W_A, s), lambda bi, h, qi: (bi, h, 0, 0)),
                  pl.BlockSpec((1, NEAR_TILES, t, t), lambda bi, h, qi: (h, 0, 0, 0)),
                  const((1, HEAD_W_A)), const((8, HEAD_W_A)), const((1, HEAD_W_A))],
        out_specs=pl.BlockSpec((1, t, HEAD_W_A), lambda bi, h, qi: (bi, qi, h)),
        out_shape=jax.ShapeDtypeStruct((b, s, D_MODEL), BF16),
        scratch_shapes=[pltpu.VMEM((HEAD_W_A, 2 * t), BF16),
                        pltpu.VMEM((1, 2 * t), F32),
                        pltpu.VMEM((1, 2 * t), F32),
                        pltpu.VMEM((HEAD_W_A, 2 * t), F32),
                        pltpu.VMEM((8, 2 * t), F32)],
        compiler_params=_cparams(("parallel", "parallel", "parallel")),
        name="attn",
    )(rel_bias, qt, kn, vt, bias_t, kg2, lamv, subg)


def _log_gate(lr, wg_ref, bg_ref):
    x = jnp.dot(lr.astype(BF16), wg_ref[...], preferred_element_type=F32) + bg_ref[...]
    return (jnp.minimum(x, 0.0) - jnp.log1p(jnp.exp(-jnp.abs(x)))) * (1.0 / GATE_NORM)


def _gla_bwd_kernel(q_ref, k_ref, v_ref, lr_ref, wg_ref, bg_ref, o_ref, s_scr):
    c = GLA_CHUNK

    @pl.when(pl.program_id(1) == 0)
    def _():
        s_scr[...] = jnp.zeros(s_scr.shape, F32)

    g = _log_gate(lr_ref[0], wg_ref, bg_ref)
    row = lax.broadcasted_iota(jnp.int32, g.shape, 0)
    suf = g
    sh = 1
    while sh < c:
        suf = suf + jnp.where(row < c - sh, pltpu.roll(suf, c - sh, 0), 0.0)
        sh *= 2
    tot = suf[0:1, :]
    for h in range(N_HEADS_B):
        sl = slice(h * KEY_DIM_B, (h + 1) * KEY_DIM_B)
        vs = slice(h * V_DIM_B, (h + 1) * V_DIM_B)
        q = q_ref[0][:, sl].astype(F32) * KEY_DIM_B ** -0.5
        k = k_ref[0][:, sl].astype(F32)
        qd = (q * jnp.exp(suf[:, sl])).astype(BF16)
        kd = k * jnp.exp(tot[:, sl] - suf[:, sl])
        st = s_scr[h]
        o_ref[0, :, vs] = jnp.dot(qd, st.astype(BF16), preferred_element_type=F32)
        dec = jnp.broadcast_to(jnp.exp(tot[:, sl]), (c, KEY_DIM_B)).T[:, 0:1]
        s_scr[h] = st * dec + jnp.dot(kd.T.astype(BF16), v_ref[0][:, vs], preferred_element_type=F32)


def _gla_fwd_kernel(q_ref, k_ref, v_ref, og_ref, lr_ref, obi_ref, wgf_ref, bgf_ref, wgb_ref, bgb_ref,
                    gn_ref, o_ref, s_scr):
    c = GLA_CHUNK

    @pl.when(pl.program_id(1) == 0)
    def _():
        s_scr[...] = jnp.zeros(s_scr.shape, F32)

    lr = lr_ref[0]
    gf_all = _log_gate(lr, wgf_ref, bgf_ref)
    gb_all = _log_gate(lr, wgb_ref, bgb_ref)
    ri = lax.broadcasted_iota(jnp.int32, (c, c), 0)
    ci = lax.broadcasted_iota(jnp.int32, (c, c), 1)
    xr = jnp.bitwise_xor(ri, ci)
    nt = (((1,), (1,)), ((), ()))
    for h in range(N_HEADS_B):
        sl = slice(h * KEY_DIM_B, (h + 1) * KEY_DIM_B)
        vs = slice(h * V_DIM_B, (h + 1) * V_DIM_B)
        q = q_ref[0][:, sl].astype(F32) * KEY_DIM_B ** -0.5
        k = k_ref[0][:, sl].astype(F32)
        v = v_ref[0][:, vs]
        pre = gf_all[:, sl]
        tf = pre
        suf = gb_all[:, sl]
        tb = suf
        a = jnp.where(ri == ci, 2.0 * lax.dot_general(q.astype(BF16), k.astype(BF16), nt,
                                                      preferred_element_type=F32), 0.0)
        blk = 1
        while blk < c:
            level = jnp.logical_and(xr >= blk, xr < 2 * blk)
            rf = lax.dot_general((q * jnp.exp(pre)).astype(BF16), (k * jnp.exp(tf - pre)).astype(BF16),
                                 nt, preferred_element_type=F32)
            a = jnp.where(jnp.logical_and(level, ri > ci), rf, a)
            rb = lax.dot_general((q * jnp.exp(suf)).astype(BF16), (k * jnp.exp(tb - suf)).astype(BF16),
                                 nt, preferred_element_type=F32)
            a = jnp.where(jnp.logical_and(level, ri < ci), rb, a)
            hi = jnp.bitwise_and(ri, blk) != 0
            tf_dn, tf_up = pltpu.roll(tf, blk, 0), pltpu.roll(tf, c - blk, 0)
            tb_dn, tb_up = pltpu.roll(tb, blk, 0), pltpu.roll(tb, c - blk, 0)
            pre = pre + jnp.where(hi, tf_dn, 0.0)
            suf = suf + jnp.where(hi, 0.0, tb_up)
            tf = tf + jnp.where(hi, tf_dn, tf_up)
            tb = tb + jnp.where(hi, tb_dn, tb_up)
            blk *= 2
        st = s_scr[h]
        o = (jnp.dot(a.astype(BF16), v, preferred_element_type=F32)
             + jnp.dot((q * jnp.exp(pre)).astype(BF16), st.astype(BF16), preferred_element_type=F32)
             + obi_ref[0, :, vs])
        kd = k * jnp.exp(tf - pre)
        s_scr[h] = st * jnp.exp(tf).T[:, 0:1] + jnp.dot(kd.T.astype(BF16), v, preferred_element_type=F32)
        ms = jnp.mean(o * o, axis=-1, keepdims=True)
        og = og_ref[0][:, vs].astype(F32)
        o_ref[0, :, vs] = (o * lax.rsqrt(ms + EPS) * gn_ref[...] * (og / (1.0 + jnp.exp(-og)))
                           ).astype(o_ref.dtype)


def _gla(main3, lr3, wgf, bgf, wgb, bgb, gn):
    b, s, _ = main3.shape
    c = GLA_CHUNK
    nc = s // c
    hk = N_HEADS_B * KEY_DIM_B
    state = pltpu.VMEM((N_HEADS_B, KEY_DIM_B, V_DIM_B), F32)

    def specs(cidx):
        return [pl.BlockSpec((1, c, hk), lambda bi, i: (bi, cidx(i), 3 * D_MODEL // hk)),
                pl.BlockSpec((1, c, hk), lambda bi, i: (bi, cidx(i), 3 * D_MODEL // hk + 1)),
                pl.BlockSpec((1, c, D_MODEL), lambda bi, i: (bi, cidx(i), 4))]

    full = lambda shape: pl.BlockSpec(shape, lambda bi, i: (0,) * len(shape))
    rev = lambda i: nc - 1 - i
    obi = pl.pallas_call(
        _gla_bwd_kernel,
        grid=(b, nc),
        in_specs=specs(rev) + [pl.BlockSpec((1, c, LR_W), lambda bi, i: (bi, rev(i), 0)),
                               full((LR_W, hk)), full((1, hk))],
        out_specs=pl.BlockSpec((1, c, D_MODEL), lambda bi, i: (bi, rev(i), 0)),
        out_shape=jax.ShapeDtypeStruct((b, s, D_MODEL), F32),
        scratch_shapes=[state],
        compiler_params=_cparams(("parallel", "arbitrary")),
        name="gla_bwd",
    )(main3, main3, main3, lr3, wgb, bgb)

    fwd = lambda i: i
    return pl.pallas_call(
        _gla_fwd_kernel,
        grid=(b, nc),
        in_specs=specs(fwd) + [pl.BlockSpec((1, c, D_MODEL), lambda bi, i: (bi, i, 5)),
                               pl.BlockSpec((1, c, LR_W), lambda bi, i: (bi, i, 0)),
                               pl.BlockSpec((1, c, D_MODEL), lambda bi, i: (bi, i, 0)),
                               full((LR_W, hk)), full((1, hk)), full((LR_W, hk)), full((1, hk)),
                               full((1, V_DIM_B))],
        out_specs=pl.BlockSpec((1, c, D_MODEL), lambda bi, i: (bi, i, 0)),
        out_shape=jax.ShapeDtypeStruct((b, s, D_MODEL), BF16),
        scratch_shapes=[state],
        compiler_params=_cparams(("parallel", "arbitrary")),
        name="gla_fwd",
    )(main3, main3, main3, main3, lr3, obi, wgf, bgf, wgb, bgb, gn)


def _mix_kernel(x_ref, oa_ref, ob_ref, ga_ref, gb_ref, mod_ref, wa_ref, wb_ref, wo_ref, o_ref):
    ya = jnp.dot(oa_ref[...], wa_ref[...], preferred_element_type=F32)
    yb = jnp.dot(ob_ref[...], wb_ref[...], preferred_element_type=F32)
    sig = lambda r: 1.0 / (1.0 + jnp.exp(-r[...].astype(F32)))
    merged = sig(ga_ref) * ya + sig(gb_ref) * yb
    o_ref[...] = x_ref[...] + mod_ref[0, 2:3, :] * jnp.dot(merged.astype(BF16), wo_ref[...],
                                                           preferred_element_type=F32)


def _mix(x2, oa2, ob2, main2, mod8, wa, wb, wo, seq):
    t = x2.shape[0]
    tm = ROW_TILE
    row = lambda j: pl.BlockSpec((tm, D_MODEL), lambda i: (i, j))
    wspec = pl.BlockSpec((D_MODEL, D_MODEL), lambda i: (0, 0))
    return pl.pallas_call(
        _mix_kernel,
        grid=(t // tm,),
        in_specs=[row(0), row(0), row(0), row(6), row(7),
                  pl.BlockSpec((1, 8, D_MODEL), lambda i: ((i * tm) // seq, 0, 0)),
                  wspec, wspec, wspec],
        out_specs=row(0),
        out_shape=jax.ShapeDtypeStruct((t, D_MODEL), F32),
        compiler_params=_cparams(("parallel",)),
        name="mix",
    )(x2, oa2, ob2, main2, main2, mod8, wa, wb, wo)


def _mlp_kernel(x_ref, mod_ref, g_ref, wu_ref, wd_ref, o_ref):
    x = x_ref[...]
    h = _modulated_norm(x, g_ref[...], mod_ref[0, 3:4, :], mod_ref[0, 4:5, :]).astype(BF16)
    acc = jnp.zeros(x.shape, F32)
    for j in range(D_FF // D_MODEL):
        cs = slice(j * D_MODEL, (j + 1) * D_MODEL)
        u = jnp.maximum(jnp.dot(h, wu_ref[:, cs], preferred_element_type=F32), 0.0)
        acc = acc + jnp.dot((u * u).astype(BF16), wd_ref[cs, :], preferred_element_type=F32)
    o_ref[...] = x + mod_ref[0, 5:6, :] * acc


def _mlp(x2, mod8, g2, wu, wd, seq):
    t = x2.shape[0]
    tm = ROW_TILE
    once = pl.Buffered(1)
    return pl.pallas_call(
        _mlp_kernel,
        grid=(t // tm,),
        in_specs=[pl.BlockSpec((tm, D_MODEL), lambda i: (i, 0)),
                  pl.BlockSpec((1, 8, D_MODEL), lambda i: ((i * tm) // seq, 0, 0)),
                  pl.BlockSpec((1, D_MODEL), lambda i: (0, 0)),
                  pl.BlockSpec((D_MODEL, D_FF), lambda i: (0, 0), pipeline_mode=once),
                  pl.BlockSpec((D_FF, D_MODEL), lambda i: (0, 0), pipeline_mode=once)],
        out_specs=pl.BlockSpec((tm, D_MODEL), lambda i: (i, 0)),
        out_shape=jax.ShapeDtypeStruct((t, D_MODEL), F32),
        compiler_params=_cparams(("parallel",)),
        name="mlp",
    )(x2, mod8, g2, wu, wd)


def _split_w_in(w_in):
    o = np.cumsum([0, 1024, 1024, 1024, 512, 512, 1024, 1024, GATE_RANK, GATE_RANK, 1024, 1024])
    main = jnp.concatenate([w_in[:, o[0]:o[7]], w_in[:, o[9]:o[11]]], axis=1)
    lr = jnp.pad(w_in[:, o[7]:o[9]], ((0, 0), (0, LR_W - 2 * GATE_RANK)))
    return main.astype(BF16), lr.astype(BF16)


def _pad_gate_w(w_gate, row0):
    return jnp.pad(w_gate, ((row0, LR_W - GATE_RANK - row0), (0, 0))).astype(BF16)


def _layer(x, mod8, layer_idx, p):
    b, s, _ = x.shape
    t = b * s
    lambda_init = 0.8 - 0.6 * math.exp(-0.3 * layer_idx)
    x2 = x.reshape(t, D_MODEL)
    row = lambda v: v.reshape(1, -1).astype(F32)

    main2, lr2 = _proj(x2, mod8, row(p["norm1_g"]), p["w_main"], p["w_lr"], s)
    main3 = main2.reshape(b, s, N_MAIN)
    lr3 = lr2.reshape(b, s, LR_W)

    kg2 = row(jnp.tile(p["k_norm_g"], 2))
    qt, kn, vt = _prep(main3, row(jnp.tile(p["q_norm_g"], 2)), kg2)
    lamv = jnp.zeros((8, HEAD_W_A), F32).at[0:4, :HEAD_DIM_A].set(
        jnp.stack([p["lam_q1"], p["lam_k1"], p["lam_q2"], p["lam_k2"]]).astype(F32))
    oa = _attn(p["rel_bias"], qt, kn, vt, p["bias_t"], kg2, lamv, row(p["subln_g"]), lambda_init)

    ob = _gla(main3, lr3, p["wgf"], row(p["b_gate_f"]), p["wgb"], row(p["b_gate_b"]), row(p["gla_norm_g"]))

    x1 = _mix(x2, oa.reshape(t, D_MODEL), ob.reshape(t, D_MODEL), main2, mod8,
              p["w_branch_a"], p["w_branch_b"], p["w_out"], s)
    y = _mlp(x1, mod8, row(p["norm2_g"]), p["w_up"], p["w_down"], s)
    return y.reshape(b, s, D_MODEL)


def kernel(x_prompt, x_sample, c_prompt, c_sample, rel_bias, w_ada, b_ada, norm1_g, w_in, q_norm_g, k_norm_g, lam_q1, lam_k1, lam_q2, lam_k2, subln_g, w_gate_f, b_gate_f, w_gate_b, b_gate_b, gla_norm_g, w_branch_a, w_branch_b, w_out, norm2_g, w_up, w_down):
    depth = w_in.shape[0]
    nb_p, nb_s = c_prompt.shape[0], c_sample.shape[0]
    rows = -(-(nb_p + nb_s) // 8) * 8
    c_all = jnp.pad(jnp.concatenate([c_prompt, c_sample], axis=0), ((0, rows - nb_p - nb_s), (0, 0)))
    bias_t = _bias_tiles(rel_bias.astype(F32))

    xp, xs = x_prompt, x_sample
    for l in range(depth):
        mod = _ada(c_all, w_ada[l], b_ada[l]).reshape(rows, 6, D_MODEL)
        mod8 = jnp.pad(mod, ((0, 0), (0, 2), (0, 0)))
        w_main, w_lr = _split_w_in(w_in[l])
        p = dict(rel_bias=rel_bias.astype(F32), bias_t=bias_t, norm1_g=norm1_g[l], w_main=w_main, w_lr=w_lr,
                 q_norm_g=q_norm_g[l], k_norm_g=k_norm_g[l], lam_q1=lam_q1[l], lam_k1=lam_k1[l],
                 lam_q2=lam_q2[l], lam_k2=lam_k2[l], subln_g=subln_g[l],
                 wgf=_pad_gate_w(w_gate_f[l], 0), b_gate_f=b_gate_f[l],
                 wgb=_pad_gate_w(w_gate_b[l], GATE_RANK), b_gate_b=b_gate_b[l],
                 gla_norm_g=gla_norm_g[l], w_branch_a=w_branch_a[l].astype(BF16),
                 w_branch_b=w_branch_b[l].astype(BF16), w_out=w_out[l].astype(BF16),
                 norm2_g=norm2_g[l], w_up=w_up[l].astype(BF16), w_down=w_down[l].astype(BF16))
        xp = _layer(xp, mod8[:nb_p], l, p)
        xs = _layer(xs, mod8[nb_p:nb_p + nb_s], l, p)
    return (xp, xs)
```

```python
import functools
import math

import jax
import jax.numpy as jnp
import numpy as np
from jax import lax
from jax.experimental import pallas as pl
from jax.experimental.pallas import tpu as pltpu

F32 = jnp.float32
BF16 = jnp.bfloat16

D_MODEL = 1024
HEAD_DIM_A = 64
N_HEADS_A = 8
HEAD_W_A = 2 * HEAD_DIM_A
N_HEADS_B = 4
KEY_DIM_B = 128
V_DIM_B = 256
GATE_RANK = 16
GATE_NORM = 16.0
N_BUCKETS = 32
MAX_DISTANCE = 128
D_FF = 4 * D_MODEL
EPS = 1e-6
LOG2E = math.log2(math.e)

N_MAIN = 8 * D_MODEL
N_REST = 5 * D_MODEL
LR_W = 128

VMEM_LIMIT = 56 * 1024 * 1024
ROW_TILE = 512
ATT_TILE = 256
ATT_BLOCK = 4
NEAR_TILES = 5
EXP2_RANGE = 100.0
GLA_CHUNK = 128


def _cparams(sem):
    return pltpu.CompilerParams(dimension_semantics=sem, vmem_limit_bytes=VMEM_LIMIT)


def _ada_kernel(c_ref, w_ref, b_ref, o_ref):
    c = c_ref[...]
    a = c / (1.0 + jnp.exp(-c))
    o_ref[...] = jnp.dot(a, w_ref[...], preferred_element_type=F32,
                         precision=lax.Precision.HIGHEST) + b_ref[...]


def _ada(c_all, w_ada, b_ada):
    r = c_all.shape[0]
    tn = 1024
    return pl.pallas_call(
        _ada_kernel,
        grid=(6 * D_MODEL // tn,),
        in_specs=[pl.BlockSpec((r, D_MODEL), lambda n: (0, 0)),
                  pl.BlockSpec((D_MODEL, tn), lambda n: (0, n)),
                  pl.BlockSpec((1, tn), lambda n: (0, n))],
        out_specs=pl.BlockSpec((r, tn), lambda n: (0, n)),
        out_shape=jax.ShapeDtypeStruct((r, 6 * D_MODEL), F32),
        compiler_params=_cparams(("parallel",)),
        name="ada",
    )(c_all, w_ada, b_ada.reshape(1, -1))


def _modulated_norm(x, g, shift, scale):
    ms = jnp.mean(x * x, axis=-1, keepdims=True)
    return x * lax.rsqrt(ms + EPS) * g * (1.0 + scale) + shift


def _pair_rms(x, g):
    lane = lax.broadcasted_iota(jnp.int32, x.shape, 1)
    x2 = x * x
    s_all = jnp.sum(x2, axis=-1, keepdims=True)
    s_lo = jnp.sum(jnp.where(lane < HEAD_DIM_A, x2, 0.0), axis=-1, keepdims=True)
    ss = jnp.where(lane < HEAD_DIM_A, s_lo, s_all - s_lo)
    return x * lax.rsqrt(ss * (1.0 / HEAD_DIM_A) + EPS) * g


def _proj_kernel(x_ref, mod_ref, g_ref, w_ref, wlr_ref, qg_ref, kg_ref,
                 qt_ref, kn_ref, vt_ref, o_ref, lr_ref):
    h = _modulated_norm(x_ref[...], g_ref[...], mod_ref[0, 0:1, :], mod_ref[0, 1:2, :]).astype(BF16)
    lr_ref[...] = jnp.dot(h, wlr_ref[...], preferred_element_type=F32)
    piece = lambda j: jnp.dot(h, w_ref[:, j * D_MODEL:(j + 1) * D_MODEL], preferred_element_type=F32)
    heads = [slice(i * HEAD_W_A, (i + 1) * HEAD_W_A) for i in range(N_HEADS_A)]

    qa = piece(0)
    for i, sl in enumerate(heads):
        qn = _pair_rms(qa[:, sl], qg_ref[...]) * (HEAD_DIM_A ** -0.5 * LOG2E)
        qt_ref[0, i] = qn.T.astype(BF16)
    ka = piece(1)
    for sl in heads:
        kn_ref[:, sl] = _pair_rms(ka[:, sl], kg_ref[...]).astype(BF16)
    va = piece(2)
    for i, sl in enumerate(heads):
        vt_ref[0, i] = va[:, sl].T.astype(BF16)
    for j in range(3, N_MAIN // D_MODEL):
        o_ref[:, (j - 3) * D_MODEL:(j - 2) * D_MODEL] = piece(j).astype(o_ref.dtype)


def _proj(x2, mod8, g1, w_main, w_lr, qg2, kg2, seq):
    t = x2.shape[0]
    tm = ROW_TILE
    spt = seq // tm
    once = pl.Buffered(1)
    const = lambda shape, **kw: pl.BlockSpec(shape, lambda i: (0,) * len(shape), **kw)
    tspec = pl.BlockSpec((1, N_HEADS_A, HEAD_W_A, tm), lambda i: (i // spt, 0, 0, i % spt))
    tshape = jax.ShapeDtypeStruct((t // seq, N_HEADS_A, HEAD_W_A, seq), BF16)
    return pl.pallas_call(
        _proj_kernel,
        grid=(t // tm,),
        in_specs=[pl.BlockSpec((tm, D_MODEL), lambda i: (i, 0)),
                  pl.BlockSpec((1, 8, D_MODEL), lambda i: (i // spt, 0, 0)),
                  const((1, D_MODEL)),
                  const((D_MODEL, N_MAIN), pipeline_mode=once),
                  const((D_MODEL, LR_W), pipeline_mode=once),
                  const((1, HEAD_W_A)), const((1, HEAD_W_A))],
        out_specs=[tspec,
                   pl.BlockSpec((tm, D_MODEL), lambda i: (i, 0)),
                   tspec,
                   pl.BlockSpec((tm, N_REST), lambda i: (i, 0)),
                   pl.BlockSpec((tm, LR_W), lambda i: (i, 0))],
        out_shape=[tshape,
                   jax.ShapeDtypeStruct((t, D_MODEL), BF16),
                   tshape,
                   jax.ShapeDtypeStruct((t, N_REST), BF16),
                   jax.ShapeDtypeStruct((t, LR_W), F32)],
        compiler_params=_cparams(("parallel",)),
        name="proj",
    )(x2, mod8, g1, w_main, w_lr, qg2, kg2)


def _t5_bucket(rel):
    nb = N_BUCKETS // 2
    max_exact = nb // 2
    ret = (rel > 0).astype(jnp.int32) * nb
    n = jnp.abs(rel)
    nf = jnp.maximum(n, 1).astype(F32)
    large = max_exact + (jnp.log(nf / max_exact) / math.log(MAX_DISTANCE / max_exact)
                         * (nb - max_exact)).astype(jnp.int32)
    large = jnp.minimum(large, nb - 1)
    return ret + jnp.where(n < max_exact, n, large)


def _bias_kernel(tbl_ref, bkt_ref, o_ref):
    h = pl.program_id(0)
    bkt = bkt_ref[...]
    acc = jnp.zeros(bkt.shape, F32)
    for i in range(N_BUCKETS):
        acc = jnp.where(bkt == i, tbl_ref[i, h], acc)
    o_ref[0] = acc * LOG2E


def _bias_tiles(rel_bias):
    t = ATT_TILE
    kk = jnp.arange(t, dtype=jnp.int32)[:, None]
    qq = jnp.arange(t, dtype=jnp.int32)[None, :]
    rel = jnp.stack([(d - NEAR_TILES // 2) * t + kk - qq for d in range(NEAR_TILES)])
    return pl.pallas_call(
        _bias_kernel,
        grid=(N_HEADS_A,),
        in_specs=[pl.BlockSpec(memory_space=pltpu.SMEM),
                  pl.BlockSpec((NEAR_TILES, t, t), lambda h: (0, 0, 0))],
        out_specs=pl.BlockSpec((1, NEAR_TILES, t, t), lambda h: (h, 0, 0, 0)),
        out_shape=jax.ShapeDtypeStruct((N_HEADS_A, NEAR_TILES, t, t), F32),
        compiler_params=_cparams(("parallel",)),
        name="bias",
    )(rel_bias, _t5_bucket(rel))


def _attn_kernel(tbl_ref, qt_ref, k_ref, vt_ref, bias_ref, kg_ref, lamv_ref, subg_ref, o_ref,
                 w_scr, m_scr, l_scr, acc_scr, l8_scr, s_scr, *, nk, lambda_init):
    t = ATT_TILE
    blk = min(ATT_BLOCK, nk)
    h = pl.program_id(1)
    qi = pl.program_id(2)

    q = qt_ref[0, 0]
    z = jnp.zeros((HEAD_DIM_A, t), BF16)
    w_scr[:, :t] = jnp.concatenate([q[:HEAD_DIM_A], z], axis=0)
    w_scr[:, t:] = jnp.concatenate([z, q[HEAD_DIM_A:]], axis=0)

    wf = w_scr[...].astype(F32)
    qn = jnp.sqrt(jnp.max(jnp.sum(wf * wf, axis=0, keepdims=True), axis=1, keepdims=True))
    kb = jnp.max(jnp.abs(kg_ref[...]), axis=1, keepdims=True) * (1.01 * HEAD_DIM_A ** 0.5)
    babs = jnp.abs(tbl_ref[0, h])
    for i in range(1, N_BUCKETS):
        babs = jnp.maximum(babs, jnp.abs(tbl_ref[i, h]))
    fast = (qn * kb)[0, 0] + babs * LOG2E <= EXP2_RANGE

    def chunk_bias(kj):
        return bias_ref[0, jnp.clip(kj - qi, -2, 2) + 2]

    def scores(kj, m):
        k0 = pl.multiple_of(kj * t, t)
        return jnp.dot(k_ref[0, pl.ds(k0, t), :], w_scr[:, m * t:(m + 1) * t],
                       preferred_element_type=F32) + chunk_bias(kj)

    def values(kj):
        return vt_ref[0, 0, :, pl.ds(pl.multiple_of(kj * t, t), t)]

    @pl.when(fast)
    def _():
        l8_scr[...] = jnp.zeros(l8_scr.shape, F32)
        acc_scr[...] = jnp.zeros(acc_scr.shape, F32)

        bt = blk * t
        nb = nk // blk

        def qk(j, slot):
            k0 = pl.multiple_of(j * bt, bt)
            s_scr[slot] = jnp.dot(k_ref[0, pl.ds(k0, bt), :], w_scr[...], preferred_element_type=F32)

        def softmax_pv(j, slot):
            p = []
            for c in range(blk):
                b = chunk_bias(j * blk + c)
                p.append(jnp.exp2(s_scr[slot, c * t:(c + 1) * t, :] + jnp.concatenate([b, b], axis=1)))
            p = jnp.concatenate(p, axis=0)
            l8_scr[...] += jnp.sum(p.reshape(bt // 8, 8, 2 * t), axis=0)
            k0 = pl.multiple_of(j * bt, bt)
            acc_scr[...] += jnp.dot(vt_ref[0, 0, :, pl.ds(k0, bt)], p.astype(BF16),
                                    preferred_element_type=F32)

        def pair(i, prefetch):
            qk(2 * i + 1, 1)
            softmax_pv(2 * i, 0)
            if prefetch:
                qk(2 * i + 2, 0)
            softmax_pv(2 * i + 1, 1)

        def pair_body(i, carry):
            pair(i, True)
            return carry

        qk(0, 0)
        if nb % 2 == 0:
            lax.fori_loop(0, nb // 2 - 1, pair_body, 0)
            pair(nb // 2 - 1, False)
        else:
            lax.fori_loop(0, nb // 2, pair_body, 0)
            softmax_pv(nb - 1, 0)
        l_scr[...] = jnp.sum(l8_scr[...], axis=0, keepdims=True)

    @pl.when(jnp.logical_not(fast))
    def _():
        m_scr[...] = jnp.full(m_scr.shape, -jnp.inf, F32)
        l_scr[...] = jnp.zeros(l_scr.shape, F32)
        acc_scr[...] = jnp.zeros(acc_scr.shape, F32)

        def online_step(kj, c):
            for m in range(2):
                cols = slice(m * t, (m + 1) * t)
                s = scores(kj, m)
                m_old = m_scr[:, cols]
                m_new = jnp.maximum(m_old, jnp.max(s, axis=0, keepdims=True))
                alpha = jnp.exp2(m_old - m_new)
                p = jnp.exp2(s - m_new)
                l_scr[:, cols] = alpha * l_scr[:, cols] + jnp.sum(p, axis=0, keepdims=True)
                pv = jnp.dot(values(kj), p.astype(BF16), preferred_element_type=F32)
                acc_scr[:, cols] = alpha * acc_scr[:, cols] + pv
                m_scr[:, cols] = m_new
            return c

        lax.fori_loop(0, nk, online_step, 0)

    lv = lamv_ref[...]
    lam = (jnp.exp(jnp.sum(lv[0:1] * lv[1:2], axis=-1, keepdims=True))
           - jnp.exp(jnp.sum(lv[2:3] * lv[3:4], axis=-1, keepdims=True)) + lambda_init)
    acc = acc_scr[...]
    inv = 1.0 / l_scr[...]
    o = acc[:, :t] * inv[:, :t] - lam * (acc[:, t:] * inv[:, t:])
    ms = jnp.mean(o * o, axis=0, keepdims=True)
    y = (o * lax.rsqrt(ms + EPS)).T
    o_ref[0] = (y * subg_ref[...] * (1.0 - lambda_init)).astype(o_ref.dtype)


def _attn(rel_bias, qt, kn, vt, bias_t, kg2, lamv, subg, lambda_init):
    b, s, _ = kn.shape
    t = ATT_TILE
    nk = s // t
    assert nk % min(ATT_BLOCK, nk) == 0 and t >= MAX_DISTANCE
    kern = functools.partial(_attn_kernel, nk=nk, lambda_init=lambda_init)
    const = lambda shape: pl.BlockSpec(shape, lambda bi, h, qi: (0,) * len(shape))
    return pl.pallas_call(
        kern,
        grid=(b, N_HEADS_A, s // t),
        in_specs=[pl.BlockSpec(memory_space=pltpu.SMEM),
                  pl.BlockSpec((1, 1, HEAD_W_A, t), lambda bi, h, qi: (bi, h, 0, qi)),
                  pl.BlockSpec((1, s, HEAD_W_A), lambda bi, h, qi: (bi, 0, h)),
                  pl.BlockSpec((1, 1, HEAD_W_A, s), lambda bi, h, qi: (bi, h, 0, 0)),
                  pl.BlockSpec((1, NEAR_TILES, t, t), lambda bi, h, qi: (h, 0, 0, 0)),
                  const((1, HEAD_W_A)), const((8, HEAD_W_A)), const((1, HEAD_W_A))],
        out_specs=pl.BlockSpec((1, t, HEAD_W_A), lambda bi, h, qi: (bi, qi, h)),
        out_shape=jax.ShapeDtypeStruct((b, s, D_MODEL), BF16),
        scratch_shapes=[pltpu.VMEM((HEAD_W_A, 2 * t), BF16),
                        pltpu.VMEM((1, 2 * t), F32),
                        pltpu.VMEM((1, 2 * t), F32),
                        pltpu.VMEM((HEAD_W_A, 2 * t), F32),
                        pltpu.VMEM((8, 2 * t), F32),
                        pltpu.VMEM((2, min(ATT_BLOCK, nk) * t, 2 * t), F32)],
        compiler_params=_cparams(("parallel", "parallel", "parallel")),
        name="attn",
    )(rel_bias, qt, kn, vt, bias_t, kg2, lamv, subg)


def _log_gate(lr, wg_ref, bg_ref):
    x = jnp.dot(lr.astype(BF16), wg_ref[...], preferred_element_type=F32) + bg_ref[...]
    return (jnp.minimum(x, 0.0) - jnp.log1p(jnp.exp(-jnp.abs(x)))) * (1.0 / GATE_NORM)


def _gla_bwd_kernel(q_ref, k_ref, v_ref, lr_ref, wg_ref, bg_ref, o_ref, s_scr):
    c = GLA_CHUNK

    @pl.when(pl.program_id(1) == 0)
    def _():
        s_scr[...] = jnp.zeros(s_scr.shape, F32)

    g = _log_gate(lr_ref[0], wg_ref, bg_ref)
    row = lax.broadcasted_iota(jnp.int32, g.shape, 0)
    suf = g
    sh = 1
    while sh < c:
        suf = suf + jnp.where(row < c - sh, pltpu.roll(suf, c - sh, 0), 0.0)
        sh *= 2
    tot = suf[0:1, :]
    for h in range(N_HEADS_B):
        sl = slice(h * KEY_DIM_B, (h + 1) * KEY_DIM_B)
        vs = slice(h * V_DIM_B, (h + 1) * V_DIM_B)
        q = q_ref[0][:, sl].astype(F32) * KEY_DIM_B ** -0.5
        k = k_ref[0][:, sl].astype(F32)
        qd = (q * jnp.exp(suf[:, sl])).astype(BF16)
        kd = k * jnp.exp(tot[:, sl] - suf[:, sl])
        st = s_scr[h]
        o_ref[0, :, vs] = jnp.dot(qd, st.astype(BF16), preferred_element_type=F32)
        dec = jnp.broadcast_to(jnp.exp(tot[:, sl]), (c, KEY_DIM_B)).T[:, 0:1]
        s_scr[h] = st * dec + jnp.dot(kd.T.astype(BF16), v_ref[0][:, vs], preferred_element_type=F32)


def _gla_fwd_kernel(q_ref, k_ref, v_ref, og_ref, lr_ref, obi_ref, wgf_ref, bgf_ref, wgb_ref, bgb_ref,
                    gn_ref, o_ref, s_scr):
    c = GLA_CHUNK

    @pl.when(pl.program_id(1) == 0)
    def _():
        s_scr[...] = jnp.zeros(s_scr.shape, F32)

    lr = lr_ref[0]
    gf_all = _log_gate(lr, wgf_ref, bgf_ref)
    gb_all = _log_gate(lr, wgb_ref, bgb_ref)
    ri = lax.broadcasted_iota(jnp.int32, (c, c), 0)
    ci = lax.broadcasted_iota(jnp.int32, (c, c), 1)
    xr = jnp.bitwise_xor(ri, ci)
    nt = (((1,), (1,)), ((), ()))
    for h in range(N_HEADS_B):
        sl = slice(h * KEY_DIM_B, (h + 1) * KEY_DIM_B)
        vs = slice(h * V_DIM_B, (h + 1) * V_DIM_B)
        q = q_ref[0][:, sl].astype(F32) * KEY_DIM_B ** -0.5
        k = k_ref[0][:, sl].astype(F32)
        v = v_ref[0][:, vs]
        pre = gf_all[:, sl]
        tf = pre
        suf = gb_all[:, sl]
        tb = suf
        a = jnp.where(ri == ci, 2.0 * lax.dot_general(q.astype(BF16), k.astype(BF16), nt,
                                                      preferred_element_type=F32), 0.0)
        blk = 1
        while blk < c:
            level = jnp.logical_and(xr >= blk, xr < 2 * blk)
            rf = lax.dot_general((q * jnp.exp(pre)).astype(BF16), (k * jnp.exp(tf - pre)).astype(BF16),
                                 nt, preferred_element_type=F32)
            a = jnp.where(jnp.logical_and(level, ri > ci), rf, a)
            rb = lax.dot_general((q * jnp.exp(suf)).astype(BF16), (k * jnp.exp(tb - suf)).astype(BF16),
                                 nt, preferred_element_type=F32)
            a = jnp.where(jnp.logical_and(level, ri < ci), rb, a)
            hi = jnp.bitwise_and(ri, blk) != 0
            tf_dn, tf_up = pltpu.roll(tf, blk, 0), pltpu.roll(tf, c - blk, 0)
            tb_dn, tb_up = pltpu.roll(tb, blk, 0), pltpu.roll(tb, c - blk, 0)
            pre = pre + jnp.where(hi, tf_dn, 0.0)
            suf = suf + jnp.where(hi, 0.0, tb_up)
            tf = tf + jnp.where(hi, tf_dn, tf_up)
            tb = tb + jnp.where(hi, tb_dn, tb_up)
            blk *= 2
        st = s_scr[h]
        o = (jnp.dot(a.astype(BF16), v, preferred_element_type=F32)
             + jnp.dot((q * jnp.exp(pre)).astype(BF16), st.astype(BF16), preferred_element_type=F32)
             + obi_ref[0, :, vs])
        kd = k * jnp.exp(tf - pre)
        s_scr[h] = st * jnp.exp(tf).T[:, 0:1] + jnp.dot(kd.T.astype(BF16), v, preferred_element_type=F32)
        ms = jnp.mean(o * o, axis=-1, keepdims=True)
        og = og_ref[0][:, vs].astype(F32)
        o_ref[0, :, vs] = (o * lax.rsqrt(ms + EPS) * gn_ref[...] * (og / (1.0 + jnp.exp(-og)))
                           ).astype(o_ref.dtype)


def _gla(main3, lr3, wgf, bgf, wgb, bgb, gn):
    b, s, _ = main3.shape
    c = GLA_CHUNK
    nc = s // c
    hk = N_HEADS_B * KEY_DIM_B
    state = pltpu.VMEM((N_HEADS_B, KEY_DIM_B, V_DIM_B), F32)

    def specs(cidx):
        return [pl.BlockSpec((1, c, hk), lambda bi, i: (bi, cidx(i), 0)),
                pl.BlockSpec((1, c, hk), lambda bi, i: (bi, cidx(i), 1)),
                pl.BlockSpec((1, c, D_MODEL), lambda bi, i: (bi, cidx(i), 1))]

    full = lambda shape: pl.BlockSpec(shape, lambda bi, i: (0,) * len(shape))
    rev = lambda i: nc - 1 - i
    obi = pl.pallas_call(
        _gla_bwd_kernel,
        grid=(b, nc),
        in_specs=specs(rev) + [pl.BlockSpec((1, c, LR_W), lambda bi, i: (bi, rev(i), 0)),
                               full((LR_W, hk)), full((1, hk))],
        out_specs=pl.BlockSpec((1, c, D_MODEL), lambda bi, i: (bi, rev(i), 0)),
        out_shape=jax.ShapeDtypeStruct((b, s, D_MODEL), F32),
        scratch_shapes=[state],
        compiler_params=_cparams(("parallel", "arbitrary")),
        name="gla_bwd",
    )(main3, main3, main3, lr3, wgb, bgb)

    fwd = lambda i: i
    return pl.pallas_call(
        _gla_fwd_kernel,
        grid=(b, nc),
        in_specs=specs(fwd) + [pl.BlockSpec((1, c, D_MODEL), lambda bi, i: (bi, i, 2)),
                               pl.BlockSpec((1, c, LR_W), lambda bi, i: (bi, i, 0)),
                               pl.BlockSpec((1, c, D_MODEL), lambda bi, i: (bi, i, 0)),
                               full((LR_W, hk)), full((1, hk)), full((LR_W, hk)), full((1, hk)),
                               full((1, V_DIM_B))],
        out_specs=pl.BlockSpec((1, c, D_MODEL), lambda bi, i: (bi, i, 0)),
        out_shape=jax.ShapeDtypeStruct((b, s, D_MODEL), BF16),
        scratch_shapes=[state],
        compiler_params=_cparams(("parallel", "arbitrary")),
        name="gla_fwd",
    )(main3, main3, main3, main3, lr3, obi, wgf, bgf, wgb, bgb, gn)


def _mix_kernel(x_ref, oa_ref, ob_ref, ga_ref, gb_ref, mod_ref, wa_ref, wb_ref, wo_ref, o_ref):
    ya = jnp.dot(oa_ref[...], wa_ref[...], preferred_element_type=F32)
    yb = jnp.dot(ob_ref[...], wb_ref[...], preferred_element_type=F32)
    sig = lambda r: 1.0 / (1.0 + jnp.exp(-r[...].astype(F32)))
    merged = sig(ga_ref) * ya + sig(gb_ref) * yb
    o_ref[...] = x_ref[...] + mod_ref[0, 2:3, :] * jnp.dot(merged.astype(BF16), wo_ref[...],
                                                           preferred_element_type=F32)


def _mix(x2, oa2, ob2, main2, mod8, wa, wb, wo, seq):
    t = x2.shape[0]
    tm = ROW_TILE
    row = lambda j: pl.BlockSpec((tm, D_MODEL), lambda i: (i, j))
    wspec = pl.BlockSpec((D_MODEL, D_MODEL), lambda i: (0, 0))
    return pl.pallas_call(
        _mix_kernel,
        grid=(t // tm,),
        in_specs=[row(0), row(0), row(0), row(3), row(4),
                  pl.BlockSpec((1, 8, D_MODEL), lambda i: ((i * tm) // seq, 0, 0)),
                  wspec, wspec, wspec],
        out_specs=row(0),
        out_shape=jax.ShapeDtypeStruct((t, D_MODEL), F32),
        compiler_params=_cparams(("parallel",)),
        name="mix",
    )(x2, oa2, ob2, main2, main2, mod8, wa, wb, wo)


def _mlp_kernel(x_ref, mod_ref, g_ref, wu_ref, wd_ref, o_ref):
    x = x_ref[...]
    h = _modulated_norm(x, g_ref[...], mod_ref[0, 3:4, :], mod_ref[0, 4:5, :]).astype(BF16)
    acc = jnp.zeros(x.shape, F32)
    for j in range(D_FF // D_MODEL):
        cs = slice(j * D_MODEL, (j + 1) * D_MODEL)
        u = jnp.maximum(jnp.dot(h, wu_ref[:, cs], preferred_element_type=F32), 0.0)
        acc = acc + jnp.dot((u * u).astype(BF16), wd_ref[cs, :], preferred_element_type=F32)
    o_ref[...] = x + mod_ref[0, 5:6, :] * acc


def _mlp(x2, mod8, g2, wu, wd, seq):
    t = x2.shape[0]
    tm = ROW_TILE
    once = pl.Buffered(1)
    return pl.pallas_call(
        _mlp_kernel,
        grid=(t // tm,),
        in_specs=[pl.BlockSpec((tm, D_MODEL), lambda i: (i, 0)),
                  pl.BlockSpec((1, 8, D_MODEL), lambda i: ((i * tm) // seq, 0, 0)),
                  pl.BlockSpec((1, D_MODEL), lambda i: (0, 0)),
                  pl.BlockSpec((D_MODEL, D_FF), lambda i: (0, 0), pipeline_mode=once),
                  pl.BlockSpec((D_FF, D_MODEL), lambda i: (0, 0), pipeline_mode=once)],
        out_specs=pl.BlockSpec((tm, D_MODEL), lambda i: (i, 0)),
        out_shape=jax.ShapeDtypeStruct((t, D_MODEL), F32),
        compiler_params=_cparams(("parallel",)),
        name="mlp",
    )(x2, mod8, g2, wu, wd)


def _split_w_in(w_in):
    o = np.cumsum([0, 1024, 1024, 1024, 512, 512, 1024, 1024, GATE_RANK, GATE_RANK, 1024, 1024])
    main = jnp.concatenate([w_in[:, o[0]:o[7]], w_in[:, o[9]:o[11]]], axis=1)
    lr = jnp.pad(w_in[:, o[7]:o[9]], ((0, 0), (0, LR_W - 2 * GATE_RANK)))
    return main.astype(BF16), lr.astype(BF16)


def _pad_gate_w(w_gate, row0):
    return jnp.pad(w_gate, ((row0, LR_W - GATE_RANK - row0), (0, 0))).astype(BF16)


def _layer(x, mod8, layer_idx, p):
    b, s, _ = x.shape
    t = b * s
    lambda_init = 0.8 - 0.6 * math.exp(-0.3 * layer_idx)
    x2 = x.reshape(t, D_MODEL)
    row = lambda v: v.reshape(1, -1).astype(F32)

    kg2 = row(jnp.tile(p["k_norm_g"], 2))
    qt, kn2, vt, main2, lr2 = _proj(x2, mod8, row(p["norm1_g"]), p["w_main"], p["w_lr"],
                                    row(jnp.tile(p["q_norm_g"], 2)), kg2, s)
    kn = kn2.reshape(b, s, D_MODEL)
    main3 = main2.reshape(b, s, N_REST)
    lr3 = lr2.reshape(b, s, LR_W)
    lamv = jnp.zeros((8, HEAD_W_A), F32).at[0:4, :HEAD_DIM_A].set(
        jnp.stack([p["lam_q1"], p["lam_k1"], p["lam_q2"], p["lam_k2"]]).astype(F32))
    oa = _attn(p["rel_bias"], qt, kn, vt, p["bias_t"], kg2, lamv, row(p["subln_g"]), lambda_init)

    ob = _gla(main3, lr3, p["wgf"], row(p["b_gate_f"]), p["wgb"], row(p["b_gate_b"]), row(p["gla_norm_g"]))

    x1 = _mix(x2, oa.reshape(t, D_MODEL), ob.reshape(t, D_MODEL), main2, mod8,
              p["w_branch_a"], p["w_branch_b"], p["w_out"], s)
    y = _mlp(x1, mod8, row(p["norm2_g"]), p["w_up"], p["w_down"], s)
    return y.reshape(b, s, D_MODEL)


def kernel(x_prompt, x_sample, c_prompt, c_sample, rel_bias, w_ada, b_ada, norm1_g, w_in, q_norm_g, k_norm_g, lam_q1, lam_k1, lam_q2, lam_k2, subln_g, w_gate_f, b_gate_f, w_gate_b, b_gate_b, gla_norm_g, w_branch_a, w_branch_b, w_out, norm2_g, w_up, w_down):
    depth = w_in.shape[0]
    nb_p, nb_s = c_prompt.shape[0], c_sample.shape[0]
    rows = -(-(nb_p + nb_s) // 8) * 8
    c_all = jnp.pad(jnp.concatenate([c_prompt, c_sample], axis=0), ((0, rows - nb_p - nb_s), (0, 0)))
    bias_t = _bias_tiles(rel_bias.astype(F32))

    xp, xs = x_prompt, x_sample
    for l in range(depth):
        mod = _ada(c_all, w_ada[l], b_ada[l]).reshape(rows, 6, D_MODEL)
        mod8 = jnp.pad(mod, ((0, 0), (0, 2), (0, 0)))
        w_main, w_lr = _split_w_in(w_in[l])
        p = dict(rel_bias=rel_bias.astype(F32), bias_t=bias_t, norm1_g=norm1_g[l], w_main=w_main, w_lr=w_lr,
                 q_norm_g=q_norm_g[l], k_norm_g=k_norm_g[l], lam_q1=lam_q1[l], lam_k1=lam_k1[l],
                 lam_q2=lam_q2[l], lam_k2=lam_k2[l], subln_g=subln_g[l],
                 wgf=_pad_gate_w(w_gate_f[l], 0), b_gate_f=b_gate_f[l],
                 wgb=_pad_gate_w(w_gate_b[l], GATE_RANK), b_gate_b=b_gate_b[l],
                 gla_norm_g=gla_norm_g[l], w_branch_a=w_branch_a[l].astype(BF16),
                 w_branch_b=w_branch_b[l].astype(BF16), w_out=w_out[l].astype(BF16),
                 norm2_g=norm2_g[l], w_up=w_up[l].astype(BF16), w_down=w_down[l].astype(BF16))
        xp = _layer(xp, mod8[:nb_p], l, p)
        xs = _layer(xs, mod8[nb_p:nb_p + nb_s], l, p)
    return (xp, xs)
```

```python
import functools
import math

import jax
import jax.numpy as jnp
import numpy as np
from jax import lax
from jax.experimental import pallas as pl
from jax.experimental.pallas import tpu as pltpu

F32 = jnp.float32
BF16 = jnp.bfloat16

D_MODEL = 1024
HEAD_DIM_A = 64
N_HEADS_A = 8
HEAD_W_A = 2 * HEAD_DIM_A
N_HEADS_B = 4
KEY_DIM_B = 128
V_DIM_B = 256
GATE_RANK = 16
GATE_NORM = 16.0
N_BUCKETS = 32
MAX_DISTANCE = 128
D_FF = 4 * D_MODEL
EPS = 1e-6
LOG2E = math.log2(math.e)

N_MAIN = 8 * D_MODEL
N_REST = 5 * D_MODEL
LR_W = 128

VMEM_LIMIT = 56 * 1024 * 1024
ROW_TILE = 512
ATT_TILE = 256
ATT_BLOCK = 16
NEAR_TILES = 5
EXP2_RANGE = 100.0
GLA_CHUNK = 128
SUBLANES = 8


def _cparams(sem):
    return pltpu.CompilerParams(dimension_semantics=sem, vmem_limit_bytes=VMEM_LIMIT)


def _ada_kernel(c_ref, w_ref, b_ref, o_ref):
    c = c_ref[...]
    a = c / (1.0 + jnp.exp(-c))
    o_ref[...] = jnp.dot(a, w_ref[...], preferred_element_type=F32,
                         precision=lax.Precision.HIGHEST) + b_ref[...]


def _ada(c_all, w_ada, b_ada):
    r = c_all.shape[0]
    tn = 1024
    return pl.pallas_call(
        _ada_kernel,
        grid=(6 * D_MODEL // tn,),
        in_specs=[pl.BlockSpec((r, D_MODEL), lambda n: (0, 0)),
                  pl.BlockSpec((D_MODEL, tn), lambda n: (0, n)),
                  pl.BlockSpec((1, tn), lambda n: (0, n))],
        out_specs=pl.BlockSpec((r, tn), lambda n: (0, n)),
        out_shape=jax.ShapeDtypeStruct((r, 6 * D_MODEL), F32),
        compiler_params=_cparams(("parallel",)),
        name="ada",
    )(c_all, w_ada, b_ada.reshape(1, -1))


def _modulated_norm(x, g, shift, scale):
    ms = jnp.mean(x * x, axis=-1, keepdims=True)
    return x * lax.rsqrt(ms + EPS) * g * (1.0 + scale) + shift


def _pair_rms(x, g):
    lane = lax.broadcasted_iota(jnp.int32, x.shape, 1)
    x2 = x * x
    s_all = jnp.sum(x2, axis=-1, keepdims=True)
    s_lo = jnp.sum(jnp.where(lane < HEAD_DIM_A, x2, 0.0), axis=-1, keepdims=True)
    ss = jnp.where(lane < HEAD_DIM_A, s_lo, s_all - s_lo)
    return x * lax.rsqrt(ss * (1.0 / HEAD_DIM_A) + EPS) * g


def _proj_kernel(x_ref, mod_ref, g_ref, w_ref, wlr_ref, qg_ref, kg_ref,
                 qt_ref, kn_ref, vt_ref, o_ref, lr_ref):
    h = _modulated_norm(x_ref[...], g_ref[...], mod_ref[0, 0:1, :], mod_ref[0, 1:2, :]).astype(BF16)
    lr_ref[...] = jnp.dot(h, wlr_ref[...], preferred_element_type=F32)
    piece = lambda j: jnp.dot(h, w_ref[:, j * D_MODEL:(j + 1) * D_MODEL], preferred_element_type=F32)
    heads = [slice(i * HEAD_W_A, (i + 1) * HEAD_W_A) for i in range(N_HEADS_A)]

    qa = piece(0)
    for i, sl in enumerate(heads):
        qn = _pair_rms(qa[:, sl], qg_ref[...]) * (HEAD_DIM_A ** -0.5 * LOG2E)
        qt_ref[0, i] = qn.T.astype(BF16)
    ka = piece(1)
    for sl in heads:
        kn_ref[:, sl] = _pair_rms(ka[:, sl], kg_ref[...]).astype(BF16)
    va = piece(2)
    for i, sl in enumerate(heads):
        vt_ref[0, i] = va[:, sl].T.astype(BF16)
    for j in range(3, N_MAIN // D_MODEL):
        o_ref[:, (j - 3) * D_MODEL:(j - 2) * D_MODEL] = piece(j).astype(o_ref.dtype)


def _proj(x2, mod8, g1, w_main, w_lr, qg2, kg2, seq):
    t = x2.shape[0]
    tm = ROW_TILE
    spt = seq // tm
    once = pl.Buffered(1)
    const = lambda shape, **kw: pl.BlockSpec(shape, lambda i: (0,) * len(shape), **kw)
    tspec = pl.BlockSpec((1, N_HEADS_A, HEAD_W_A, tm), lambda i: (i // spt, 0, 0, i % spt))
    tshape = jax.ShapeDtypeStruct((t // seq, N_HEADS_A, HEAD_W_A, seq), BF16)
    return pl.pallas_call(
        _proj_kernel,
        grid=(t // tm,),
        in_specs=[pl.BlockSpec((tm, D_MODEL), lambda i: (i, 0)),
                  pl.BlockSpec((1, 8, D_MODEL), lambda i: (i // spt, 0, 0)),
                  const((1, D_MODEL)),
                  const((D_MODEL, N_MAIN), pipeline_mode=once),
                  const((D_MODEL, LR_W), pipeline_mode=once),
                  const((1, HEAD_W_A)), const((1, HEAD_W_A))],
        out_specs=[tspec,
                   pl.BlockSpec((tm, D_MODEL), lambda i: (i, 0)),
                   tspec,
                   pl.BlockSpec((tm, N_REST), lambda i: (i, 0)),
                   pl.BlockSpec((tm, LR_W), lambda i: (i, 0))],
        out_shape=[tshape,
                   jax.ShapeDtypeStruct((t, D_MODEL), BF16),
                   tshape,
                   jax.ShapeDtypeStruct((t, N_REST), BF16),
                   jax.ShapeDtypeStruct((t, LR_W), F32)],
        compiler_params=_cparams(("parallel",)),
        name="proj",
    )(x2, mod8, g1, w_main, w_lr, qg2, kg2)


def _t5_bucket(rel):
    nb = N_BUCKETS // 2
    max_exact = nb // 2
    ret = (rel > 0).astype(jnp.int32) * nb
    n = jnp.abs(rel)
    nf = jnp.maximum(n, 1).astype(F32)
    large = max_exact + (jnp.log(nf / max_exact) / math.log(MAX_DISTANCE / max_exact)
                         * (nb - max_exact)).astype(jnp.int32)
    large = jnp.minimum(large, nb - 1)
    return ret + jnp.where(n < max_exact, n, large)


def _bias_kernel(tbl_ref, bkt_ref, qg_ref, kg_ref, o_ref, ok_ref):
    h = pl.program_id(0)
    bkt = bkt_ref[...]
    acc = jnp.zeros(bkt.shape, F32)
    babs = jnp.float32(0.0)
    for i in range(N_BUCKETS):
        acc = jnp.where(bkt == i, tbl_ref[i, h], acc)
        babs = jnp.maximum(babs, jnp.abs(tbl_ref[i, h]))
    o_ref[0] = acc * LOG2E
    gmax = lambda r: jnp.max(jnp.abs(r[...]), axis=1, keepdims=True)
    bound = gmax(qg_ref) * gmax(kg_ref) * (1.02 * HEAD_DIM_A * HEAD_DIM_A ** -0.5 * LOG2E) + babs * LOG2E
    ok_ref[0] = jnp.broadcast_to((bound <= EXP2_RANGE).astype(jnp.int32), ok_ref.shape[1:])


def _bias_tiles(rel_bias, qg2, kg2):
    t = ATT_TILE
    kk = jnp.arange(t, dtype=jnp.int32)[:, None]
    qq = jnp.arange(t, dtype=jnp.int32)[None, :]
    rel = jnp.stack([(d - NEAR_TILES // 2) * t + kk - qq for d in range(NEAR_TILES)])
    gspec = pl.BlockSpec((1, HEAD_W_A), lambda h: (0, 0))
    tiles, ok = pl.pallas_call(
        _bias_kernel,
        grid=(N_HEADS_A,),
        in_specs=[pl.BlockSpec(memory_space=pltpu.SMEM),
                  pl.BlockSpec((NEAR_TILES, t, t), lambda h: (0, 0, 0)), gspec, gspec],
        out_specs=[pl.BlockSpec((1, NEAR_TILES, t, t), lambda h: (h, 0, 0, 0)),
                   pl.BlockSpec((1, 8, HEAD_W_A), lambda h: (h, 0, 0))],
        out_shape=[jax.ShapeDtypeStruct((N_HEADS_A, NEAR_TILES, t, t), F32),
                   jax.ShapeDtypeStruct((N_HEADS_A, 8, HEAD_W_A), jnp.int32)],
        compiler_params=_cparams(("parallel",)),
        name="bias",
    )(rel_bias, _t5_bucket(rel), qg2, kg2)
    return tiles, ok[:, 0, 0]


def _attn_kernel(ok_ref, qt_ref, k_ref, vt_ref, bias_ref, lamv_ref, subg_ref, o_ref,
                 w_scr, m_scr, l_scr, acc_scr, l8_scr, *, nk, lambda_init):
    t = ATT_TILE
    blk = min(ATT_BLOCK, nk)
    h = pl.program_id(1)
    qi = pl.program_id(2)

    q = qt_ref[0, 0]
    z = jnp.zeros((HEAD_DIM_A, t), BF16)
    w_scr[:, :t] = jnp.concatenate([q[:HEAD_DIM_A], z], axis=0)
    w_scr[:, t:] = jnp.concatenate([z, q[HEAD_DIM_A:]], axis=0)

    fast = ok_ref[h] != 0

    def chunk_bias(kj):
        return bias_ref[0, jnp.clip(kj - qi, -2, 2) + 2]

    def scores(kj, m):
        k0 = pl.multiple_of(kj * t, t)
        return jnp.dot(k_ref[0, pl.ds(k0, t), :], w_scr[:, m * t:(m + 1) * t],
                       preferred_element_type=F32) + chunk_bias(kj)

    def values(kj):
        return vt_ref[0, 0, :, pl.ds(pl.multiple_of(kj * t, t), t)]

    @pl.when(fast)
    def _():
        l8_scr[...] = jnp.zeros(l8_scr.shape, F32)
        acc_scr[...] = jnp.zeros(acc_scr.shape, F32)

        bt = blk * t
        nb = nk // blk

        def block(j, carry):
            k0 = pl.multiple_of(j * bt, bt)
            s = jnp.dot(k_ref[0, pl.ds(k0, bt), :], w_scr[...], preferred_element_type=F32)
            p = []
            for c in range(blk):
                b = chunk_bias(j * blk + c)
                p.append(jnp.exp2(s[c * t:(c + 1) * t] + jnp.concatenate([b, b], axis=1)))
            p = jnp.concatenate(p, axis=0)
            l8_scr[...] += jnp.sum(p.reshape(bt // 8, 8, 2 * t), axis=0)
            acc_scr[...] += jnp.dot(vt_ref[0, 0, :, pl.ds(k0, bt)], p.astype(BF16),
                                    preferred_element_type=F32)
            return carry

        lax.fori_loop(0, nb, block, 0)
        l_scr[...] = jnp.sum(l8_scr[...], axis=0, keepdims=True)

    @pl.when(jnp.logical_not(fast))
    def _():
        m_scr[...] = jnp.full(m_scr.shape, -jnp.inf, F32)
        l_scr[...] = jnp.zeros(l_scr.shape, F32)
        acc_scr[...] = jnp.zeros(acc_scr.shape, F32)

        def online_step(kj, c):
            for m in range(2):
                cols = slice(m * t, (m + 1) * t)
                s = scores(kj, m)
                m_old = m_scr[:, cols]
                m_new = jnp.maximum(m_old, jnp.max(s, axis=0, keepdims=True))
                alpha = jnp.exp2(m_old - m_new)
                p = jnp.exp2(s - m_new)
                l_scr[:, cols] = alpha * l_scr[:, cols] + jnp.sum(p, axis=0, keepdims=True)
                pv = jnp.dot(values(kj), p.astype(BF16), preferred_element_type=F32)
                acc_scr[:, cols] = alpha * acc_scr[:, cols] + pv
                m_scr[:, cols] = m_new
            return c

        lax.fori_loop(0, nk, online_step, 0)

    lv = lamv_ref[...]
    lam = (jnp.exp(jnp.sum(lv[0:1] * lv[1:2], axis=-1, keepdims=True))
           - jnp.exp(jnp.sum(lv[2:3] * lv[3:4], axis=-1, keepdims=True)) + lambda_init)
    acc = acc_scr[...]
    inv = 1.0 / l_scr[...]
    o = acc[:, :t] * inv[:, :t] - lam * (acc[:, t:] * inv[:, t:])
    ms = jnp.mean(o * o, axis=0, keepdims=True)
    y = (o * lax.rsqrt(ms + EPS)).T
    o_ref[0] = (y * subg_ref[...] * (1.0 - lambda_init)).astype(o_ref.dtype)


def _attn(range_ok, qt, kn, vt, bias_t, lamv, subg, lambda_init):
    b, s, _ = kn.shape
    t = ATT_TILE
    nk = s // t
    assert nk % min(ATT_BLOCK, nk) == 0 and t >= MAX_DISTANCE
    kern = functools.partial(_attn_kernel, nk=nk, lambda_init=lambda_init)
    const = lambda shape: pl.BlockSpec(shape, lambda bi, h, qi: (0,) * len(shape))
    return pl.pallas_call(
        kern,
        grid=(b, N_HEADS_A, s // t),
        in_specs=[pl.BlockSpec(memory_space=pltpu.SMEM),
                  pl.BlockSpec((1, 1, HEAD_W_A, t), lambda bi, h, qi: (bi, h, 0, qi)),
                  pl.BlockSpec((1, s, HEAD_W_A), lambda bi, h, qi: (bi, 0, h)),
                  pl.BlockSpec((1, 1, HEAD_W_A, s), lambda bi, h, qi: (bi, h, 0, 0)),
                  pl.BlockSpec((1, NEAR_TILES, t, t), lambda bi, h, qi: (h, 0, 0, 0)),
                  const((8, HEAD_W_A)), const((1, HEAD_W_A))],
        out_specs=pl.BlockSpec((1, t, HEAD_W_A), lambda bi, h, qi: (bi, qi, h)),
        out_shape=jax.ShapeDtypeStruct((b, s, D_MODEL), BF16),
        scratch_shapes=[pltpu.VMEM((HEAD_W_A, 2 * t), BF16),
                        pltpu.VMEM((1, 2 * t), F32),
                        pltpu.VMEM((1, 2 * t), F32),
                        pltpu.VMEM((HEAD_W_A, 2 * t), F32),
                        pltpu.VMEM((8, 2 * t), F32)],
        compiler_params=_cparams(("parallel", "parallel", "parallel")),
        name="attn",
    )(range_ok, qt, kn, vt, bias_t, lamv, subg)


def _log_gate(lr, wg_ref, bg_ref):
    x = jnp.dot(lr.astype(BF16), wg_ref[...], preferred_element_type=F32) + bg_ref[...]
    return (jnp.minimum(x, 0.0) - jnp.log1p(jnp.exp(-jnp.abs(x)))) * (LOG2E / GATE_NORM)


def _gla_bwd_kernel(q_ref, k_ref, v_ref, lr_ref, wg_ref, bg_ref, o_ref, s_scr):
    c = GLA_CHUNK

    @pl.when(pl.program_id(1) == 0)
    def _():
        s_scr[...] = jnp.zeros(s_scr.shape, F32)

    g = _log_gate(lr_ref[0], wg_ref, bg_ref)
    row = lax.broadcasted_iota(jnp.int32, g.shape, 0)
    suf = g
    sh = 1
    while sh < c:
        suf = suf + jnp.where(row < c - sh, pltpu.roll(suf, c - sh, 0), 0.0)
        sh *= 2
    tot = suf[0:1, :]
    for h in range(N_HEADS_B):
        sl = slice(h * KEY_DIM_B, (h + 1) * KEY_DIM_B)
        vs = slice(h * V_DIM_B, (h + 1) * V_DIM_B)
        q = q_ref[0][:, sl].astype(F32) * KEY_DIM_B ** -0.5
        k = k_ref[0][:, sl].astype(F32)
        qd = (q * jnp.exp2(suf[:, sl])).astype(BF16)
        kd = k * jnp.exp2(tot[:, sl] - suf[:, sl])
        st = s_scr[h]
        o_ref[0, :, vs] = jnp.dot(qd, st.astype(BF16), preferred_element_type=F32)
        dec = jnp.broadcast_to(jnp.exp2(tot[:, sl]), (c, KEY_DIM_B)).T[:, 0:1]
        s_scr[h] = st * dec + jnp.dot(kd.T.astype(BF16), v_ref[0][:, vs], preferred_element_type=F32)


def _gla_fwd_kernel(q_ref, k_ref, v_ref, og_ref, lr_ref, obi_ref, wgf_ref, bgf_ref, wgb_ref, bgb_ref,
                    gn_ref, o_ref, s_scr):
    c = GLA_CHUNK

    @pl.when(pl.program_id(1) == 0)
    def _():
        s_scr[...] = jnp.zeros(s_scr.shape, F32)

    lr = lr_ref[0]
    gf_all = _log_gate(lr, wgf_ref, bgf_ref)
    gb_all = _log_gate(lr, wgb_ref, bgb_ref)
    ri = lax.broadcasted_iota(jnp.int32, (c, c), 0)
    ci = lax.broadcasted_iota(jnp.int32, (c, c), 1)
    xr = jnp.bitwise_xor(ri, ci)
    nt = (((1,), (1,)), ((), ()))
    for h in range(N_HEADS_B):
        sl = slice(h * KEY_DIM_B, (h + 1) * KEY_DIM_B)
        vs = slice(h * V_DIM_B, (h + 1) * V_DIM_B)
        q = q_ref[0][:, sl].astype(F32) * KEY_DIM_B ** -0.5
        k = k_ref[0][:, sl].astype(F32)
        v = v_ref[0][:, vs]
        pre = gf_all[:, sl]
        tf = pre
        suf = gb_all[:, sl]
        tb = suf
        a = jnp.where(ri == ci, 2.0 * lax.dot_general(q.astype(BF16), k.astype(BF16), nt,
                                                      preferred_element_type=F32), 0.0)
        blk = 1
        def level_scores(a, blk, q_arg, k_arg):
            r = lax.dot_general((q * jnp.exp2(q_arg)).astype(BF16), (k * jnp.exp2(k_arg)).astype(BF16),
                                nt, preferred_element_type=F32)
            return jnp.where(jnp.logical_and(xr >= blk, xr < 2 * blk), r, a)

        while blk < SUBLANES:
            hi = jnp.bitwise_and(ri, blk) != 0
            a = level_scores(a, blk, jnp.where(hi, pre, suf), jnp.where(hi, tb - suf, tf - pre))
            tf_dn, tf_up = pltpu.roll(tf, blk, 0), pltpu.roll(tf, c - blk, 0)
            tb_dn, tb_up = pltpu.roll(tb, blk, 0), pltpu.roll(tb, c - blk, 0)
            pre = pre + jnp.where(hi, tf_dn, 0.0)
            suf = suf + jnp.where(hi, 0.0, tb_up)
            tf = tf + jnp.where(hi, tf_dn, tf_up)
            tb = tb + jnp.where(hi, tb_dn, tb_up)
            blk *= 2
        ns = c // SUBLANES
        slabs = lambda x: [x[SUBLANES * r:SUBLANES * (r + 1)] for r in range(ns)]
        pre, suf, tf, tb = slabs(pre), slabs(suf), slabs(tf), slabs(tb)
        while blk < c:
            g = blk // SUBLANES
            up = [bool(r & g) for r in range(ns)]
            q_arg = [pre[r] if up[r] else suf[r] for r in range(ns)]
            k_arg = [tb[r] - suf[r] if up[r] else tf[r] - pre[r] for r in range(ns)]
            a = level_scores(a, blk, jnp.concatenate(q_arg, axis=0), jnp.concatenate(k_arg, axis=0))
            pre = [pre[r] + tf[r - g] if up[r] else pre[r] for r in range(ns)]
            suf = [suf[r] if up[r] else suf[r] + tb[r + g] for r in range(ns)]
            tf_pair = {r: tf[r] + tf[r + g] for r in range(ns) if not up[r]}
            tb_pair = {r: tb[r] + tb[r + g] for r in range(ns) if not up[r]}
            tf = [tf_pair[r - g] if up[r] else tf_pair[r] for r in range(ns)]
            tb = [tb_pair[r - g] if up[r] else tb_pair[r] for r in range(ns)]
            blk *= 2
        pre, tf = jnp.concatenate(pre, axis=0), jnp.concatenate(tf, axis=0)
        st = s_scr[h]
        o = (jnp.dot(a.astype(BF16), v, preferred_element_type=F32)
             + jnp.dot((q * jnp.exp2(pre)).astype(BF16), st.astype(BF16), preferred_element_type=F32)
             + obi_ref[0, :, vs])
        kd = k * jnp.exp2(tf - pre)
        s_scr[h] = st * jnp.exp2(tf).T[:, 0:1] + jnp.dot(kd.T.astype(BF16), v, preferred_element_type=F32)
        ms = jnp.mean(o * o, axis=-1, keepdims=True)
        og = og_ref[0][:, vs].astype(F32)
        o_ref[0, :, vs] = (o * lax.rsqrt(ms + EPS) * gn_ref[...] * (og / (1.0 + jnp.exp(-og)))
                           ).astype(o_ref.dtype)


def _gla(main3, lr3, wgf, bgf, wgb, bgb, gn):
    b, s, _ = main3.shape
    c = GLA_CHUNK
    nc = s // c
    hk = N_HEADS_B * KEY_DIM_B
    state = pltpu.VMEM((N_HEADS_B, KEY_DIM_B, V_DIM_B), F32)

    def specs(cidx):
        return [pl.BlockSpec((1, c, hk), lambda bi, i: (bi, cidx(i), 0)),
                pl.BlockSpec((1, c, hk), lambda bi, i: (bi, cidx(i), 1)),
                pl.BlockSpec((1, c, D_MODEL), lambda bi, i: (bi, cidx(i), 1))]

    full = lambda shape: pl.BlockSpec(shape, lambda bi, i: (0,) * len(shape))
    rev = lambda i: nc - 1 - i
    obi = pl.pallas_call(
        _gla_bwd_kernel,
        grid=(b, nc),
        in_specs=specs(rev) + [pl.BlockSpec((1, c, LR_W), lambda bi, i: (bi, rev(i), 0)),
                               full((LR_W, hk)), full((1, hk))],
        out_specs=pl.BlockSpec((1, c, D_MODEL), lambda bi, i: (bi, rev(i), 0)),
        out_shape=jax.ShapeDtypeStruct((b, s, D_MODEL), F32),
        scratch_shapes=[state],
        compiler_params=_cparams(("parallel", "arbitrary")),
        name="gla_bwd",
    )(main3, main3, main3, lr3, wgb, bgb)

    fwd = lambda i: i
    return pl.pallas_call(
        _gla_fwd_kernel,
        grid=(b, nc),
        in_specs=specs(fwd) + [pl.BlockSpec((1, c, D_MODEL), lambda bi, i: (bi, i, 2)),
                               pl.BlockSpec((1, c, LR_W), lambda bi, i: (bi, i, 0)),
                               pl.BlockSpec((1, c, D_MODEL), lambda bi, i: (bi, i, 0)),
                               full((LR_W, hk)), full((1, hk)), full((LR_W, hk)), full((1, hk)),
                               full((1, V_DIM_B))],
        out_specs=pl.BlockSpec((1, c, D_MODEL), lambda bi, i: (bi, i, 0)),
        out_shape=jax.ShapeDtypeStruct((b, s, D_MODEL), BF16),
        scratch_shapes=[state],
        compiler_params=_cparams(("parallel", "arbitrary")),
        name="gla_fwd",
    )(main3, main3, main3, main3, lr3, obi, wgf, bgf, wgb, bgb, gn)


def _mix_kernel(x_ref, oa_ref, ob_ref, ga_ref, gb_ref, mod_ref, wa_ref, wb_ref, wo_ref, o_ref):
    ya = jnp.dot(oa_ref[...], wa_ref[...], preferred_element_type=F32)
    yb = jnp.dot(ob_ref[...], wb_ref[...], preferred_element_type=F32)
    sig = lambda r: 1.0 / (1.0 + jnp.exp(-r[...].astype(F32)))
    merged = sig(ga_ref) * ya + sig(gb_ref) * yb
    o_ref[...] = x_ref[...] + mod_ref[0, 2:3, :] * jnp.dot(merged.astype(BF16), wo_ref[...],
                                                           preferred_element_type=F32)


def _mix(x2, oa2, ob2, main2, mod8, wa, wb, wo, seq):
    t = x2.shape[0]
    tm = ROW_TILE
    row = lambda j: pl.BlockSpec((tm, D_MODEL), lambda i: (i, j))
    wspec = pl.BlockSpec((D_MODEL, D_MODEL), lambda i: (0, 0))
    return pl.pallas_call(
        _mix_kernel,
        grid=(t // tm,),
        in_specs=[row(0), row(0), row(0), row(3), row(4),
                  pl.BlockSpec((1, 8, D_MODEL), lambda i: ((i * tm) // seq, 0, 0)),
                  wspec, wspec, wspec],
        out_specs=row(0),
        out_shape=jax.ShapeDtypeStruct((t, D_MODEL), F32),
        compiler_params=_cparams(("parallel",)),
        name="mix",
    )(x2, oa2, ob2, main2, main2, mod8, wa, wb, wo)


def _mlp_kernel(x_ref, mod_ref, g_ref, wu_ref, wd_ref, o_ref):
    x = x_ref[...]
    h = _modulated_norm(x, g_ref[...], mod_ref[0, 3:4, :], mod_ref[0, 4:5, :]).astype(BF16)
    acc = jnp.zeros(x.shape, F32)
    for j in range(D_FF // D_MODEL):
        cs = slice(j * D_MODEL, (j + 1) * D_MODEL)
        u = jnp.maximum(jnp.dot(h, wu_ref[:, cs], preferred_element_type=F32), 0.0)
        acc = acc + jnp.dot((u * u).astype(BF16), wd_ref[cs, :], preferred_element_type=F32)
    o_ref[...] = x + mod_ref[0, 5:6, :] * acc


def _mlp(x2, mod8, g2, wu, wd, seq):
    t = x2.shape[0]
    tm = ROW_TILE
    once = pl.Buffered(1)
    return pl.pallas_call(
        _mlp_kernel,
        grid=(t // tm,),
        in_specs=[pl.BlockSpec((tm, D_MODEL), lambda i: (i, 0)),
                  pl.BlockSpec((1, 8, D_MODEL), lambda i: ((i * tm) // seq, 0, 0)),
                  pl.BlockSpec((1, D_MODEL), lambda i: (0, 0)),
                  pl.BlockSpec((D_MODEL, D_FF), lambda i: (0, 0), pipeline_mode=once),
                  pl.BlockSpec((D_FF, D_MODEL), lambda i: (0, 0), pipeline_mode=once)],
        out_specs=pl.BlockSpec((tm, D_MODEL), lambda i: (i, 0)),
        out_shape=jax.ShapeDtypeStruct((t, D_MODEL), F32),
        compiler_params=_cparams(("parallel",)),
        name="mlp",
    )(x2, mod8, g2, wu, wd)


def _split_w_in(w_in):
    o = np.cumsum([0, 1024, 1024, 1024, 512, 512, 1024, 1024, GATE_RANK, GATE_RANK, 1024, 1024])
    main = jnp.concatenate([w_in[:, o[0]:o[7]], w_in[:, o[9]:o[11]]], axis=1)
    lr = jnp.pad(w_in[:, o[7]:o[9]], ((0, 0), (0, LR_W - 2 * GATE_RANK)))
    return main.astype(BF16), lr.astype(BF16)


def _pad_gate_w(w_gate, row0):
    return jnp.pad(w_gate, ((row0, LR_W - GATE_RANK - row0), (0, 0))).astype(BF16)


def _layer(x, mod8, layer_idx, p):
    b, s, _ = x.shape
    t = b * s
    lambda_init = 0.8 - 0.6 * math.exp(-0.3 * layer_idx)
    x2 = x.reshape(t, D_MODEL)
    row = lambda v: v.reshape(1, -1).astype(F32)

    qg2, kg2 = row(jnp.tile(p["q_norm_g"], 2)), row(jnp.tile(p["k_norm_g"], 2))
    qt, kn2, vt, main2, lr2 = _proj(x2, mod8, row(p["norm1_g"]), p["w_main"], p["w_lr"], qg2, kg2, s)
    kn = kn2.reshape(b, s, D_MODEL)
    main3 = main2.reshape(b, s, N_REST)
    lr3 = lr2.reshape(b, s, LR_W)
    lamv = jnp.zeros((8, HEAD_W_A), F32).at[0:4, :HEAD_DIM_A].set(
        jnp.stack([p["lam_q1"], p["lam_k1"], p["lam_q2"], p["lam_k2"]]).astype(F32))
    bias_t, range_ok = _bias_tiles(p["rel_bias"], qg2, kg2)
    oa = _attn(range_ok, qt, kn, vt, bias_t, lamv, row(p["subln_g"]), lambda_init)

    ob = _gla(main3, lr3, p["wgf"], row(p["b_gate_f"]), p["wgb"], row(p["b_gate_b"]), row(p["gla_norm_g"]))

    x1 = _mix(x2, oa.reshape(t, D_MODEL), ob.reshape(t, D_MODEL), main2, mod8,
              p["w_branch_a"], p["w_branch_b"], p["w_out"], s)
    y = _mlp(x1, mod8, row(p["norm2_g"]), p["w_up"], p["w_down"], s)
    return y.reshape(b, s, D_MODEL)


def kernel(x_prompt, x_sample, c_prompt, c_sample, rel_bias, w_ada, b_ada, norm1_g, w_in, q_norm_g, k_norm_g, lam_q1, lam_k1, lam_q2, lam_k2, subln_g, w_gate_f, b_gate_f, w_gate_b, b_gate_b, gla_norm_g, w_branch_a, w_branch_b, w_out, norm2_g, w_up, w_down):
    depth = w_in.shape[0]
    nb_p, nb_s = c_prompt.shape[0], c_sample.shape[0]
    rows = -(-(nb_p + nb_s) // 8) * 8
    c_all = jnp.pad(jnp.concatenate([c_prompt, c_sample], axis=0), ((0, rows - nb_p - nb_s), (0, 0)))

    xp, xs = x_prompt, x_sample
    for l in range(depth):
        mod = _ada(c_all, w_ada[l], b_ada[l]).reshape(rows, 6, D_MODEL)
        mod8 = jnp.pad(mod, ((0, 0), (0, 2), (0, 0)))
        w_main, w_lr = _split_w_in(w_in[l])
        p = dict(rel_bias=rel_bias.astype(F32), norm1_g=norm1_g[l], w_main=w_main, w_lr=w_lr,
                 q_norm_g=q_norm_g[l], k_norm_g=k_norm_g[l], lam_q1=lam_q1[l], lam_k1=lam_k1[l],
                 lam_q2=lam_q2[l], lam_k2=lam_k2[l], subln_g=subln_g[l],
                 wgf=_pad_gate_w(w_gate_f[l], 0), b_gate_f=b_gate_f[l],
                 wgb=_pad_gate_w(w_gate_b[l], GATE_RANK), b_gate_b=b_gate_b[l],
                 gla_norm_g=gla_norm_g[l], w_branch_a=w_branch_a[l].astype(BF16),
                 w_branch_b=w_branch_b[l].astype(BF16), w_out=w_out[l].astype(BF16),
                 norm2_g=norm2_g[l], w_up=w_up[l].astype(BF16), w_down=w_down[l].astype(BF16))
        xp = _layer(xp, mod8[:nb_p], l, p)
        xs = _layer(xs, mod8[nb_p:nb_p + nb_s], l, p)
    return (xp, xs)
```

```python
import functools
import math

import jax
import jax.numpy as jnp
import numpy as np
from jax import lax
from jax.experimental import pallas as pl
from jax.experimental.pallas import tpu as pltpu

F32 = jnp.float32
BF16 = jnp.bfloat16

D_MODEL = 1024
HEAD_DIM_A = 64
N_HEADS_A = 8
HEAD_W_A = 2 * HEAD_DIM_A
N_HEADS_B = 4
KEY_DIM_B = 128
V_DIM_B = 256
GATE_RANK = 16
GATE_NORM = 16.0
N_BUCKETS = 32
MAX_DISTANCE = 128
D_FF = 4 * D_MODEL
EPS = 1e-6
LOG2E = math.log2(math.e)

N_MAIN = 8 * D_MODEL
N_REST = 5 * D_MODEL
LR_W = 128

VMEM_LIMIT = 56 * 1024 * 1024
ROW_TILE = 512
ATT_TILE = 256
ATT_BLOCK = 16
NEAR_TILES = 5
EXP2_RANGE = 100.0
GLA_CHUNK = 128
SUBLANES = 8


def _cparams(sem):
    return pltpu.CompilerParams(dimension_semantics=sem, vmem_limit_bytes=VMEM_LIMIT)


def _ada_kernel(c_ref, w_ref, b_ref, o_ref):
    c = c_ref[...]
    a = c / (1.0 + jnp.exp(-c))
    o_ref[...] = jnp.dot(a, w_ref[...], preferred_element_type=F32,
                         precision=lax.Precision.HIGHEST) + b_ref[...]


def _ada(c_all, w_ada, b_ada):
    r = c_all.shape[0]
    tn = 1024
    return pl.pallas_call(
        _ada_kernel,
        grid=(6 * D_MODEL // tn,),
        in_specs=[pl.BlockSpec((r, D_MODEL), lambda n: (0, 0)),
                  pl.BlockSpec((D_MODEL, tn), lambda n: (0, n)),
                  pl.BlockSpec((1, tn), lambda n: (0, n))],
        out_specs=pl.BlockSpec((r, tn), lambda n: (0, n)),
        out_shape=jax.ShapeDtypeStruct((r, 6 * D_MODEL), F32),
        compiler_params=_cparams(("parallel",)),
        name="ada",
    )(c_all, w_ada, b_ada.reshape(1, -1))


def _modulated_norm(x, g, shift, scale):
    ms = jnp.mean(x * x, axis=-1, keepdims=True)
    return x * lax.rsqrt(ms + EPS) * g * (1.0 + scale) + shift


def _pair_rms(x, g):
    lane = lax.broadcasted_iota(jnp.int32, x.shape, 1)
    x2 = x * x
    s_all = jnp.sum(x2, axis=-1, keepdims=True)
    s_lo = jnp.sum(jnp.where(lane < HEAD_DIM_A, x2, 0.0), axis=-1, keepdims=True)
    ss = jnp.where(lane < HEAD_DIM_A, s_lo, s_all - s_lo)
    return x * lax.rsqrt(ss * (1.0 / HEAD_DIM_A) + EPS) * g


def _proj_kernel(x_ref, mod_ref, g_ref, w_ref, wlr_ref, qg_ref, kg_ref,
                 qt_ref, kn_ref, vt_ref, o_ref, lr_ref):
    h = _modulated_norm(x_ref[...], g_ref[...], mod_ref[0, 0:1, :], mod_ref[0, 1:2, :]).astype(BF16)
    lr_ref[...] = jnp.dot(h, wlr_ref[...], preferred_element_type=F32)
    piece = lambda j: jnp.dot(h, w_ref[:, j * D_MODEL:(j + 1) * D_MODEL], preferred_element_type=F32)
    heads = [slice(i * HEAD_W_A, (i + 1) * HEAD_W_A) for i in range(N_HEADS_A)]

    qa = piece(0)
    for i, sl in enumerate(heads):
        qn = _pair_rms(qa[:, sl], qg_ref[...]) * (HEAD_DIM_A ** -0.5 * LOG2E)
        qt_ref[0, i] = qn.T.astype(BF16)
    ka = piece(1)
    for sl in heads:
        kn_ref[:, sl] = _pair_rms(ka[:, sl], kg_ref[...]).astype(BF16)
    va = piece(2)
    for i, sl in enumerate(heads):
        vt_ref[0, i] = va[:, sl].T.astype(BF16)
    for j in range(3, N_MAIN // D_MODEL):
        o_ref[:, (j - 3) * D_MODEL:(j - 2) * D_MODEL] = piece(j).astype(o_ref.dtype)


def _proj(x2, mod8, g1, w_main, w_lr, qg2, kg2, seq):
    t = x2.shape[0]
    tm = ROW_TILE
    spt = seq // tm
    once = pl.Buffered(1)
    const = lambda shape, **kw: pl.BlockSpec(shape, lambda i: (0,) * len(shape), **kw)
    tspec = pl.BlockSpec((1, N_HEADS_A, HEAD_W_A, tm), lambda i: (i // spt, 0, 0, i % spt))
    tshape = jax.ShapeDtypeStruct((t // seq, N_HEADS_A, HEAD_W_A, seq), BF16)
    return pl.pallas_call(
        _proj_kernel,
        grid=(t // tm,),
        in_specs=[pl.BlockSpec((tm, D_MODEL), lambda i: (i, 0)),
                  pl.BlockSpec((1, 8, D_MODEL), lambda i: (i // spt, 0, 0)),
                  const((1, D_MODEL)),
                  const((D_MODEL, N_MAIN), pipeline_mode=once),
                  const((D_MODEL, LR_W), pipeline_mode=once),
                  const((1, HEAD_W_A)), const((1, HEAD_W_A))],
        out_specs=[tspec,
                   pl.BlockSpec((tm, D_MODEL), lambda i: (i, 0)),
                   tspec,
                   pl.BlockSpec((tm, N_REST), lambda i: (i, 0)),
                   pl.BlockSpec((tm, LR_W), lambda i: (i, 0))],
        out_shape=[tshape,
                   jax.ShapeDtypeStruct((t, D_MODEL), BF16),
                   tshape,
                   jax.ShapeDtypeStruct((t, N_REST), BF16),
                   jax.ShapeDtypeStruct((t, LR_W), F32)],
        compiler_params=_cparams(("parallel",)),
        name="proj",
    )(x2, mod8, g1, w_main, w_lr, qg2, kg2)


def _t5_bucket(rel):
    nb = N_BUCKETS // 2
    max_exact = nb // 2
    ret = (rel > 0).astype(jnp.int32) * nb
    n = jnp.abs(rel)
    nf = jnp.maximum(n, 1).astype(F32)
    large = max_exact + (jnp.log(nf / max_exact) / math.log(MAX_DISTANCE / max_exact)
                         * (nb - max_exact)).astype(jnp.int32)
    large = jnp.minimum(large, nb - 1)
    return ret + jnp.where(n < max_exact, n, large)


def _bias_kernel(tbl_ref, bkt_ref, qg_ref, kg_ref, o_ref, ok_ref):
    h = pl.program_id(0)
    bkt = bkt_ref[...]
    acc = jnp.zeros(bkt.shape, F32)
    babs = jnp.float32(0.0)
    for i in range(N_BUCKETS):
        acc = jnp.where(bkt == i, tbl_ref[i, h], acc)
        babs = jnp.maximum(babs, jnp.abs(tbl_ref[i, h]))
    o_ref[0] = acc * LOG2E
    gmax = lambda r: jnp.max(jnp.abs(r[...]), axis=1, keepdims=True)
    bound = gmax(qg_ref) * gmax(kg_ref) * (1.02 * HEAD_DIM_A * HEAD_DIM_A ** -0.5 * LOG2E) + babs * LOG2E
    ok_ref[0] = jnp.broadcast_to((bound <= EXP2_RANGE).astype(jnp.int32), ok_ref.shape[1:])


def _bias_tiles(rel_bias, qg2, kg2):
    t = ATT_TILE
    kk = jnp.arange(t, dtype=jnp.int32)[:, None]
    qq = jnp.arange(t, dtype=jnp.int32)[None, :]
    rel = jnp.stack([(d - NEAR_TILES // 2) * t + kk - qq for d in range(NEAR_TILES)])
    gspec = pl.BlockSpec((1, HEAD_W_A), lambda h: (0, 0))
    tiles, ok = pl.pallas_call(
        _bias_kernel,
        grid=(N_HEADS_A,),
        in_specs=[pl.BlockSpec(memory_space=pltpu.SMEM),
                  pl.BlockSpec((NEAR_TILES, t, t), lambda h: (0, 0, 0)), gspec, gspec],
        out_specs=[pl.BlockSpec((1, NEAR_TILES, t, t), lambda h: (h, 0, 0, 0)),
                   pl.BlockSpec((1, 8, HEAD_W_A), lambda h: (h, 0, 0))],
        out_shape=[jax.ShapeDtypeStruct((N_HEADS_A, NEAR_TILES, t, t), F32),
                   jax.ShapeDtypeStruct((N_HEADS_A, 8, HEAD_W_A), jnp.int32)],
        compiler_params=_cparams(("parallel",)),
        name="bias",
    )(rel_bias, _t5_bucket(rel), qg2, kg2)
    return tiles, ok[:, 0, 0]


def _attn_kernel(ok_ref, qt_ref, k_ref, vt_ref, bias_ref, lamv_ref, subg_ref, o_ref,
                 w_scr, m_scr, l_scr, acc_scr, l8_scr, *, nk, qsub, lambda_init):
    t = ATT_TILE
    blk = min(ATT_BLOCK, nk)
    bt = blk * t
    h = pl.program_id(1)
    tiles = [(u, pl.program_id(2) * qsub + u) for u in range(qsub)]

    z = jnp.zeros((HEAD_DIM_A, t), BF16)
    for u, _ in tiles:
        q = qt_ref[0, 0, :, u * t:(u + 1) * t]
        w_scr[u, :, :t] = jnp.concatenate([q[:HEAD_DIM_A], z], axis=0)
        w_scr[u, :, t:] = jnp.concatenate([z, q[HEAD_DIM_A:]], axis=0)

    def chunk_bias(kj, qi):
        b = bias_ref[0, jnp.clip(kj - qi, -2, 2) + 2]
        return jnp.concatenate([b, b], axis=1)

    @pl.when(ok_ref[h] != 0)
    def _():
        l8_scr[...] = jnp.zeros(l8_scr.shape, F32)
        acc_scr[...] = jnp.zeros(acc_scr.shape, F32)

        def block(j, carry):
            k0 = pl.multiple_of(j * bt, bt)
            for u, qi in tiles:
                s = jnp.dot(k_ref[0, pl.ds(k0, bt), :], w_scr[u], preferred_element_type=F32)
                p = [jnp.exp2(s[c * t:(c + 1) * t] + chunk_bias(j * blk + c, qi)) for c in range(blk)]
                p = jnp.concatenate(p, axis=0)
                l8_scr[u] += jnp.sum(p.reshape(bt // 8, 8, 2 * t), axis=0)
                acc_scr[u] += jnp.dot(vt_ref[0, 0, :, pl.ds(k0, bt)], p.astype(BF16),
                                      preferred_element_type=F32)
            return carry

        lax.fori_loop(0, nk // blk, block, 0)
        l_scr[...] = jnp.sum(l8_scr[...], axis=1, keepdims=True)

    @pl.when(ok_ref[h] == 0)
    def _():
        m_scr[...] = jnp.full(m_scr.shape, -jnp.inf, F32)
        l_scr[...] = jnp.zeros(l_scr.shape, F32)
        acc_scr[...] = jnp.zeros(acc_scr.shape, F32)

        def online_step(kj, carry):
            k0 = pl.multiple_of(kj * t, t)
            for u, qi in tiles:
                s = jnp.dot(k_ref[0, pl.ds(k0, t), :], w_scr[u], preferred_element_type=F32)
                s = s + chunk_bias(kj, qi)
                m_old = m_scr[u]
                m_new = jnp.maximum(m_old, jnp.max(s, axis=0, keepdims=True))
                alpha = jnp.exp2(m_old - m_new)
                p = jnp.exp2(s - m_new)
                l_scr[u] = alpha * l_scr[u] + jnp.sum(p, axis=0, keepdims=True)
                pv = jnp.dot(vt_ref[0, 0, :, pl.ds(k0, t)], p.astype(BF16), preferred_element_type=F32)
                acc_scr[u] = alpha * acc_scr[u] + pv
                m_scr[u] = m_new
            return carry

        lax.fori_loop(0, nk, online_step, 0)

    lv = lamv_ref[...]
    lam = (jnp.exp(jnp.sum(lv[0:1] * lv[1:2], axis=-1, keepdims=True))
           - jnp.exp(jnp.sum(lv[2:3] * lv[3:4], axis=-1, keepdims=True)) + lambda_init)
    for u, _ in tiles:
        acc = acc_scr[u]
        inv = 1.0 / l_scr[u]
        o = acc[:, :t] * inv[:, :t] - lam * (acc[:, t:] * inv[:, t:])
        ms = jnp.mean(o * o, axis=0, keepdims=True)
        y = (o * lax.rsqrt(ms + EPS)).T
        o_ref[0, u * t:(u + 1) * t, :] = (y * subg_ref[...] * (1.0 - lambda_init)).astype(o_ref.dtype)


def _attn(range_ok, qt, kn, vt, bias_t, lamv, subg, lambda_init):
    b, s, _ = kn.shape
    t = ATT_TILE
    nk = s // t
    assert nk % min(ATT_BLOCK, nk) == 0 and t >= MAX_DISTANCE
    qsub = 2 if nk <= ATT_BLOCK and nk % 2 == 0 else 1
    kern = functools.partial(_attn_kernel, nk=nk, qsub=qsub, lambda_init=lambda_init)
    const = lambda shape: pl.BlockSpec(shape, lambda bi, h, qi: (0,) * len(shape))
    tq = qsub * t
    return pl.pallas_call(
        kern,
        grid=(b, N_HEADS_A, s // tq),
        in_specs=[pl.BlockSpec(memory_space=pltpu.SMEM),
                  pl.BlockSpec((1, 1, HEAD_W_A, tq), lambda bi, h, qi: (bi, h, 0, qi)),
                  pl.BlockSpec((1, s, HEAD_W_A), lambda bi, h, qi: (bi, 0, h)),
                  pl.BlockSpec((1, 1, HEAD_W_A, s), lambda bi, h, qi: (bi, h, 0, 0)),
                  pl.BlockSpec((1, NEAR_TILES, t, t), lambda bi, h, qi: (h, 0, 0, 0)),
                  const((8, HEAD_W_A)), const((1, HEAD_W_A))],
        out_specs=pl.BlockSpec((1, tq, HEAD_W_A), lambda bi, h, qi: (bi, qi, h)),
        out_shape=jax.ShapeDtypeStruct((b, s, D_MODEL), BF16),
        scratch_shapes=[pltpu.VMEM((qsub, HEAD_W_A, 2 * t), BF16),
                        pltpu.VMEM((qsub, 1, 2 * t), F32),
                        pltpu.VMEM((qsub, 1, 2 * t), F32),
                        pltpu.VMEM((qsub, HEAD_W_A, 2 * t), F32),
                        pltpu.VMEM((qsub, 8, 2 * t), F32)],
        compiler_params=_cparams(("parallel", "parallel", "parallel")),
        name="attn",
    )(range_ok, qt, kn, vt, bias_t, lamv, subg)


def _log_gate(lr, wg_ref, bg_ref):
    x = jnp.dot(lr.astype(BF16), wg_ref[...], preferred_element_type=F32) + bg_ref[...]
    return (jnp.minimum(x, 0.0) - jnp.log1p(jnp.exp(-jnp.abs(x)))) * (LOG2E / GATE_NORM)


def _gla_bwd_kernel(q_ref, k_ref, v_ref, lr_ref, wg_ref, bg_ref, o_ref, s_scr):
    c = GLA_CHUNK

    @pl.when(pl.program_id(1) == 0)
    def _():
        s_scr[...] = jnp.zeros(s_scr.shape, F32)

    g = _log_gate(lr_ref[0], wg_ref, bg_ref)
    row = lax.broadcasted_iota(jnp.int32, g.shape, 0)
    suf = g
    sh = 1
    while sh < c:
        suf = suf + jnp.where(row < c - sh, pltpu.roll(suf, c - sh, 0), 0.0)
        sh *= 2
    tot = suf[0:1, :]
    for h in range(N_HEADS_B):
        sl = slice(h * KEY_DIM_B, (h + 1) * KEY_DIM_B)
        vs = slice(h * V_DIM_B, (h + 1) * V_DIM_B)
        q = q_ref[0][:, sl].astype(F32) * KEY_DIM_B ** -0.5
        k = k_ref[0][:, sl].astype(F32)
        qd = (q * jnp.exp2(suf[:, sl])).astype(BF16)
        kd = k * jnp.exp2(tot[:, sl] - suf[:, sl])
        st = s_scr[h]
        o_ref[0, :, vs] = jnp.dot(qd, st.astype(BF16), preferred_element_type=F32)
        dec = jnp.broadcast_to(jnp.exp2(tot[:, sl]), (c, KEY_DIM_B)).T[:, 0:1]
        s_scr[h] = st * dec + jnp.dot(kd.T.astype(BF16), v_ref[0][:, vs], preferred_element_type=F32)


def _gla_fwd_kernel(q_ref, k_ref, v_ref, og_ref, lr_ref, obi_ref, wgf_ref, bgf_ref, wgb_ref, bgb_ref,
                    gn_ref, o_ref, s_scr):
    c = GLA_CHUNK

    @pl.when(pl.program_id(1) == 0)
    def _():
        s_scr[...] = jnp.zeros(s_scr.shape, F32)

    lr = lr_ref[0]
    gf_all = _log_gate(lr, wgf_ref, bgf_ref)
    gb_all = _log_gate(lr, wgb_ref, bgb_ref)
    ri = lax.broadcasted_iota(jnp.int32, (c, c), 0)
    ci = lax.broadcasted_iota(jnp.int32, (c, c), 1)
    xr = jnp.bitwise_xor(ri, ci)
    nt = (((1,), (1,)), ((), ()))
    for h in range(N_HEADS_B):
        sl = slice(h * KEY_DIM_B, (h + 1) * KEY_DIM_B)
        vs = slice(h * V_DIM_B, (h + 1) * V_DIM_B)
        q = q_ref[0][:, sl].astype(F32) * KEY_DIM_B ** -0.5
        k = k_ref[0][:, sl].astype(F32)
        v = v_ref[0][:, vs]
        pre = gf_all[:, sl]
        tf = pre
        suf = gb_all[:, sl]
        tb = suf
        a = jnp.where(ri == ci, 2.0 * lax.dot_general(q.astype(BF16), k.astype(BF16), nt,
                                                      preferred_element_type=F32), 0.0)
        blk = 1
        def level_scores(a, blk, q_arg, k_arg):
            r = lax.dot_general((q * jnp.exp2(q_arg)).astype(BF16), (k * jnp.exp2(k_arg)).astype(BF16),
                                nt, preferred_element_type=F32)
            return jnp.where(jnp.logical_and(xr >= blk, xr < 2 * blk), r, a)

        while blk < SUBLANES:
            hi = jnp.bitwise_and(ri, blk) != 0
            a = level_scores(a, blk, jnp.where(hi, pre, suf), jnp.where(hi, tb - suf, tf - pre))
            tf_dn, tf_up = pltpu.roll(tf, blk, 0), pltpu.roll(tf, c - blk, 0)
            tb_dn, tb_up = pltpu.roll(tb, blk, 0), pltpu.roll(tb, c - blk, 0)
            pre = pre + jnp.where(hi, tf_dn, 0.0)
            suf = suf + jnp.where(hi, 0.0, tb_up)
            tf = tf + jnp.where(hi, tf_dn, tf_up)
            tb = tb + jnp.where(hi, tb_dn, tb_up)
            blk *= 2
        ns = c // SUBLANES
        slabs = lambda x: [x[SUBLANES * r:SUBLANES * (r + 1)] for r in range(ns)]
        pre, suf, tf, tb = slabs(pre), slabs(suf), slabs(tf), slabs(tb)
        while blk < c:
            g = blk // SUBLANES
            up = [bool(r & g) for r in range(ns)]
            q_arg = [pre[r] if up[r] else suf[r] for r in range(ns)]
            k_arg = [tb[r] - suf[r] if up[r] else tf[r] - pre[r] for r in range(ns)]
            a = level_scores(a, blk, jnp.concatenate(q_arg, axis=0), jnp.concatenate(k_arg, axis=0))
            pre = [pre[r] + tf[r - g] if up[r] else pre[r] for r in range(ns)]
            suf = [suf[r] if up[r] else suf[r] + tb[r + g] for r in range(ns)]
            tf_pair = {r: tf[r] + tf[r + g] for r in range(ns) if not up[r]}
            tb_pair = {r: tb[r] + tb[r + g] for r in range(ns) if not up[r]}
            tf = [tf_pair[r - g] if up[r] else tf_pair[r] for r in range(ns)]
            tb = [tb_pair[r - g] if up[r] else tb_pair[r] for r in range(ns)]
            blk *= 2
        pre, tf = jnp.concatenate(pre, axis=0), jnp.concatenate(tf, axis=0)
        st = s_scr[h]
        o = (jnp.dot(a.astype(BF16), v, preferred_element_type=F32)
             + jnp.dot((q * jnp.exp2(pre)).astype(BF16), st.astype(BF16), preferred_element_type=F32)
             + obi_ref[0, :, vs])
        kd = k * jnp.exp2(tf - pre)
        s_scr[h] = st * jnp.exp2(tf).T[:, 0:1] + jnp.dot(kd.T.astype(BF16), v, preferred_element_type=F32)
        ms = jnp.mean(o * o, axis=-1, keepdims=True)
        og = og_ref[0][:, vs].astype(F32)
        o_ref[0, :, vs] = (o * lax.rsqrt(ms + EPS) * gn_ref[...] * (og / (1.0 + jnp.exp(-og)))
                           ).astype(o_ref.dtype)


def _gla(main3, lr3, wgf, bgf, wgb, bgb, gn):
    b, s, _ = main3.shape
    c = GLA_CHUNK
    nc = s // c
    hk = N_HEADS_B * KEY_DIM_B
    state = pltpu.VMEM((N_HEADS_B, KEY_DIM_B, V_DIM_B), F32)

    def specs(cidx):
        return [pl.BlockSpec((1, c, hk), lambda bi, i: (bi, cidx(i), 0)),
                pl.BlockSpec((1, c, hk), lambda bi, i: (bi, cidx(i), 1)),
                pl.BlockSpec((1, c, D_MODEL), lambda bi, i: (bi, cidx(i), 1))]

    full = lambda shape: pl.BlockSpec(shape, lambda bi, i: (0,) * len(shape))
    rev = lambda i: nc - 1 - i
    obi = pl.pallas_call(
        _gla_bwd_kernel,
        grid=(b, nc),
        in_specs=specs(rev) + [pl.BlockSpec((1, c, LR_W), lambda bi, i: (bi, rev(i), 0)),
                               full((LR_W, hk)), full((1, hk))],
        out_specs=pl.BlockSpec((1, c, D_MODEL), lambda bi, i: (bi, rev(i), 0)),
        out_shape=jax.ShapeDtypeStruct((b, s, D_MODEL), F32),
        scratch_shapes=[state],
        compiler_params=_cparams(("parallel", "arbitrary")),
        name="gla_bwd",
    )(main3, main3, main3, lr3, wgb, bgb)

    fwd = lambda i: i
    return pl.pallas_call(
        _gla_fwd_kernel,
        grid=(b, nc),
        in_specs=specs(fwd) + [pl.BlockSpec((1, c, D_MODEL), lambda bi, i: (bi, i, 2)),
                               pl.BlockSpec((1, c, LR_W), lambda bi, i: (bi, i, 0)),
                               pl.BlockSpec((1, c, D_MODEL), lambda bi, i: (bi, i, 0)),
                               full((LR_W, hk)), full((1, hk)), full((LR_W, hk)), full((1, hk)),
                               full((1, V_DIM_B))],
        out_specs=pl.BlockSpec((1, c, D_MODEL), lambda bi, i: (bi, i, 0)),
        out_shape=jax.ShapeDtypeStruct((b, s, D_MODEL), BF16),
        scratch_shapes=[state],
        compiler_params=_cparams(("parallel", "arbitrary")),
        name="gla_fwd",
    )(main3, main3, main3, main3, lr3, obi, wgf, bgf, wgb, bgb, gn)


def _mix_kernel(x_ref, oa_ref, ob_ref, ga_ref, gb_ref, mod_ref, wa_ref, wb_ref, wo_ref, o_ref):
    ya = jnp.dot(oa_ref[...], wa_ref[...], preferred_element_type=F32)
    yb = jnp.dot(ob_ref[...], wb_ref[...], preferred_element_type=F32)
    sig = lambda r: 1.0 / (1.0 + jnp.exp(-r[...].astype(F32)))
    merged = sig(ga_ref) * ya + sig(gb_ref) * yb
    o_ref[...] = x_ref[...] + mod_ref[0, 2:3, :] * jnp.dot(merged.astype(BF16), wo_ref[...],
                                                           preferred_element_type=F32)


def _mix(x2, oa2, ob2, main2, mod8, wa, wb, wo, seq):
    t = x2.shape[0]
    tm = ROW_TILE
    row = lambda j: pl.BlockSpec((tm, D_MODEL), lambda i: (i, j))
    wspec = pl.BlockSpec((D_MODEL, D_MODEL), lambda i: (0, 0))
    return pl.pallas_call(
        _mix_kernel,
        grid=(t // tm,),
        in_specs=[row(0), row(0), row(0), row(3), row(4),
                  pl.BlockSpec((1, 8, D_MODEL), lambda i: ((i * tm) // seq, 0, 0)),
                  wspec, wspec, wspec],
        out_specs=row(0),
        out_shape=jax.ShapeDtypeStruct((t, D_MODEL), F32),
        compiler_params=_cparams(("parallel",)),
        name="mix",
    )(x2, oa2, ob2, main2, main2, mod8, wa, wb, wo)


def _mlp_kernel(x_ref, mod_ref, g_ref, wu_ref, wd_ref, o_ref):
    x = x_ref[...]
    h = _modulated_norm(x, g_ref[...], mod_ref[0, 3:4, :], mod_ref[0, 4:5, :]).astype(BF16)
    acc = jnp.zeros(x.shape, F32)
    for j in range(D_FF // D_MODEL):
        cs = slice(j * D_MODEL, (j + 1) * D_MODEL)
        u = jnp.maximum(jnp.dot(h, wu_ref[:, cs], preferred_element_type=F32), 0.0)
        acc = acc + jnp.dot((u * u).astype(BF16), wd_ref[cs, :], preferred_element_type=F32)
    o_ref[...] = x + mod_ref[0, 5:6, :] * acc


def _mlp(x2, mod8, g2, wu, wd, seq):
    t = x2.shape[0]
    tm = ROW_TILE
    once = pl.Buffered(1)
    return pl.pallas_call(
        _mlp_kernel,
        grid=(t // tm,),
        in_specs=[pl.BlockSpec((tm, D_MODEL), lambda i: (i, 0)),
                  pl.BlockSpec((1, 8, D_MODEL), lambda i: ((i * tm) // seq, 0, 0)),
                  pl.BlockSpec((1, D_MODEL), lambda i: (0, 0)),
                  pl.BlockSpec((D_MODEL, D_FF), lambda i: (0, 0), pipeline_mode=once),
                  pl.BlockSpec((D_FF, D_MODEL), lambda i: (0, 0), pipeline_mode=once)],
        out_specs=pl.BlockSpec((tm, D_MODEL), lambda i: (i, 0)),
        out_shape=jax.ShapeDtypeStruct((t, D_MODEL), F32),
        compiler_params=_cparams(("parallel",)),
        name="mlp",
    )(x2, mod8, g2, wu, wd)


def _split_w_in(w_in):
    o = np.cumsum([0, 1024, 1024, 1024, 512, 512, 1024, 1024, GATE_RANK, GATE_RANK, 1024, 1024])
    main = jnp.concatenate([w_in[:, o[0]:o[7]], w_in[:, o[9]:o[11]]], axis=1)
    lr = jnp.pad(w_in[:, o[7]:o[9]], ((0, 0), (0, LR_W - 2 * GATE_RANK)))
    return main.astype(BF16), lr.astype(BF16)


def _pad_gate_w(w_gate, row0):
    return jnp.pad(w_gate, ((row0, LR_W - GATE_RANK - row0), (0, 0))).astype(BF16)


def _layer(x, mod8, layer_idx, p):
    b, s, _ = x.shape
    t = b * s
    lambda_init = 0.8 - 0.6 * math.exp(-0.3 * layer_idx)
    x2 = x.reshape(t, D_MODEL)
    row = lambda v: v.reshape(1, -1).astype(F32)

    qg2, kg2 = row(jnp.tile(p["q_norm_g"], 2)), row(jnp.tile(p["k_norm_g"], 2))
    qt, kn2, vt, main2, lr2 = _proj(x2, mod8, row(p["norm1_g"]), p["w_main"], p["w_lr"], qg2, kg2, s)
    kn = kn2.reshape(b, s, D_MODEL)
    main3 = main2.reshape(b, s, N_REST)
    lr3 = lr2.reshape(b, s, LR_W)
    lamv = jnp.zeros((8, HEAD_W_A), F32).at[0:4, :HEAD_DIM_A].set(
        jnp.stack([p["lam_q1"], p["lam_k1"], p["lam_q2"], p["lam_k2"]]).astype(F32))
    bias_t, range_ok = _bias_tiles(p["rel_bias"], qg2, kg2)
    oa = _attn(range_ok, qt, kn, vt, bias_t, lamv, row(p["subln_g"]), lambda_init)

    ob = _gla(main3, lr3, p["wgf"], row(p["b_gate_f"]), p["wgb"], row(p["b_gate_b"]), row(p["gla_norm_g"]))

    x1 = _mix(x2, oa.reshape(t, D_MODEL), ob.reshape(t, D_MODEL), main2, mod8,
              p["w_branch_a"], p["w_branch_b"], p["w_out"], s)
    y = _mlp(x1, mod8, row(p["norm2_g"]), p["w_up"], p["w_down"], s)
    return y.reshape(b, s, D_MODEL)


def kernel(x_prompt, x_sample, c_prompt, c_sample, rel_bias, w_ada, b_ada, norm1_g, w_in, q_norm_g, k_norm_g, lam_q1, lam_k1, lam_q2, lam_k2, subln_g, w_gate_f, b_gate_f, w_gate_b, b_gate_b, gla_norm_g, w_branch_a, w_branch_b, w_out, norm2_g, w_up, w_down):
    depth = w_in.shape[0]
    nb_p, nb_s = c_prompt.shape[0], c_sample.shape[0]
    rows = -(-(nb_p + nb_s) // 8) * 8
    c_all = jnp.pad(jnp.concatenate([c_prompt, c_sample], axis=0), ((0, rows - nb_p - nb_s), (0, 0)))

    xp, xs = x_prompt, x_sample
    for l in range(depth):
        mod = _ada(c_all, w_ada[l], b_ada[l]).reshape(rows, 6, D_MODEL)
        mod8 = jnp.pad(mod, ((0, 0), (0, 2), (0, 0)))
        w_main, w_lr = _split_w_in(w_in[l])
        p = dict(rel_bias=rel_bias.astype(F32), norm1_g=norm1_g[l], w_main=w_main, w_lr=w_lr,
                 q_norm_g=q_norm_g[l], k_norm_g=k_norm_g[l], lam_q1=lam_q1[l], lam_k1=lam_k1[l],
                 lam_q2=lam_q2[l], lam_k2=lam_k2[l], subln_g=subln_g[l],
                 wgf=_pad_gate_w(w_gate_f[l], 0), b_gate_f=b_gate_f[l],
                 wgb=_pad_gate_w(w_gate_b[l], GATE_RANK), b_gate_b=b_gate_b[l],
                 gla_norm_g=gla_norm_g[l], w_branch_a=w_branch_a[l].astype(BF16),
                 w_branch_b=w_branch_b[l].astype(BF16), w_out=w_out[l].astype(BF16),
                 norm2_g=norm2_g[l], w_up=w_up[l].astype(BF16), w_down=w_down[l].astype(BF16))
        xp = _layer(xp, mod8[:nb_p], l, p)
        xs = _layer(xs, mod8[nb_p:nb_p + nb_s], l, p)
    return (xp, xs)
```

```python
import functools
import math

import jax
import jax.numpy as jnp
import numpy as np
from jax import lax
from jax.experimental import pallas as pl
from jax.experimental.pallas import tpu as pltpu

F32 = jnp.float32
BF16 = jnp.bfloat16

D_MODEL = 1024
HEAD_DIM_A = 64
N_HEADS_A = 8
HEAD_W_A = 2 * HEAD_DIM_A
N_HEADS_B = 4
KEY_DIM_B = 128
V_DIM_B = 256
GATE_RANK = 16
GATE_NORM = 16.0
N_BUCKETS = 32
MAX_DISTANCE = 128
D_FF = 4 * D_MODEL
EPS = 1e-6
LOG2E = math.log2(math.e)

N_MAIN = 8 * D_MODEL
N_REST = 5 * D_MODEL
LR_W = 128

VMEM_LIMIT = 56 * 1024 * 1024
ROW_TILE = 512
ATT_TILE = 256
ATT_BLOCK = 16
NEAR_TILES = 5
EXP2_RANGE = 100.0
GLA_CHUNK = 128
SUBLANES = 8


def _cparams(sem):
    return pltpu.CompilerParams(dimension_semantics=sem, vmem_limit_bytes=VMEM_LIMIT)


def _ada_kernel(c_ref, w_ref, b_ref, o_ref):
    c = c_ref[...]
    a = c / (1.0 + jnp.exp(-c))
    o_ref[...] = jnp.dot(a, w_ref[...], preferred_element_type=F32,
                         precision=lax.Precision.HIGHEST) + b_ref[...]


def _ada(c_all, w_ada, b_ada):
    r = c_all.shape[0]
    tn = 1024
    return pl.pallas_call(
        _ada_kernel,
        grid=(6 * D_MODEL // tn,),
        in_specs=[pl.BlockSpec((r, D_MODEL), lambda n: (0, 0)),
                  pl.BlockSpec((D_MODEL, tn), lambda n: (0, n)),
                  pl.BlockSpec((1, tn), lambda n: (0, n))],
        out_specs=pl.BlockSpec((r, tn), lambda n: (0, n)),
        out_shape=jax.ShapeDtypeStruct((r, 6 * D_MODEL), F32),
        compiler_params=_cparams(("parallel",)),
        name="ada",
    )(c_all, w_ada, b_ada.reshape(1, -1))


def _modulated_norm(x, g, shift, scale):
    ms = jnp.mean(x * x, axis=-1, keepdims=True)
    return x * lax.rsqrt(ms + EPS) * g * (1.0 + scale) + shift


def _pair_rms(x, g):
    lane = lax.broadcasted_iota(jnp.int32, x.shape, 1)
    x2 = x * x
    s_all = jnp.sum(x2, axis=-1, keepdims=True)
    s_lo = jnp.sum(jnp.where(lane < HEAD_DIM_A, x2, 0.0), axis=-1, keepdims=True)
    ss = jnp.where(lane < HEAD_DIM_A, s_lo, s_all - s_lo)
    return x * lax.rsqrt(ss * (1.0 / HEAD_DIM_A) + EPS) * g


def _proj_kernel(x_ref, mod_ref, g_ref, w_ref, wlr_ref, qg_ref, kg_ref, wgf_ref, bgf_ref, wgb_ref, bgb_ref,
                 qt_ref, kn_ref, vt_ref, o_ref, lr_ref, of_ref, s_scr, *, spt):
    @pl.when(pl.program_id(0) % spt == 0)
    def _():
        s_scr[...] = jnp.zeros(s_scr.shape, F32)

    h = _modulated_norm(x_ref[...], g_ref[...], mod_ref[0, 0:1, :], mod_ref[0, 1:2, :]).astype(BF16)
    lr = jnp.dot(h, wlr_ref[...], preferred_element_type=F32)
    lr_ref[...] = lr
    piece = lambda j: jnp.dot(h, w_ref[:, j * D_MODEL:(j + 1) * D_MODEL], preferred_element_type=F32)
    heads = [slice(i * HEAD_W_A, (i + 1) * HEAD_W_A) for i in range(N_HEADS_A)]

    qa = piece(0)
    for i, sl in enumerate(heads):
        qn = _pair_rms(qa[:, sl], qg_ref[...]) * (HEAD_DIM_A ** -0.5 * LOG2E)
        qt_ref[0, i] = qn.T.astype(BF16)
    ka = piece(1)
    for sl in heads:
        kn_ref[:, sl] = _pair_rms(ka[:, sl], kg_ref[...]).astype(BF16)
    va = piece(2)
    for i, sl in enumerate(heads):
        vt_ref[0, i] = va[:, sl].T.astype(BF16)
    qk = piece(3)
    vg = piece(4).astype(BF16)
    o_ref[:, 0:D_MODEL] = qk.astype(o_ref.dtype)
    o_ref[:, D_MODEL:2 * D_MODEL] = vg
    for j in range(5, N_MAIN // D_MODEL):
        o_ref[:, (j - 3) * D_MODEL:(j - 2) * D_MODEL] = piece(j).astype(o_ref.dtype)

    hk = N_HEADS_B * KEY_DIM_B
    for c in range(x_ref.shape[0] // GLA_CHUNK):
        rows = slice(c * GLA_CHUNK, (c + 1) * GLA_CHUNK)
        gf = _log_gate(lr[rows], wgf_ref, bgf_ref)
        gb = _log_gate(lr[rows], wgb_ref, bgb_ref)
        for i in range(N_HEADS_B):
            sl = slice(i * KEY_DIM_B, (i + 1) * KEY_DIM_B)
            vs = slice(i * V_DIM_B, (i + 1) * V_DIM_B)
            of_ref[rows, vs] = _gla_fwd_chunk(qk[rows, sl] * KEY_DIM_B ** -0.5,
                                              qk[rows, hk + i * KEY_DIM_B:hk + (i + 1) * KEY_DIM_B],
                                              vg[rows, vs], gf[:, sl], gb[:, sl], s_scr, i)


def _proj(x2, mod8, g1, w_main, w_lr, qg2, kg2, wgf, bgf, wgb, bgb, seq):
    t = x2.shape[0]
    tm = ROW_TILE
    spt = seq // tm
    hk = N_HEADS_B * KEY_DIM_B
    once = pl.Buffered(1)
    const = lambda shape, **kw: pl.BlockSpec(shape, lambda i: (0,) * len(shape), **kw)
    tspec = pl.BlockSpec((1, N_HEADS_A, HEAD_W_A, tm), lambda i: (i // spt, 0, 0, i % spt))
    tshape = jax.ShapeDtypeStruct((t // seq, N_HEADS_A, HEAD_W_A, seq), BF16)
    rowspec = lambda w: pl.BlockSpec((tm, w), lambda i: (i, 0))
    return pl.pallas_call(
        functools.partial(_proj_kernel, spt=spt),
        grid=(t // tm,),
        in_specs=[rowspec(D_MODEL),
                  pl.BlockSpec((1, 8, D_MODEL), lambda i: (i // spt, 0, 0)),
                  const((1, D_MODEL)),
                  const((D_MODEL, N_MAIN), pipeline_mode=once),
                  const((D_MODEL, LR_W), pipeline_mode=once),
                  const((1, HEAD_W_A)), const((1, HEAD_W_A)),
                  const((LR_W, hk)), const((1, hk)), const((LR_W, hk)), const((1, hk))],
        out_specs=[tspec, rowspec(D_MODEL), tspec, rowspec(N_REST), rowspec(LR_W), rowspec(D_MODEL)],
        out_shape=[tshape,
                   jax.ShapeDtypeStruct((t, D_MODEL), BF16),
                   tshape,
                   jax.ShapeDtypeStruct((t, N_REST), BF16),
                   jax.ShapeDtypeStruct((t, LR_W), F32),
                   jax.ShapeDtypeStruct((t, D_MODEL), F32)],
        scratch_shapes=[pltpu.VMEM((N_HEADS_B, KEY_DIM_B, V_DIM_B), F32)],
        compiler_params=_cparams(("arbitrary",)),
        name="proj",
    )(x2, mod8, g1, w_main, w_lr, qg2, kg2, wgf, bgf, wgb, bgb)


def _t5_bucket(rel):
    nb = N_BUCKETS // 2
    max_exact = nb // 2
    ret = (rel > 0).astype(jnp.int32) * nb
    n = jnp.abs(rel)
    nf = jnp.maximum(n, 1).astype(F32)
    large = max_exact + (jnp.log(nf / max_exact) / math.log(MAX_DISTANCE / max_exact)
                         * (nb - max_exact)).astype(jnp.int32)
    large = jnp.minimum(large, nb - 1)
    return ret + jnp.where(n < max_exact, n, large)


def _bias_kernel(tbl_ref, bkt_ref, qg_ref, kg_ref, o_ref, ok_ref):
    h = pl.program_id(0)
    bkt = bkt_ref[...]
    acc = jnp.zeros(bkt.shape, F32)
    babs = jnp.float32(0.0)
    for i in range(N_BUCKETS):
        acc = jnp.where(bkt == i, tbl_ref[i, h], acc)
        babs = jnp.maximum(babs, jnp.abs(tbl_ref[i, h]))
    o_ref[0] = acc * LOG2E
    gmax = lambda r: jnp.max(jnp.abs(r[...]), axis=1, keepdims=True)
    bound = gmax(qg_ref) * gmax(kg_ref) * (1.02 * HEAD_DIM_A * HEAD_DIM_A ** -0.5 * LOG2E) + babs * LOG2E
    ok_ref[0] = jnp.broadcast_to((bound <= EXP2_RANGE).astype(jnp.int32), ok_ref.shape[1:])


def _bias_tiles(rel_bias, qg2, kg2):
    t = ATT_TILE
    kk = jnp.arange(t, dtype=jnp.int32)[:, None]
    qq = jnp.arange(t, dtype=jnp.int32)[None, :]
    rel = jnp.stack([(d - NEAR_TILES // 2) * t + kk - qq for d in range(NEAR_TILES)])
    gspec = pl.BlockSpec((1, HEAD_W_A), lambda h: (0, 0))
    tiles, ok = pl.pallas_call(
        _bias_kernel,
        grid=(N_HEADS_A,),
        in_specs=[pl.BlockSpec(memory_space=pltpu.SMEM),
                  pl.BlockSpec((NEAR_TILES, t, t), lambda h: (0, 0, 0)), gspec, gspec],
        out_specs=[pl.BlockSpec((1, NEAR_TILES, t, t), lambda h: (h, 0, 0, 0)),
                   pl.BlockSpec((1, 8, HEAD_W_A), lambda h: (h, 0, 0))],
        out_shape=[jax.ShapeDtypeStruct((N_HEADS_A, NEAR_TILES, t, t), F32),
                   jax.ShapeDtypeStruct((N_HEADS_A, 8, HEAD_W_A), jnp.int32)],
        compiler_params=_cparams(("parallel",)),
        name="bias",
    )(rel_bias, _t5_bucket(rel), qg2, kg2)
    return tiles, ok[:, 0, 0]


def _attn_kernel(ok_ref, qt_ref, k_ref, vt_ref, bias_ref, lamv_ref, subg_ref, o_ref,
                 w_scr, m_scr, l_scr, acc_scr, l8_scr, *, nk, qsub, lambda_init):
    t = ATT_TILE
    blk = min(ATT_BLOCK, nk)
    bt = blk * t
    h = pl.program_id(1)
    tiles = [(u, pl.program_id(2) * qsub + u) for u in range(qsub)]

    z = jnp.zeros((HEAD_DIM_A, t), BF16)
    for u, _ in tiles:
        q = qt_ref[0, 0, :, u * t:(u + 1) * t]
        w_scr[u, :, :t] = jnp.concatenate([q[:HEAD_DIM_A], z], axis=0)
        w_scr[u, :, t:] = jnp.concatenate([z, q[HEAD_DIM_A:]], axis=0)

    def chunk_bias(kj, qi):
        b = bias_ref[0, jnp.clip(kj - qi, -2, 2) + 2]
        return jnp.concatenate([b, b], axis=1)

    @pl.when(ok_ref[h] != 0)
    def _():
        l8_scr[...] = jnp.zeros(l8_scr.shape, F32)
        acc_scr[...] = jnp.zeros(acc_scr.shape, F32)

        def block(j, carry):
            k0 = pl.multiple_of(j * bt, bt)
            for u, qi in tiles:
                s = jnp.dot(k_ref[0, pl.ds(k0, bt), :], w_scr[u], preferred_element_type=F32)
                p = [jnp.exp2(s[c * t:(c + 1) * t] + chunk_bias(j * blk + c, qi)) for c in range(blk)]
                p = jnp.concatenate(p, axis=0)
                l8_scr[u] += jnp.sum(p.reshape(bt // 8, 8, 2 * t), axis=0)
                acc_scr[u] += jnp.dot(vt_ref[0, 0, :, pl.ds(k0, bt)], p.astype(BF16),
                                      preferred_element_type=F32)
            return carry

        lax.fori_loop(0, nk // blk, block, 0)
        l_scr[...] = jnp.sum(l8_scr[...], axis=1, keepdims=True)

    @pl.when(ok_ref[h] == 0)
    def _():
        m_scr[...] = jnp.full(m_scr.shape, -jnp.inf, F32)
        l_scr[...] = jnp.zeros(l_scr.shape, F32)
        acc_scr[...] = jnp.zeros(acc_scr.shape, F32)

        def online_step(kj, carry):
            k0 = pl.multiple_of(kj * t, t)
            for u, qi in tiles:
                s = jnp.dot(k_ref[0, pl.ds(k0, t), :], w_scr[u], preferred_element_type=F32)
                s = s + chunk_bias(kj, qi)
                m_old = m_scr[u]
                m_new = jnp.maximum(m_old, jnp.max(s, axis=0, keepdims=True))
                alpha = jnp.exp2(m_old - m_new)
                p = jnp.exp2(s - m_new)
                l_scr[u] = alpha * l_scr[u] + jnp.sum(p, axis=0, keepdims=True)
                pv = jnp.dot(vt_ref[0, 0, :, pl.ds(k0, t)], p.astype(BF16), preferred_element_type=F32)
                acc_scr[u] = alpha * acc_scr[u] + pv
                m_scr[u] = m_new
            return carry

        lax.fori_loop(0, nk, online_step, 0)

    lv = lamv_ref[...]
    lam = (jnp.exp(jnp.sum(lv[0:1] * lv[1:2], axis=-1, keepdims=True))
           - jnp.exp(jnp.sum(lv[2:3] * lv[3:4], axis=-1, keepdims=True)) + lambda_init)
    for u, _ in tiles:
        acc = acc_scr[u]
        inv = 1.0 / l_scr[u]
        o = acc[:, :t] * inv[:, :t] - lam * (acc[:, t:] * inv[:, t:])
        ms = jnp.mean(o * o, axis=0, keepdims=True)
        y = (o * lax.rsqrt(ms + EPS)).T
        o_ref[0, u * t:(u + 1) * t, :] = (y * subg_ref[...] * (1.0 - lambda_init)).astype(o_ref.dtype)


def _attn(range_ok, qt, kn, vt, bias_t, lamv, subg, lambda_init):
    b, s, _ = kn.shape
    t = ATT_TILE
    nk = s // t
    assert nk % min(ATT_BLOCK, nk) == 0 and t >= MAX_DISTANCE
    qsub = 2 if nk <= ATT_BLOCK and nk % 2 == 0 else 1
    kern = functools.partial(_attn_kernel, nk=nk, qsub=qsub, lambda_init=lambda_init)
    const = lambda shape: pl.BlockSpec(shape, lambda bi, h, qi: (0,) * len(shape))
    tq = qsub * t
    return pl.pallas_call(
        kern,
        grid=(b, N_HEADS_A, s // tq),
        in_specs=[pl.BlockSpec(memory_space=pltpu.SMEM),
                  pl.BlockSpec((1, 1, HEAD_W_A, tq), lambda bi, h, qi: (bi, h, 0, qi)),
                  pl.BlockSpec((1, s, HEAD_W_A), lambda bi, h, qi: (bi, 0, h)),
                  pl.BlockSpec((1, 1, HEAD_W_A, s), lambda bi, h, qi: (bi, h, 0, 0)),
                  pl.BlockSpec((1, NEAR_TILES, t, t), lambda bi, h, qi: (h, 0, 0, 0)),
                  const((8, HEAD_W_A)), const((1, HEAD_W_A))],
        out_specs=pl.BlockSpec((1, tq, HEAD_W_A), lambda bi, h, qi: (bi, qi, h)),
        out_shape=jax.ShapeDtypeStruct((b, s, D_MODEL), BF16),
        scratch_shapes=[pltpu.VMEM((qsub, HEAD_W_A, 2 * t), BF16),
                        pltpu.VMEM((qsub, 1, 2 * t), F32),
                        pltpu.VMEM((qsub, 1, 2 * t), F32),
                        pltpu.VMEM((qsub, HEAD_W_A, 2 * t), F32),
                        pltpu.VMEM((qsub, 8, 2 * t), F32)],
        compiler_params=_cparams(("parallel", "parallel", "parallel")),
        name="attn",
    )(range_ok, qt, kn, vt, bias_t, lamv, subg)


def _log_gate(lr, wg_ref, bg_ref):
    x = jnp.dot(lr.astype(BF16), wg_ref[...], preferred_element_type=F32) + bg_ref[...]
    return (jnp.minimum(x, 0.0) - jnp.log1p(jnp.exp(-jnp.abs(x)))) * (LOG2E / GATE_NORM)


def _gla_bwd_kernel(q_ref, k_ref, v_ref, og_ref, lr_ref, of_ref, wg_ref, bg_ref, gn_ref, o_ref, s_scr):
    c = GLA_CHUNK

    @pl.when(pl.program_id(1) == 0)
    def _():
        s_scr[...] = jnp.zeros(s_scr.shape, F32)

    g = _log_gate(lr_ref[0], wg_ref, bg_ref)
    row = lax.broadcasted_iota(jnp.int32, g.shape, 0)
    suf = g
    sh = 1
    while sh < c:
        suf = suf + jnp.where(row < c - sh, pltpu.roll(suf, c - sh, 0), 0.0)
        sh *= 2
    tot = suf[0:1, :]
    for h in range(N_HEADS_B):
        sl = slice(h * KEY_DIM_B, (h + 1) * KEY_DIM_B)
        vs = slice(h * V_DIM_B, (h + 1) * V_DIM_B)
        q = q_ref[0, :, sl].astype(F32) * KEY_DIM_B ** -0.5
        k = k_ref[0, :, sl].astype(F32)
        qd = (q * jnp.exp2(suf[:, sl])).astype(BF16)
        kd = k * jnp.exp2(tot[:, sl] - suf[:, sl])
        st = s_scr[h]
        o = jnp.dot(qd, st.astype(BF16), preferred_element_type=F32) + of_ref[0, :, vs]
        dec = jnp.broadcast_to(jnp.exp2(tot[:, sl]), (c, KEY_DIM_B)).T[:, 0:1]
        s_scr[h] = st * dec + jnp.dot(kd.T.astype(BF16), v_ref[0, :, vs], preferred_element_type=F32)
        ms = jnp.mean(o * o, axis=-1, keepdims=True)
        og = og_ref[0, :, vs].astype(F32)
        o_ref[0, :, vs] = (o * lax.rsqrt(ms + EPS) * gn_ref[...] * (og / (1.0 + jnp.exp(-og)))
                           ).astype(o_ref.dtype)


def _gla_chunk_scores(q, k, gf, gb):
    c = GLA_CHUNK
    ri = lax.broadcasted_iota(jnp.int32, (c, c), 0)
    ci = lax.broadcasted_iota(jnp.int32, (c, c), 1)
    xr = jnp.bitwise_xor(ri, ci)
    nt = (((1,), (1,)), ((), ()))
    pre, tf, suf, tb = gf, gf, gb, gb
    a = jnp.where(ri == ci, 2.0 * lax.dot_general(q.astype(BF16), k.astype(BF16), nt,
                                                  preferred_element_type=F32), 0.0)

    def level_scores(a, blk, q_arg, k_arg):
        r = lax.dot_general((q * jnp.exp2(q_arg)).astype(BF16), (k * jnp.exp2(k_arg)).astype(BF16),
                            nt, preferred_element_type=F32)
        return jnp.where(jnp.logical_and(xr >= blk, xr < 2 * blk), r, a)

    blk = 1
    while blk < SUBLANES:
        hi = jnp.bitwise_and(ri, blk) != 0
        a = level_scores(a, blk, jnp.where(hi, pre, suf), jnp.where(hi, tb - suf, tf - pre))
        tf_dn, tf_up = pltpu.roll(tf, blk, 0), pltpu.roll(tf, c - blk, 0)
        tb_dn, tb_up = pltpu.roll(tb, blk, 0), pltpu.roll(tb, c - blk, 0)
        pre = pre + jnp.where(hi, tf_dn, 0.0)
        suf = suf + jnp.where(hi, 0.0, tb_up)
        tf = tf + jnp.where(hi, tf_dn, tf_up)
        tb = tb + jnp.where(hi, tb_dn, tb_up)
        blk *= 2
    ns = c // SUBLANES
    slabs = lambda x: [x[SUBLANES * r:SUBLANES * (r + 1)] for r in range(ns)]
    pre, suf, tf, tb = slabs(pre), slabs(suf), slabs(tf), slabs(tb)
    while blk < c:
        g = blk // SUBLANES
        up = [bool(r & g) for r in range(ns)]
        q_arg = [pre[r] if up[r] else suf[r] for r in range(ns)]
        k_arg = [tb[r] - suf[r] if up[r] else tf[r] - pre[r] for r in range(ns)]
        a = level_scores(a, blk, jnp.concatenate(q_arg, axis=0), jnp.concatenate(k_arg, axis=0))
        pre = [pre[r] + tf[r - g] if up[r] else pre[r] for r in range(ns)]
        suf = [suf[r] if up[r] else suf[r] + tb[r + g] for r in range(ns)]
        tf_pair = {r: tf[r] + tf[r + g] for r in range(ns) if not up[r]}
        tb_pair = {r: tb[r] + tb[r + g] for r in range(ns) if not up[r]}
        tf = [tf_pair[r - g] if up[r] else tf_pair[r] for r in range(ns)]
        tb = [tb_pair[r - g] if up[r] else tb_pair[r] for r in range(ns)]
        blk *= 2
    return a, jnp.concatenate(pre, axis=0), jnp.concatenate(tf, axis=0)


def _gla_fwd_chunk(q, k, v, gf, gb, s_scr, h):
    a, pre, tf = _gla_chunk_scores(q, k, gf, gb)
    st = s_scr[h]
    o = (jnp.dot(a.astype(BF16), v, preferred_element_type=F32)
         + jnp.dot((q * jnp.exp2(pre)).astype(BF16), st.astype(BF16), preferred_element_type=F32))
    kd = k * jnp.exp2(tf - pre)
    s_scr[h] = st * jnp.exp2(tf).T[:, 0:1] + jnp.dot(kd.T.astype(BF16), v, preferred_element_type=F32)
    return o


def _gla_bwd(main3, lr3, of3, wgb, bgb, gn):
    b, s, _ = main3.shape
    c = GLA_CHUNK
    nc = s // c
    hk = N_HEADS_B * KEY_DIM_B
    rev = lambda i: nc - 1 - i
    blk = lambda w, j: pl.BlockSpec((1, c, w), lambda bi, i: (bi, rev(i), j))
    full = lambda shape: pl.BlockSpec(shape, lambda bi, i: (0,) * len(shape))
    return pl.pallas_call(
        _gla_bwd_kernel,
        grid=(b, nc),
        in_specs=[blk(hk, 0), blk(hk, 1), blk(D_MODEL, 1), blk(D_MODEL, 2), blk(LR_W, 0), blk(D_MODEL, 0),
                  full((LR_W, hk)), full((1, hk)), full((1, V_DIM_B))],
        out_specs=blk(D_MODEL, 0),
        out_shape=jax.ShapeDtypeStruct((b, s, D_MODEL), BF16),
        scratch_shapes=[pltpu.VMEM((N_HEADS_B, KEY_DIM_B, V_DIM_B), F32)],
        compiler_params=_cparams(("parallel", "arbitrary")),
        name="gla_bwd",
    )(main3, main3, main3, main3, lr3, of3, wgb, bgb, gn)


def _mix_kernel(x_ref, oa_ref, ob_ref, ga_ref, gb_ref, mod_ref, wa_ref, wb_ref, wo_ref, o_ref):
    ya = jnp.dot(oa_ref[...], wa_ref[...], preferred_element_type=F32)
    yb = jnp.dot(ob_ref[...], wb_ref[...], preferred_element_type=F32)
    sig = lambda r: 1.0 / (1.0 + jnp.exp(-r[...].astype(F32)))
    merged = sig(ga_ref) * ya + sig(gb_ref) * yb
    o_ref[...] = x_ref[...] + mod_ref[0, 2:3, :] * jnp.dot(merged.astype(BF16), wo_ref[...],
                                                           preferred_element_type=F32)


def _mix(x2, oa2, ob2, main2, mod8, wa, wb, wo, seq):
    t = x2.shape[0]
    tm = ROW_TILE
    row = lambda j: pl.BlockSpec((tm, D_MODEL), lambda i: (i, j))
    wspec = pl.BlockSpec((D_MODEL, D_MODEL), lambda i: (0, 0))
    return pl.pallas_call(
        _mix_kernel,
        grid=(t // tm,),
        in_specs=[row(0), row(0), row(0), row(3), row(4),
                  pl.BlockSpec((1, 8, D_MODEL), lambda i: ((i * tm) // seq, 0, 0)),
                  wspec, wspec, wspec],
        out_specs=row(0),
        out_shape=jax.ShapeDtypeStruct((t, D_MODEL), F32),
        compiler_params=_cparams(("parallel",)),
        name="mix",
    )(x2, oa2, ob2, main2, main2, mod8, wa, wb, wo)


def _mlp_kernel(x_ref, mod_ref, g_ref, wu_ref, wd_ref, o_ref):
    x = x_ref[...]
    h = _modulated_norm(x, g_ref[...], mod_ref[0, 3:4, :], mod_ref[0, 4:5, :]).astype(BF16)
    acc = jnp.zeros(x.shape, F32)
    for j in range(D_FF // D_MODEL):
        cs = slice(j * D_MODEL, (j + 1) * D_MODEL)
        u = jnp.maximum(jnp.dot(h, wu_ref[:, cs], preferred_element_type=F32), 0.0)
        acc = acc + jnp.dot((u * u).astype(BF16), wd_ref[cs, :], preferred_element_type=F32)
    o_ref[...] = x + mod_ref[0, 5:6, :] * acc


def _mlp(x2, mod8, g2, wu, wd, seq):
    t = x2.shape[0]
    tm = ROW_TILE
    once = pl.Buffered(1)
    return pl.pallas_call(
        _mlp_kernel,
        grid=(t // tm,),
        in_specs=[pl.BlockSpec((tm, D_MODEL), lambda i: (i, 0)),
                  pl.BlockSpec((1, 8, D_MODEL), lambda i: ((i * tm) // seq, 0, 0)),
                  pl.BlockSpec((1, D_MODEL), lambda i: (0, 0)),
                  pl.BlockSpec((D_MODEL, D_FF), lambda i: (0, 0), pipeline_mode=once),
                  pl.BlockSpec((D_FF, D_MODEL), lambda i: (0, 0), pipeline_mode=once)],
        out_specs=pl.BlockSpec((tm, D_MODEL), lambda i: (i, 0)),
        out_shape=jax.ShapeDtypeStruct((t, D_MODEL), F32),
        compiler_params=_cparams(("parallel",)),
        name="mlp",
    )(x2, mod8, g2, wu, wd)


def _split_w_in(w_in):
    o = np.cumsum([0, 1024, 1024, 1024, 512, 512, 1024, 1024, GATE_RANK, GATE_RANK, 1024, 1024])
    main = jnp.concatenate([w_in[:, o[0]:o[7]], w_in[:, o[9]:o[11]]], axis=1)
    lr = jnp.pad(w_in[:, o[7]:o[9]], ((0, 0), (0, LR_W - 2 * GATE_RANK)))
    return main.astype(BF16), lr.astype(BF16)


def _pad_gate_w(w_gate, row0):
    return jnp.pad(w_gate, ((row0, LR_W - GATE_RANK - row0), (0, 0))).astype(BF16)


def _layer(x, mod8, layer_idx, p):
    b, s, _ = x.shape
    t = b * s
    lambda_init = 0.8 - 0.6 * math.exp(-0.3 * layer_idx)
    x2 = x.reshape(t, D_MODEL)
    row = lambda v: v.reshape(1, -1).astype(F32)

    qg2, kg2 = row(jnp.tile(p["q_norm_g"], 2)), row(jnp.tile(p["k_norm_g"], 2))
    qt, kn2, vt, main2, lr2, of2 = _proj(x2, mod8, row(p["norm1_g"]), p["w_main"], p["w_lr"], qg2, kg2,
                                         p["wgf"], row(p["b_gate_f"]), p["wgb"], row(p["b_gate_b"]), s)
    kn = kn2.reshape(b, s, D_MODEL)
    main3 = main2.reshape(b, s, N_REST)
    lr3 = lr2.reshape(b, s, LR_W)
    lamv = jnp.zeros((8, HEAD_W_A), F32).at[0:4, :HEAD_DIM_A].set(
        jnp.stack([p["lam_q1"], p["lam_k1"], p["lam_q2"], p["lam_k2"]]).astype(F32))
    bias_t, range_ok = _bias_tiles(p["rel_bias"], qg2, kg2)
    oa = _attn(range_ok, qt, kn, vt, bias_t, lamv, row(p["subln_g"]), lambda_init)

    ob = _gla_bwd(main3, lr3, of2.reshape(b, s, D_MODEL), p["wgb"], row(p["b_gate_b"]), row(p["gla_norm_g"]))

    x1 = _mix(x2, oa.reshape(t, D_MODEL), ob.reshape(t, D_MODEL), main2, mod8,
              p["w_branch_a"], p["w_branch_b"], p["w_out"], s)
    y = _mlp(x1, mod8, row(p["norm2_g"]), p["w_up"], p["w_down"], s)
    return y.reshape(b, s, D_MODEL)


def kernel(x_prompt, x_sample, c_prompt, c_sample, rel_bias, w_ada, b_ada, norm1_g, w_in, q_norm_g, k_norm_g, lam_q1, lam_k1, lam_q2, lam_k2, subln_g, w_gate_f, b_gate_f, w_gate_b, b_gate_b, gla_norm_g, w_branch_a, w_branch_b, w_out, norm2_g, w_up, w_down):
    depth = w_in.shape[0]
    nb_p, nb_s = c_prompt.shape[0], c_sample.shape[0]
    rows = -(-(nb_p + nb_s) // 8) * 8
    c_all = jnp.pad(jnp.concatenate([c_prompt, c_sample], axis=0), ((0, rows - nb_p - nb_s), (0, 0)))

    xp, xs = x_prompt, x_sample
    for l in range(depth):
        mod = _ada(c_all, w_ada[l], b_ada[l]).reshape(rows, 6, D_MODEL)
        mod8 = jnp.pad(mod, ((0, 0), (0, 2), (0, 0)))
        w_main, w_lr = _split_w_in(w_in[l])
        p = dict(rel_bias=rel_bias.astype(F32), norm1_g=norm1_g[l], w_main=w_main, w_lr=w_lr,
                 q_norm_g=q_norm_g[l], k_norm_g=k_norm_g[l], lam_q1=lam_q1[l], lam_k1=lam_k1[l],
                 lam_q2=lam_q2[l], lam_k2=lam_k2[l], subln_g=subln_g[l],
                 wgf=_pad_gate_w(w_gate_f[l], 0), b_gate_f=b_gate_f[l],
                 wgb=_pad_gate_w(w_gate_b[l], GATE_RANK), b_gate_b=b_gate_b[l],
                 gla_norm_g=gla_norm_g[l], w_branch_a=w_branch_a[l].astype(BF16),
                 w_branch_b=w_branch_b[l].astype(BF16), w_out=w_out[l].astype(BF16),
                 norm2_g=norm2_g[l], w_up=w_up[l].astype(BF16), w_down=w_down[l].astype(BF16))
        xp = _layer(xp, mod8[:nb_p], l, p)
        xs = _layer(xs, mod8[nb_p:nb_p + nb_s], l, p)
    return (xp, xs)
```

```python
import functools
import math

import jax
import jax.numpy as jnp
import numpy as np
from jax import lax
from jax.experimental import pallas as pl
from jax.experimental.pallas import tpu as pltpu

F32 = jnp.float32
BF16 = jnp.bfloat16

D_MODEL = 1024
HEAD_DIM_A = 64
N_HEADS_A = 8
HEAD_W_A = 2 * HEAD_DIM_A
N_HEADS_B = 4
KEY_DIM_B = 128
V_DIM_B = 256
GATE_RANK = 16
GATE_NORM = 16.0
N_BUCKETS = 32
MAX_DISTANCE = 128
D_FF = 4 * D_MODEL
EPS = 1e-6
LOG2E = math.log2(math.e)

N_MAIN = 8 * D_MODEL
N_REST = 5 * D_MODEL
LR_W = 128

VMEM_LIMIT = 56 * 1024 * 1024
ROW_TILE = 512
ATT_TILE = 256
ATT_BLOCK = 16
NEAR_TILES = 5
EXP2_RANGE = 100.0
GLA_CHUNK = 128
SUBLANES = 8


def _cparams(sem):
    return pltpu.CompilerParams(dimension_semantics=sem, vmem_limit_bytes=VMEM_LIMIT)


def _ada_kernel(c_ref, w_ref, b_ref, o_ref):
    c = c_ref[...]
    a = c / (1.0 + jnp.exp(-c))
    o_ref[...] = jnp.dot(a, w_ref[...], preferred_element_type=F32,
                         precision=lax.Precision.HIGHEST) + b_ref[...]


def _ada(c_all, w_ada, b_ada):
    r = c_all.shape[0]
    tn = 1024
    return pl.pallas_call(
        _ada_kernel,
        grid=(6 * D_MODEL // tn,),
        in_specs=[pl.BlockSpec((r, D_MODEL), lambda n: (0, 0)),
                  pl.BlockSpec((D_MODEL, tn), lambda n: (0, n)),
                  pl.BlockSpec((1, tn), lambda n: (0, n))],
        out_specs=pl.BlockSpec((r, tn), lambda n: (0, n)),
        out_shape=jax.ShapeDtypeStruct((r, 6 * D_MODEL), F32),
        compiler_params=_cparams(("parallel",)),
        name="ada",
    )(c_all, w_ada, b_ada.reshape(1, -1))


def _modulated_norm(x, g, shift, scale):
    ms = jnp.mean(x * x, axis=-1, keepdims=True)
    return x * lax.rsqrt(ms + EPS) * g * (1.0 + scale) + shift


def _pair_rms(x, g):
    lane = lax.broadcasted_iota(jnp.int32, x.shape, 1)
    x2 = x * x
    s_all = jnp.sum(x2, axis=-1, keepdims=True)
    s_lo = jnp.sum(jnp.where(lane < HEAD_DIM_A, x2, 0.0), axis=-1, keepdims=True)
    ss = jnp.where(lane < HEAD_DIM_A, s_lo, s_all - s_lo)
    return x * lax.rsqrt(ss * (1.0 / HEAD_DIM_A) + EPS) * g


def _proj_kernel(x_ref, mod_ref, g_ref, w_ref, wlr_ref, qg_ref, kg_ref,
                 qt_ref, kn_ref, vt_ref, o_ref, lr_ref):
    h = _modulated_norm(x_ref[...], g_ref[...], mod_ref[0, 0:1, :], mod_ref[0, 1:2, :]).astype(BF16)
    lr_ref[...] = jnp.dot(h, wlr_ref[...], preferred_element_type=F32)
    piece = lambda j: jnp.dot(h, w_ref[:, j * D_MODEL:(j + 1) * D_MODEL], preferred_element_type=F32)
    heads = [slice(i * HEAD_W_A, (i + 1) * HEAD_W_A) for i in range(N_HEADS_A)]

    qa = piece(0)
    for i, sl in enumerate(heads):
        qn = _pair_rms(qa[:, sl], qg_ref[...]) * (HEAD_DIM_A ** -0.5 * LOG2E)
        qt_ref[0, i] = qn.T.astype(BF16)
    ka = piece(1)
    for sl in heads:
        kn_ref[:, sl] = _pair_rms(ka[:, sl], kg_ref[...]).astype(BF16)
    va = piece(2)
    for i, sl in enumerate(heads):
        vt_ref[0, i] = va[:, sl].T.astype(BF16)
    for j in range(3, N_MAIN // D_MODEL):
        o_ref[:, (j - 3) * D_MODEL:(j - 2) * D_MODEL] = piece(j).astype(o_ref.dtype)


def _proj(x2, mod8, g1, w_main, w_lr, qg2, kg2, seq):
    t = x2.shape[0]
    tm = ROW_TILE
    spt = seq // tm
    once = pl.Buffered(1)
    const = lambda shape, **kw: pl.BlockSpec(shape, lambda i: (0,) * len(shape), **kw)
    tspec = pl.BlockSpec((1, N_HEADS_A, HEAD_W_A, tm), lambda i: (i // spt, 0, 0, i % spt))
    tshape = jax.ShapeDtypeStruct((t // seq, N_HEADS_A, HEAD_W_A, seq), BF16)
    return pl.pallas_call(
        _proj_kernel,
        grid=(t // tm,),
        in_specs=[pl.BlockSpec((tm, D_MODEL), lambda i: (i, 0)),
                  pl.BlockSpec((1, 8, D_MODEL), lambda i: (i // spt, 0, 0)),
                  const((1, D_MODEL)),
                  const((D_MODEL, N_MAIN), pipeline_mode=once),
                  const((D_MODEL, LR_W), pipeline_mode=once),
                  const((1, HEAD_W_A)), const((1, HEAD_W_A))],
        out_specs=[tspec,
                   pl.BlockSpec((tm, D_MODEL), lambda i: (i, 0)),
                   tspec,
                   pl.BlockSpec((tm, N_REST), lambda i: (i, 0)),
                   pl.BlockSpec((tm, LR_W), lambda i: (i, 0))],
        out_shape=[tshape,
                   jax.ShapeDtypeStruct((t, D_MODEL), BF16),
                   tshape,
                   jax.ShapeDtypeStruct((t, N_REST), BF16),
                   jax.ShapeDtypeStruct((t, LR_W), F32)],
        compiler_params=_cparams(("parallel",)),
        name="proj",
    )(x2, mod8, g1, w_main, w_lr, qg2, kg2)


def _t5_bucket(rel):
    nb = N_BUCKETS // 2
    max_exact = nb // 2
    ret = (rel > 0).astype(jnp.int32) * nb
    n = jnp.abs(rel)
    nf = jnp.maximum(n, 1).astype(F32)
    large = max_exact + (jnp.log(nf / max_exact) / math.log(MAX_DISTANCE / max_exact)
                         * (nb - max_exact)).astype(jnp.int32)
    large = jnp.minimum(large, nb - 1)
    return ret + jnp.where(n < max_exact, n, large)


def _bias_kernel(tbl_ref, bkt_ref, qg_ref, kg_ref, o_ref, ok_ref):
    h = pl.program_id(0)
    bkt = bkt_ref[...]
    acc = jnp.zeros(bkt.shape, F32)
    babs = jnp.float32(0.0)
    for i in range(N_BUCKETS):
        acc = jnp.where(bkt == i, tbl_ref[i, h], acc)
        babs = jnp.maximum(babs, jnp.abs(tbl_ref[i, h]))
    o_ref[0] = acc * LOG2E
    gmax = lambda r: jnp.max(jnp.abs(r[...]), axis=1, keepdims=True)
    bound = gmax(qg_ref) * gmax(kg_ref) * (1.02 * HEAD_DIM_A * HEAD_DIM_A ** -0.5 * LOG2E) + babs * LOG2E
    ok_ref[0] = jnp.broadcast_to((bound <= EXP2_RANGE).astype(jnp.int32), ok_ref.shape[1:])


def _bias_tiles(rel_bias, qg2, kg2):
    t = ATT_TILE
    kk = jnp.arange(t, dtype=jnp.int32)[:, None]
    qq = jnp.arange(t, dtype=jnp.int32)[None, :]
    rel = jnp.stack([(d - NEAR_TILES // 2) * t + kk - qq for d in range(NEAR_TILES)])
    gspec = pl.BlockSpec((1, HEAD_W_A), lambda h: (0, 0))
    tiles, ok = pl.pallas_call(
        _bias_kernel,
        grid=(N_HEADS_A,),
        in_specs=[pl.BlockSpec(memory_space=pltpu.SMEM),
                  pl.BlockSpec((NEAR_TILES, t, t), lambda h: (0, 0, 0)), gspec, gspec],
        out_specs=[pl.BlockSpec((1, NEAR_TILES, t, t), lambda h: (h, 0, 0, 0)),
                   pl.BlockSpec((1, 8, HEAD_W_A), lambda h: (h, 0, 0))],
        out_shape=[jax.ShapeDtypeStruct((N_HEADS_A, NEAR_TILES, t, t), F32),
                   jax.ShapeDtypeStruct((N_HEADS_A, 8, HEAD_W_A), jnp.int32)],
        compiler_params=_cparams(("parallel",)),
        name="bias",
    )(rel_bias, _t5_bucket(rel), qg2, kg2)
    return tiles, ok[:, 0, 0]


def _attn_kernel(ok_ref, qt_ref, k_ref, vt_ref, bias_ref, lamv_ref, subg_ref, o_ref,
                 w_scr, m_scr, l_scr, acc_scr, l8_scr, *, nk, qsub, lambda_init):
    t = ATT_TILE
    blk = min(ATT_BLOCK, nk)
    bt = blk * t
    h = pl.program_id(1)
    tiles = [(u, pl.program_id(2) * qsub + u) for u in range(qsub)]

    z = jnp.zeros((HEAD_DIM_A, t), BF16)
    for u, _ in tiles:
        q = qt_ref[0, 0, :, u * t:(u + 1) * t]
        w_scr[u, :, :t] = jnp.concatenate([q[:HEAD_DIM_A], z], axis=0)
        w_scr[u, :, t:] = jnp.concatenate([z, q[HEAD_DIM_A:]], axis=0)

    def chunk_bias(kj, qi):
        b = bias_ref[0, jnp.clip(kj - qi, -2, 2) + 2]
        return jnp.concatenate([b, b], axis=1)

    @pl.when(ok_ref[h] != 0)
    def _():
        l8_scr[...] = jnp.zeros(l8_scr.shape, F32)
        acc_scr[...] = jnp.zeros(acc_scr.shape, F32)

        def block(j, carry):
            k0 = pl.multiple_of(j * bt, bt)
            for u, qi in tiles:
                s = jnp.dot(k_ref[0, pl.ds(k0, bt), :], w_scr[u], preferred_element_type=F32)
                p = [jnp.exp2(s[c * t:(c + 1) * t] + chunk_bias(j * blk + c, qi)) for c in range(blk)]
                p = jnp.concatenate(p, axis=0)
                l8_scr[u] += jnp.sum(p.reshape(bt // 8, 8, 2 * t), axis=0)
                acc_scr[u] += jnp.dot(vt_ref[0, 0, :, pl.ds(k0, bt)], p.astype(BF16),
                                      preferred_element_type=F32)
            return carry

        lax.fori_loop(0, nk // blk, block, 0)
        l_scr[...] = jnp.sum(l8_scr[...], axis=1, keepdims=True)

    @pl.when(ok_ref[h] == 0)
    def _():
        m_scr[...] = jnp.full(m_scr.shape, -jnp.inf, F32)
        l_scr[...] = jnp.zeros(l_scr.shape, F32)
        acc_scr[...] = jnp.zeros(acc_scr.shape, F32)

        def online_step(kj, carry):
            k0 = pl.multiple_of(kj * t, t)
            for u, qi in tiles:
                s = jnp.dot(k_ref[0, pl.ds(k0, t), :], w_scr[u], preferred_element_type=F32)
                s = s + chunk_bias(kj, qi)
                m_old = m_scr[u]
                m_new = jnp.maximum(m_old, jnp.max(s, axis=0, keepdims=True))
                alpha = jnp.exp2(m_old - m_new)
                p = jnp.exp2(s - m_new)
                l_scr[u] = alpha * l_scr[u] + jnp.sum(p, axis=0, keepdims=True)
                pv = jnp.dot(vt_ref[0, 0, :, pl.ds(k0, t)], p.astype(BF16), preferred_element_type=F32)
                acc_scr[u] = alpha * acc_scr[u] + pv
                m_scr[u] = m_new
            return carry

        lax.fori_loop(0, nk, online_step, 0)

    lv = lamv_ref[...]
    lam = (jnp.exp(jnp.sum(lv[0:1] * lv[1:2], axis=-1, keepdims=True))
           - jnp.exp(jnp.sum(lv[2:3] * lv[3:4], axis=-1, keepdims=True)) + lambda_init)
    for u, _ in tiles:
        acc = acc_scr[u]
        inv = 1.0 / l_scr[u]
        o = acc[:, :t] * inv[:, :t] - lam * (acc[:, t:] * inv[:, t:])
        ms = jnp.mean(o * o, axis=0, keepdims=True)
        y = (o * lax.rsqrt(ms + EPS)).T
        o_ref[0, u * t:(u + 1) * t, :] = (y * subg_ref[...] * (1.0 - lambda_init)).astype(o_ref.dtype)


def _attn(range_ok, qt, kn, vt, bias_t, lamv, subg, lambda_init):
    b, s, _ = kn.shape
    t = ATT_TILE
    nk = s // t
    assert nk % min(ATT_BLOCK, nk) == 0 and t >= MAX_DISTANCE
    qsub = 4 if nk <= ATT_BLOCK and nk % 4 == 0 else 2
    kern = functools.partial(_attn_kernel, nk=nk, qsub=qsub, lambda_init=lambda_init)
    const = lambda shape: pl.BlockSpec(shape, lambda bi, h, qi: (0,) * len(shape))
    tq = qsub * t
    return pl.pallas_call(
        kern,
        grid=(b, N_HEADS_A, s // tq),
        in_specs=[pl.BlockSpec(memory_space=pltpu.SMEM),
                  pl.BlockSpec((1, 1, HEAD_W_A, tq), lambda bi, h, qi: (bi, h, 0, qi)),
                  pl.BlockSpec((1, s, HEAD_W_A), lambda bi, h, qi: (bi, 0, h)),
                  pl.BlockSpec((1, 1, HEAD_W_A, s), lambda bi, h, qi: (bi, h, 0, 0)),
                  pl.BlockSpec((1, NEAR_TILES, t, t), lambda bi, h, qi: (h, 0, 0, 0)),
                  const((8, HEAD_W_A)), const((1, HEAD_W_A))],
        out_specs=pl.BlockSpec((1, tq, HEAD_W_A), lambda bi, h, qi: (bi, qi, h)),
        out_shape=jax.ShapeDtypeStruct((b, s, D_MODEL), BF16),
        scratch_shapes=[pltpu.VMEM((qsub, HEAD_W_A, 2 * t), BF16),
                        pltpu.VMEM((qsub, 1, 2 * t), F32),
                        pltpu.VMEM((qsub, 1, 2 * t), F32),
                        pltpu.VMEM((qsub, HEAD_W_A, 2 * t), F32),
                        pltpu.VMEM((qsub, 8, 2 * t), F32)],
        compiler_params=_cparams(("parallel", "parallel", "parallel")),
        name="attn",
    )(range_ok, qt, kn, vt, bias_t, lamv, subg)


def _log_gate(lr, wg_ref, bg_ref):
    x = jnp.dot(lr.astype(BF16), wg_ref[...], preferred_element_type=F32) + bg_ref[...]
    return (jnp.minimum(x, 0.0) - jnp.log1p(jnp.exp(-jnp.abs(x)))) * (LOG2E / GATE_NORM)


def _gla_bwd_kernel(q_ref, k_ref, v_ref, og_ref, lr_ref, of_ref, wg_ref, bg_ref, gn_ref, o_ref, s_scr):
    c = GLA_CHUNK

    @pl.when(pl.program_id(1) == 0)
    def _():
        s_scr[...] = jnp.zeros(s_scr.shape, F32)

    g = _log_gate(lr_ref[0], wg_ref, bg_ref)
    row = lax.broadcasted_iota(jnp.int32, g.shape, 0)
    suf = g
    sh = 1
    while sh < c:
        suf = suf + jnp.where(row < c - sh, pltpu.roll(suf, c - sh, 0), 0.0)
        sh *= 2
    tot = suf[0:1, :]
    for h in range(N_HEADS_B):
        sl = slice(h * KEY_DIM_B, (h + 1) * KEY_DIM_B)
        vs = slice(h * V_DIM_B, (h + 1) * V_DIM_B)
        q = q_ref[0, :, sl].astype(F32) * KEY_DIM_B ** -0.5
        k = k_ref[0, :, sl].astype(F32)
        qd = (q * jnp.exp2(suf[:, sl])).astype(BF16)
        kd = k * jnp.exp2(tot[:, sl] - suf[:, sl])
        st = s_scr[h]
        o = jnp.dot(qd, st.astype(BF16), preferred_element_type=F32) + of_ref[0, :, vs]
        dec = jnp.broadcast_to(jnp.exp2(tot[:, sl]), (c, KEY_DIM_B)).T[:, 0:1]
        s_scr[h] = st * dec + jnp.dot(kd.T.astype(BF16), v_ref[0, :, vs], preferred_element_type=F32)
        ms = jnp.mean(o * o, axis=-1, keepdims=True)
        og = og_ref[0, :, vs].astype(F32)
        o_ref[0, :, vs] = (o * lax.rsqrt(ms + EPS) * gn_ref[...] * (og / (1.0 + jnp.exp(-og)))
                           ).astype(o_ref.dtype)


def _gla_chunk_scores(q, k, gf, gb):
    c = GLA_CHUNK
    ri = lax.broadcasted_iota(jnp.int32, (c, c), 0)
    ci = lax.broadcasted_iota(jnp.int32, (c, c), 1)
    xr = jnp.bitwise_xor(ri, ci)
    nt = (((1,), (1,)), ((), ()))
    pre, tf, suf, tb = gf, gf, gb, gb
    a = jnp.where(ri == ci, 2.0 * lax.dot_general(q.astype(BF16), k.astype(BF16), nt,
                                                  preferred_element_type=F32), 0.0)

    def level_scores(a, blk, q_arg, k_arg):
        r = lax.dot_general((q * jnp.exp2(q_arg)).astype(BF16), (k * jnp.exp2(k_arg)).astype(BF16),
                            nt, preferred_element_type=F32)
        return jnp.where(jnp.logical_and(xr >= blk, xr < 2 * blk), r, a)

    blk = 1
    while blk < SUBLANES:
        hi = jnp.bitwise_and(ri, blk) != 0
        a = level_scores(a, blk, jnp.where(hi, pre, suf), jnp.where(hi, tb - suf, tf - pre))
        tf_dn, tf_up = pltpu.roll(tf, blk, 0), pltpu.roll(tf, c - blk, 0)
        tb_dn, tb_up = pltpu.roll(tb, blk, 0), pltpu.roll(tb, c - blk, 0)
        pre = pre + jnp.where(hi, tf_dn, 0.0)
        suf = suf + jnp.where(hi, 0.0, tb_up)
        tf = tf + jnp.where(hi, tf_dn, tf_up)
        tb = tb + jnp.where(hi, tb_dn, tb_up)
        blk *= 2
    ns = c // SUBLANES
    slabs = lambda x: [x[SUBLANES * r:SUBLANES * (r + 1)] for r in range(ns)]
    pre, suf, tf, tb = slabs(pre), slabs(suf), slabs(tf), slabs(tb)
    while blk < c:
        g = blk // SUBLANES
        up = [bool(r & g) for r in range(ns)]
        q_arg = [pre[r] if up[r] else suf[r] for r in range(ns)]
        k_arg = [tb[r] - suf[r] if up[r] else tf[r] - pre[r] for r in range(ns)]
        a = level_scores(a, blk, jnp.concatenate(q_arg, axis=0), jnp.concatenate(k_arg, axis=0))
        pre = [pre[r] + tf[r - g] if up[r] else pre[r] for r in range(ns)]
        suf = [suf[r] if up[r] else suf[r] + tb[r + g] for r in range(ns)]
        tf_pair = {r: tf[r] + tf[r + g] for r in range(ns) if not up[r]}
        tb_pair = {r: tb[r] + tb[r + g] for r in range(ns) if not up[r]}
        tf = [tf_pair[r - g] if up[r] else tf_pair[r] for r in range(ns)]
        tb = [tb_pair[r - g] if up[r] else tb_pair[r] for r in range(ns)]
        blk *= 2
    return a, jnp.concatenate(pre, axis=0), jnp.concatenate(tf, axis=0)


def _gla_fwd_chunk(q, k, v, gf, gb, s_scr, h):
    a, pre, tf = _gla_chunk_scores(q, k, gf, gb)
    st = s_scr[h]
    o = (jnp.dot(a.astype(BF16), v, preferred_element_type=F32)
         + jnp.dot((q * jnp.exp2(pre)).astype(BF16), st.astype(BF16), preferred_element_type=F32))
    kd = k * jnp.exp2(tf - pre)
    s_scr[h] = st * jnp.exp2(tf).T[:, 0:1] + jnp.dot(kd.T.astype(BF16), v, preferred_element_type=F32)
    return o


def _gla_fwd_kernel(q_ref, k_ref, v_ref, lr_ref, wgf_ref, bgf_ref, wgb_ref, bgb_ref, o_ref, s_scr):
    @pl.when(pl.program_id(1) == 0)
    def _():
        s_scr[...] = jnp.zeros(s_scr.shape, F32)

    lr = lr_ref[0]
    gf = _log_gate(lr, wgf_ref, bgf_ref)
    gb = _log_gate(lr, wgb_ref, bgb_ref)
    for h in range(N_HEADS_B):
        sl = slice(h * KEY_DIM_B, (h + 1) * KEY_DIM_B)
        vs = slice(h * V_DIM_B, (h + 1) * V_DIM_B)
        q = q_ref[0, :, sl].astype(F32) * KEY_DIM_B ** -0.5
        k = k_ref[0, :, sl].astype(F32)
        o_ref[0, :, vs] = _gla_fwd_chunk(q, k, v_ref[0, :, vs], gf[:, sl], gb[:, sl], s_scr, h)


def _gla(main3, lr3, wgf, bgf, wgb, bgb, gn):
    b, s, _ = main3.shape
    c = GLA_CHUNK
    nc = s // c
    hk = N_HEADS_B * KEY_DIM_B
    state = pltpu.VMEM((N_HEADS_B, KEY_DIM_B, V_DIM_B), F32)
    full = lambda shape: pl.BlockSpec(shape, lambda bi, i: (0,) * len(shape))
    fwd = lambda w, j: pl.BlockSpec((1, c, w), lambda bi, i: (bi, i, j))
    of3 = pl.pallas_call(
        _gla_fwd_kernel,
        grid=(b, nc),
        in_specs=[fwd(hk, 0), fwd(hk, 1), fwd(D_MODEL, 1), fwd(LR_W, 0),
                  full((LR_W, hk)), full((1, hk)), full((LR_W, hk)), full((1, hk))],
        out_specs=fwd(D_MODEL, 0),
        out_shape=jax.ShapeDtypeStruct((b, s, D_MODEL), F32),
        scratch_shapes=[state],
        compiler_params=_cparams(("parallel", "arbitrary")),
        name="gla_fwd",
    )(main3, main3, main3, lr3, wgf, bgf, wgb, bgb)

    rev = lambda i: nc - 1 - i
    blk = lambda w, j: pl.BlockSpec((1, c, w), lambda bi, i: (bi, rev(i), j))
    return pl.pallas_call(
        _gla_bwd_kernel,
        grid=(b, nc),
        in_specs=[blk(hk, 0), blk(hk, 1), blk(D_MODEL, 1), blk(D_MODEL, 2), blk(LR_W, 0), blk(D_MODEL, 0),
                  full((LR_W, hk)), full((1, hk)), full((1, V_DIM_B))],
        out_specs=blk(D_MODEL, 0),
        out_shape=jax.ShapeDtypeStruct((b, s, D_MODEL), BF16),
        scratch_shapes=[state],
        compiler_params=_cparams(("parallel", "arbitrary")),
        name="gla_bwd",
    )(main3, main3, main3, main3, lr3, of3, wgb, bgb, gn)


def _mix_kernel(x_ref, oa_ref, ob_ref, ga_ref, gb_ref, mod_ref, wa_ref, wb_ref, wo_ref, o_ref):
    ya = jnp.dot(oa_ref[...], wa_ref[...], preferred_element_type=F32)
    yb = jnp.dot(ob_ref[...], wb_ref[...], preferred_element_type=F32)
    sig = lambda r: 1.0 / (1.0 + jnp.exp(-r[...].astype(F32)))
    merged = sig(ga_ref) * ya + sig(gb_ref) * yb
    o_ref[...] = x_ref[...] + mod_ref[0, 2:3, :] * jnp.dot(merged.astype(BF16), wo_ref[...],
                                                           preferred_element_type=F32)


def _mix(x2, oa2, ob2, main2, mod8, wa, wb, wo, seq):
    t = x2.shape[0]
    tm = ROW_TILE
    row = lambda j: pl.BlockSpec((tm, D_MODEL), lambda i: (i, j))
    wspec = pl.BlockSpec((D_MODEL, D_MODEL), lambda i: (0, 0))
    return pl.pallas_call(
        _mix_kernel,
        grid=(t // tm,),
        in_specs=[row(0), row(0), row(0), row(3), row(4),
                  pl.BlockSpec((1, 8, D_MODEL), lambda i: ((i * tm) // seq, 0, 0)),
                  wspec, wspec, wspec],
        out_specs=row(0),
        out_shape=jax.ShapeDtypeStruct((t, D_MODEL), F32),
        compiler_params=_cparams(("parallel",)),
        name="mix",
    )(x2, oa2, ob2, main2, main2, mod8, wa, wb, wo)


def _mlp_kernel(x_ref, mod_ref, g_ref, wu_ref, wd_ref, o_ref):
    x = x_ref[...]
    h = _modulated_norm(x, g_ref[...], mod_ref[0, 3:4, :], mod_ref[0, 4:5, :]).astype(BF16)
    acc = jnp.zeros(x.shape, F32)
    for j in range(D_FF // D_MODEL):
        cs = slice(j * D_MODEL, (j + 1) * D_MODEL)
        u = jnp.maximum(jnp.dot(h, wu_ref[:, cs], preferred_element_type=F32), 0.0)
        acc = acc + jnp.dot((u * u).astype(BF16), wd_ref[cs, :], preferred_element_type=F32)
    o_ref[...] = x + mod_ref[0, 5:6, :] * acc


def _mlp(x2, mod8, g2, wu, wd, seq):
    t = x2.shape[0]
    tm = ROW_TILE
    once = pl.Buffered(1)
    return pl.pallas_call(
        _mlp_kernel,
        grid=(t // tm,),
        in_specs=[pl.BlockSpec((tm, D_MODEL), lambda i: (i, 0)),
                  pl.BlockSpec((1, 8, D_MODEL), lambda i: ((i * tm) // seq, 0, 0)),
                  pl.BlockSpec((1, D_MODEL), lambda i: (0, 0)),
                  pl.BlockSpec((D_MODEL, D_FF), lambda i: (0, 0), pipeline_mode=once),
                  pl.BlockSpec((D_FF, D_MODEL), lambda i: (0, 0), pipeline_mode=once)],
        out_specs=pl.BlockSpec((tm, D_MODEL), lambda i: (i, 0)),
        out_shape=jax.ShapeDtypeStruct((t, D_MODEL), F32),
        compiler_params=_cparams(("parallel",)),
        name="mlp",
    )(x2, mod8, g2, wu, wd)


def _split_w_in(w_in):
    o = np.cumsum([0, 1024, 1024, 1024, 512, 512, 1024, 1024, GATE_RANK, GATE_RANK, 1024, 1024])
    main = jnp.concatenate([w_in[:, o[0]:o[7]], w_in[:, o[9]:o[11]]], axis=1)
    lr = jnp.pad(w_in[:, o[7]:o[9]], ((0, 0), (0, LR_W - 2 * GATE_RANK)))
    return main.astype(BF16), lr.astype(BF16)


def _pad_gate_w(w_gate, row0):
    return jnp.pad(w_gate, ((row0, LR_W - GATE_RANK - row0), (0, 0))).astype(BF16)


def _layer(x, mod8, layer_idx, p):
    b, s, _ = x.shape
    t = b * s
    lambda_init = 0.8 - 0.6 * math.exp(-0.3 * layer_idx)
    x2 = x.reshape(t, D_MODEL)
    row = lambda v: v.reshape(1, -1).astype(F32)

    qg2, kg2 = row(jnp.tile(p["q_norm_g"], 2)), row(jnp.tile(p["k_norm_g"], 2))
    qt, kn2, vt, main2, lr2 = _proj(x2, mod8, row(p["norm1_g"]), p["w_main"], p["w_lr"], qg2, kg2, s)
    kn = kn2.reshape(b, s, D_MODEL)
    main3 = main2.reshape(b, s, N_REST)
    lr3 = lr2.reshape(b, s, LR_W)
    lamv = jnp.zeros((8, HEAD_W_A), F32).at[0:4, :HEAD_DIM_A].set(
        jnp.stack([p["lam_q1"], p["lam_k1"], p["lam_q2"], p["lam_k2"]]).astype(F32))
    bias_t, range_ok = _bias_tiles(p["rel_bias"], qg2, kg2)
    oa = _attn(range_ok, qt, kn, vt, bias_t, lamv, row(p["subln_g"]), lambda_init)

    ob = _gla(main3, lr3, p["wgf"], row(p["b_gate_f"]), p["wgb"], row(p["b_gate_b"]), row(p["gla_norm_g"]))

    x1 = _mix(x2, oa.reshape(t, D_MODEL), ob.reshape(t, D_MODEL), main2, mod8,
              p["w_branch_a"], p["w_branch_b"], p["w_out"], s)
    y = _mlp(x1, mod8, row(p["norm2_g"]), p["w_up"], p["w_down"], s)
    return y.reshape(b, s, D_MODEL)


def kernel(x_prompt, x_sample, c_prompt, c_sample, rel_bias, w_ada, b_ada, norm1_g, w_in, q_norm_g, k_norm_g, lam_q1, lam_k1, lam_q2, lam_k2, subln_g, w_gate_f, b_gate_f, w_gate_b, b_gate_b, gla_norm_g, w_branch_a, w_branch_b, w_out, norm2_g, w_up, w_down):
    depth = w_in.shape[0]
    nb_p, nb_s = c_prompt.shape[0], c_sample.shape[0]
    rows = -(-(nb_p + nb_s) // 8) * 8
    c_all = jnp.pad(jnp.concatenate([c_prompt, c_sample], axis=0), ((0, rows - nb_p - nb_s), (0, 0)))

    xp, xs = x_prompt, x_sample
    for l in range(depth):
        mod = _ada(c_all, w_ada[l], b_ada[l]).reshape(rows, 6, D_MODEL)
        mod8 = jnp.pad(mod, ((0, 0), (0, 2), (0, 0)))
        w_main, w_lr = _split_w_in(w_in[l])
        p = dict(rel_bias=rel_bias.astype(F32), norm1_g=norm1_g[l], w_main=w_main, w_lr=w_lr,
                 q_norm_g=q_norm_g[l], k_norm_g=k_norm_g[l], lam_q1=lam_q1[l], lam_k1=lam_k1[l],
                 lam_q2=lam_q2[l], lam_k2=lam_k2[l], subln_g=subln_g[l],
                 wgf=_pad_gate_w(w_gate_f[l], 0), b_gate_f=b_gate_f[l],
                 wgb=_pad_gate_w(w_gate_b[l], GATE_RANK), b_gate_b=b_gate_b[l],
                 gla_norm_g=gla_norm_g[l], w_branch_a=w_branch_a[l].astype(BF16),
                 w_branch_b=w_branch_b[l].astype(BF16), w_out=w_out[l].astype(BF16),
                 norm2_g=norm2_g[l], w_up=w_up[l].astype(BF16), w_down=w_down[l].astype(BF16))
        xp = _layer(xp, mod8[:nb_p], l, p)
        xs = _layer(xs, mod8[nb_p:nb_p + nb_s], l, p)
    return (xp, xs)
```

```python
import functools
import math

import jax
import jax.numpy as jnp
import numpy as np
from jax import lax
from jax.experimental import pallas as pl
from jax.experimental.pallas import tpu as pltpu

F32 = jnp.float32
BF16 = jnp.bfloat16

D_MODEL = 1024
HEAD_DIM_A = 64
N_HEADS_A = 8
HEAD_W_A = 2 * HEAD_DIM_A
N_HEADS_B = 4
KEY_DIM_B = 128
V_DIM_B = 256
GATE_RANK = 16
GATE_NORM = 16.0
N_BUCKETS = 32
MAX_DISTANCE = 128
D_FF = 4 * D_MODEL
EPS = 1e-6
LOG2E = math.log2(math.e)

N_MAIN = 8 * D_MODEL
N_REST = 5 * D_MODEL
LR_W = 128

VMEM_LIMIT = 56 * 1024 * 1024
ROW_TILE = 512
ATT_TILE = 256
ATT_BLOCK = 32
NEAR_TILES = 5
EXP2_RANGE = 100.0
GLA_CHUNK = 128
SUBLANES = 8


def _cparams(sem):
    return pltpu.CompilerParams(dimension_semantics=sem, vmem_limit_bytes=VMEM_LIMIT)


def _ada_kernel(c_ref, w_ref, b_ref, o_ref):
    c = c_ref[...]
    a = c / (1.0 + jnp.exp(-c))
    o_ref[...] = jnp.dot(a, w_ref[...], preferred_element_type=F32,
                         precision=lax.Precision.HIGHEST) + b_ref[...]


def _ada(c_all, w_ada, b_ada):
    r = c_all.shape[0]
    tn = 1024
    return pl.pallas_call(
        _ada_kernel,
        grid=(6 * D_MODEL // tn,),
        in_specs=[pl.BlockSpec((r, D_MODEL), lambda n: (0, 0)),
                  pl.BlockSpec((D_MODEL, tn), lambda n: (0, n)),
                  pl.BlockSpec((1, tn), lambda n: (0, n))],
        out_specs=pl.BlockSpec((r, tn), lambda n: (0, n)),
        out_shape=jax.ShapeDtypeStruct((r, 6 * D_MODEL), F32),
        compiler_params=_cparams(("parallel",)),
        name="ada",
    )(c_all, w_ada, b_ada.reshape(1, -1))


def _modulated_norm(x, g, shift, scale):
    ms = jnp.mean(x * x, axis=-1, keepdims=True)
    return x * lax.rsqrt(ms + EPS) * g * (1.0 + scale) + shift


def _pair_rms(x, g):
    lane = lax.broadcasted_iota(jnp.int32, x.shape, 1)
    x2 = x * x
    s_all = jnp.sum(x2, axis=-1, keepdims=True)
    s_lo = jnp.sum(jnp.where(lane < HEAD_DIM_A, x2, 0.0), axis=-1, keepdims=True)
    ss = jnp.where(lane < HEAD_DIM_A, s_lo, s_all - s_lo)
    return x * lax.rsqrt(ss * (1.0 / HEAD_DIM_A) + EPS) * g


def _proj_kernel(x_ref, mod_ref, g_ref, w_ref, wlr_ref, qg_ref, kg_ref,
                 qt_ref, kn_ref, vt_ref, o_ref, lr_ref):
    h = _modulated_norm(x_ref[...], g_ref[...], mod_ref[0, 0:1, :], mod_ref[0, 1:2, :]).astype(BF16)
    lr_ref[...] = jnp.dot(h, wlr_ref[...], preferred_element_type=F32)
    piece = lambda j: jnp.dot(h, w_ref[:, j * D_MODEL:(j + 1) * D_MODEL], preferred_element_type=F32)
    heads = [slice(i * HEAD_W_A, (i + 1) * HEAD_W_A) for i in range(N_HEADS_A)]

    qa = piece(0)
    for i, sl in enumerate(heads):
        qn = _pair_rms(qa[:, sl], qg_ref[...]) * (HEAD_DIM_A ** -0.5 * LOG2E)
        qt_ref[0, i] = qn.T.astype(BF16)
    ka = piece(1)
    for sl in heads:
        kn_ref[:, sl] = _pair_rms(ka[:, sl], kg_ref[...]).astype(BF16)
    va = piece(2)
    for i, sl in enumerate(heads):
        vt_ref[0, i] = va[:, sl].T.astype(BF16)
    for j in range(3, N_MAIN // D_MODEL):
        o_ref[:, (j - 3) * D_MODEL:(j - 2) * D_MODEL] = piece(j).astype(o_ref.dtype)


def _proj(x2, mod8, g1, w_main, w_lr, qg2, kg2, seq):
    t = x2.shape[0]
    tm = ROW_TILE
    spt = seq // tm
    once = pl.Buffered(1)
    const = lambda shape, **kw: pl.BlockSpec(shape, lambda i: (0,) * len(shape), **kw)
    tspec = pl.BlockSpec((1, N_HEADS_A, HEAD_W_A, tm), lambda i: (i // spt, 0, 0, i % spt))
    tshape = jax.ShapeDtypeStruct((t // seq, N_HEADS_A, HEAD_W_A, seq), BF16)
    return pl.pallas_call(
        _proj_kernel,
        grid=(t // tm,),
        in_specs=[pl.BlockSpec((tm, D_MODEL), lambda i: (i, 0)),
                  pl.BlockSpec((1, 8, D_MODEL), lambda i: (i // spt, 0, 0)),
                  const((1, D_MODEL)),
                  const((D_MODEL, N_MAIN), pipeline_mode=once),
                  const((D_MODEL, LR_W), pipeline_mode=once),
                  const((1, HEAD_W_A)), const((1, HEAD_W_A))],
        out_specs=[tspec,
                   pl.BlockSpec((tm, D_MODEL), lambda i: (i, 0)),
                   tspec,
                   pl.BlockSpec((tm, N_REST), lambda i: (i, 0)),
                   pl.BlockSpec((tm, LR_W), lambda i: (i, 0))],
        out_shape=[tshape,
                   jax.ShapeDtypeStruct((t, D_MODEL), BF16),
                   tshape,
                   jax.ShapeDtypeStruct((t, N_REST), BF16),
                   jax.ShapeDtypeStruct((t, LR_W), F32)],
        compiler_params=_cparams(("parallel",)),
        name="proj",
    )(x2, mod8, g1, w_main, w_lr, qg2, kg2)


def _t5_bucket(rel):
    nb = N_BUCKETS // 2
    max_exact = nb // 2
    ret = (rel > 0).astype(jnp.int32) * nb
    n = jnp.abs(rel)
    nf = jnp.maximum(n, 1).astype(F32)
    large = max_exact + (jnp.log(nf / max_exact) / math.log(MAX_DISTANCE / max_exact)
                         * (nb - max_exact)).astype(jnp.int32)
    large = jnp.minimum(large, nb - 1)
    return ret + jnp.where(n < max_exact, n, large)


def _bias_kernel(tbl_ref, bkt_ref, qg_ref, kg_ref, o_ref, ok_ref):
    h = pl.program_id(0)
    bkt = bkt_ref[...]
    acc = jnp.zeros(bkt.shape, F32)
    babs = jnp.float32(0.0)
    for i in range(N_BUCKETS):
        acc = jnp.where(bkt == i, tbl_ref[i, h], acc)
        babs = jnp.maximum(babs, jnp.abs(tbl_ref[i, h]))
    o_ref[0] = acc * LOG2E
    gmax = lambda r: jnp.max(jnp.abs(r[...]), axis=1, keepdims=True)
    bound = gmax(qg_ref) * gmax(kg_ref) * (1.02 * HEAD_DIM_A * HEAD_DIM_A ** -0.5 * LOG2E) + babs * LOG2E
    ok_ref[0] = jnp.broadcast_to((bound <= EXP2_RANGE).astype(jnp.int32), ok_ref.shape[1:])


def _bias_tiles(rel_bias, qg2, kg2):
    t = ATT_TILE
    kk = jnp.arange(t, dtype=jnp.int32)[:, None]
    qq = jnp.arange(t, dtype=jnp.int32)[None, :]
    rel = jnp.stack([(d - NEAR_TILES // 2) * t + kk - qq for d in range(NEAR_TILES)])
    gspec = pl.BlockSpec((1, HEAD_W_A), lambda h: (0, 0))
    tiles, ok = pl.pallas_call(
        _bias_kernel,
        grid=(N_HEADS_A,),
        in_specs=[pl.BlockSpec(memory_space=pltpu.SMEM),
                  pl.BlockSpec((NEAR_TILES, t, t), lambda h: (0, 0, 0)), gspec, gspec],
        out_specs=[pl.BlockSpec((1, NEAR_TILES, t, t), lambda h: (h, 0, 0, 0)),
                   pl.BlockSpec((1, 8, HEAD_W_A), lambda h: (h, 0, 0))],
        out_shape=[jax.ShapeDtypeStruct((N_HEADS_A, NEAR_TILES, t, t), F32),
                   jax.ShapeDtypeStruct((N_HEADS_A, 8, HEAD_W_A), jnp.int32)],
        compiler_params=_cparams(("parallel",)),
        name="bias",
    )(rel_bias, _t5_bucket(rel), qg2, kg2)
    return tiles, ok[:, 0, 0]


def _attn_kernel(ok_ref, qt_ref, k_ref, vt_ref, bias_ref, lamv_ref, subg_ref, o_ref,
                 w_scr, m_scr, l_scr, acc_scr, l8_scr, *, nk, qsub, lambda_init):
    t = ATT_TILE
    blk = min(ATT_BLOCK, nk)
    bt = blk * t
    h = pl.program_id(1)
    tiles = [(u, pl.program_id(2) * qsub + u) for u in range(qsub)]

    z = jnp.zeros((HEAD_DIM_A, t), BF16)
    for u, _ in tiles:
        q = qt_ref[0, 0, :, u * t:(u + 1) * t]
        w_scr[u, :, :t] = jnp.concatenate([q[:HEAD_DIM_A], z], axis=0)
        w_scr[u, :, t:] = jnp.concatenate([z, q[HEAD_DIM_A:]], axis=0)

    def chunk_bias(kj, qi):
        b = bias_ref[0, jnp.clip(kj - qi, -2, 2) + 2]
        return jnp.concatenate([b, b], axis=1)

    @pl.when(ok_ref[h] != 0)
    def _():
        l8_scr[...] = jnp.zeros(l8_scr.shape, F32)
        acc_scr[...] = jnp.zeros(acc_scr.shape, F32)

        def block(j, carry):
            k0 = pl.multiple_of(j * bt, bt)
            for u, qi in tiles:
                s = jnp.dot(k_ref[0, pl.ds(k0, bt), :], w_scr[u], preferred_element_type=F32)
                p = [jnp.exp2(s[c * t:(c + 1) * t] + chunk_bias(j * blk + c, qi)) for c in range(blk)]
                p = jnp.concatenate(p, axis=0)
                l8_scr[u] += jnp.sum(p.reshape(bt // 8, 8, 2 * t), axis=0)
                acc_scr[u] += jnp.dot(vt_ref[0, 0, :, pl.ds(k0, bt)], p.astype(BF16),
                                      preferred_element_type=F32)
            return carry

        lax.fori_loop(0, nk // blk, block, 0)
        l_scr[...] = jnp.sum(l8_scr[...], axis=1, keepdims=True)

    @pl.when(ok_ref[h] == 0)
    def _():
        m_scr[...] = jnp.full(m_scr.shape, -jnp.inf, F32)
        l_scr[...] = jnp.zeros(l_scr.shape, F32)
        acc_scr[...] = jnp.zeros(acc_scr.shape, F32)

        def online_step(kj, carry):
            k0 = pl.multiple_of(kj * t, t)
            for u, qi in tiles:
                s = jnp.dot(k_ref[0, pl.ds(k0, t), :], w_scr[u], preferred_element_type=F32)
                s = s + chunk_bias(kj, qi)
                m_old = m_scr[u]
                m_new = jnp.maximum(m_old, jnp.max(s, axis=0, keepdims=True))
                alpha = jnp.exp2(m_old - m_new)
                p = jnp.exp2(s - m_new)
                l_scr[u] = alpha * l_scr[u] + jnp.sum(p, axis=0, keepdims=True)
                pv = jnp.dot(vt_ref[0, 0, :, pl.ds(k0, t)], p.astype(BF16), preferred_element_type=F32)
                acc_scr[u] = alpha * acc_scr[u] + pv
                m_scr[u] = m_new
            return carry

        lax.fori_loop(0, nk, online_step, 0)

    lv = lamv_ref[...]
    lam = (jnp.exp(jnp.sum(lv[0:1] * lv[1:2], axis=-1, keepdims=True))
           - jnp.exp(jnp.sum(lv[2:3] * lv[3:4], axis=-1, keepdims=True)) + lambda_init)
    for u, _ in tiles:
        acc = acc_scr[u]
        inv = 1.0 / l_scr[u]
        o = acc[:, :t] * inv[:, :t] - lam * (acc[:, t:] * inv[:, t:])
        ms = jnp.mean(o * o, axis=0, keepdims=True)
        y = (o * lax.rsqrt(ms + EPS)).T
        o_ref[0, u * t:(u + 1) * t, :] = (y * subg_ref[...] * (1.0 - lambda_init)).astype(o_ref.dtype)


def _attn(range_ok, qt, kn, vt, bias_t, lamv, subg, lambda_init):
    b, s, _ = kn.shape
    t = ATT_TILE
    nk = s // t
    assert nk % min(ATT_BLOCK, nk) == 0 and t >= MAX_DISTANCE
    qsub = 4 if nk <= ATT_BLOCK and nk % 4 == 0 else 2
    kern = functools.partial(_attn_kernel, nk=nk, qsub=qsub, lambda_init=lambda_init)
    const = lambda shape: pl.BlockSpec(shape, lambda bi, h, qi: (0,) * len(shape))
    tq = qsub * t
    return pl.pallas_call(
        kern,
        grid=(b, N_HEADS_A, s // tq),
        in_specs=[pl.BlockSpec(memory_space=pltpu.SMEM),
                  pl.BlockSpec((1, 1, HEAD_W_A, tq), lambda bi, h, qi: (bi, h, 0, qi)),
                  pl.BlockSpec((1, s, HEAD_W_A), lambda bi, h, qi: (bi, 0, h)),
                  pl.BlockSpec((1, 1, HEAD_W_A, s), lambda bi, h, qi: (bi, h, 0, 0)),
                  pl.BlockSpec((1, NEAR_TILES, t, t), lambda bi, h, qi: (h, 0, 0, 0)),
                  const((8, HEAD_W_A)), const((1, HEAD_W_A))],
        out_specs=pl.BlockSpec((1, tq, HEAD_W_A), lambda bi, h, qi: (bi, qi, h)),
        out_shape=jax.ShapeDtypeStruct((b, s, D_MODEL), BF16),
        scratch_shapes=[pltpu.VMEM((qsub, HEAD_W_A, 2 * t), BF16),
                        pltpu.VMEM((qsub, 1, 2 * t), F32),
                        pltpu.VMEM((qsub, 1, 2 * t), F32),
                        pltpu.VMEM((qsub, HEAD_W_A, 2 * t), F32),
                        pltpu.VMEM((qsub, 8, 2 * t), F32)],
        compiler_params=_cparams(("parallel", "parallel", "parallel")),
        name="attn",
    )(range_ok, qt, kn, vt, bias_t, lamv, subg)


def _log_gate(lr, wg_ref, bg_ref):
    x = jnp.dot(lr.astype(BF16), wg_ref[...], preferred_element_type=F32) + bg_ref[...]
    return (jnp.minimum(x, 0.0) - jnp.log1p(jnp.exp(-jnp.abs(x)))) * (LOG2E / GATE_NORM)


def _gla_bwd_kernel(q_ref, k_ref, v_ref, og_ref, lr_ref, of_ref, wg_ref, bg_ref, gn_ref, o_ref, s_scr):
    c = GLA_CHUNK

    @pl.when(pl.program_id(1) == 0)
    def _():
        s_scr[...] = jnp.zeros(s_scr.shape, F32)

    g = _log_gate(lr_ref[0], wg_ref, bg_ref)
    row = lax.broadcasted_iota(jnp.int32, g.shape, 0)
    suf = g
    sh = 1
    while sh < c:
        suf = suf + jnp.where(row < c - sh, pltpu.roll(suf, c - sh, 0), 0.0)
        sh *= 2
    tot = suf[0:1, :]
    for h in range(N_HEADS_B):
        sl = slice(h * KEY_DIM_B, (h + 1) * KEY_DIM_B)
        vs = slice(h * V_DIM_B, (h + 1) * V_DIM_B)
        q = q_ref[0, :, sl].astype(F32) * KEY_DIM_B ** -0.5
        k = k_ref[0, :, sl].astype(F32)
        qd = (q * jnp.exp2(suf[:, sl])).astype(BF16)
        kd = k * jnp.exp2(tot[:, sl] - suf[:, sl])
        st = s_scr[h]
        o = jnp.dot(qd, st.astype(BF16), preferred_element_type=F32) + of_ref[0, :, vs]
        dec = jnp.broadcast_to(jnp.exp2(tot[:, sl]), (c, KEY_DIM_B)).T[:, 0:1]
        s_scr[h] = st * dec + jnp.dot(kd.T.astype(BF16), v_ref[0, :, vs], preferred_element_type=F32)
        ms = jnp.mean(o * o, axis=-1, keepdims=True)
        og = og_ref[0, :, vs].astype(F32)
        o_ref[0, :, vs] = (o * lax.rsqrt(ms + EPS) * gn_ref[...] * (og / (1.0 + jnp.exp(-og)))
                           ).astype(o_ref.dtype)


def _gla_chunk_scores(q, k, gf, gb):
    c = GLA_CHUNK
    ri = lax.broadcasted_iota(jnp.int32, (c, c), 0)
    ci = lax.broadcasted_iota(jnp.int32, (c, c), 1)
    xr = jnp.bitwise_xor(ri, ci)
    nt = (((1,), (1,)), ((), ()))
    pre, tf, suf, tb = gf, gf, gb, gb
    a = jnp.where(ri == ci, 2.0 * lax.dot_general(q.astype(BF16), k.astype(BF16), nt,
                                                  preferred_element_type=F32), 0.0)

    def level_scores(a, blk, q_arg, k_arg):
        r = lax.dot_general((q * jnp.exp2(q_arg)).astype(BF16), (k * jnp.exp2(k_arg)).astype(BF16),
                            nt, preferred_element_type=F32)
        return jnp.where(jnp.logical_and(xr >= blk, xr < 2 * blk), r, a)

    blk = 1
    while blk < SUBLANES:
        hi = jnp.bitwise_and(ri, blk) != 0
        a = level_scores(a, blk, jnp.where(hi, pre, suf), jnp.where(hi, tb - suf, tf - pre))
        tf_dn, tf_up = pltpu.roll(tf, blk, 0), pltpu.roll(tf, c - blk, 0)
        tb_dn, tb_up = pltpu.roll(tb, blk, 0), pltpu.roll(tb, c - blk, 0)
        pre = pre + jnp.where(hi, tf_dn, 0.0)
        suf = suf + jnp.where(hi, 0.0, tb_up)
        tf = tf + jnp.where(hi, tf_dn, tf_up)
        tb = tb + jnp.where(hi, tb_dn, tb_up)
        blk *= 2
    ns = c // SUBLANES
    slabs = lambda x: [x[SUBLANES * r:SUBLANES * (r + 1)] for r in range(ns)]
    pre, suf, tf, tb = slabs(pre), slabs(suf), slabs(tf), slabs(tb)
    while blk < c:
        g = blk // SUBLANES
        up = [bool(r & g) for r in range(ns)]
        q_arg = [pre[r] if up[r] else suf[r] for r in range(ns)]
        k_arg = [tb[r] - suf[r] if up[r] else tf[r] - pre[r] for r in range(ns)]
        a = level_scores(a, blk, jnp.concatenate(q_arg, axis=0), jnp.concatenate(k_arg, axis=0))
        pre = [pre[r] + tf[r - g] if up[r] else pre[r] for r in range(ns)]
        suf = [suf[r] if up[r] else suf[r] + tb[r + g] for r in range(ns)]
        tf_pair = {r: tf[r] + tf[r + g] for r in range(ns) if not up[r]}
        tb_pair = {r: tb[r] + tb[r + g] for r in range(ns) if not up[r]}
        tf = [tf_pair[r - g] if up[r] else tf_pair[r] for r in range(ns)]
        tb = [tb_pair[r - g] if up[r] else tb_pair[r] for r in range(ns)]
        blk *= 2
    return a, jnp.concatenate(pre, axis=0), jnp.concatenate(tf, axis=0)


def _gla_fwd_chunk(q, k, v, gf, gb, s_scr, h):
    a, pre, tf = _gla_chunk_scores(q, k, gf, gb)
    st = s_scr[h]
    o = (jnp.dot(a.astype(BF16), v, preferred_element_type=F32)
         + jnp.dot((q * jnp.exp2(pre)).astype(BF16), st.astype(BF16), preferred_element_type=F32))
    kd = k * jnp.exp2(tf - pre)
    s_scr[h] = st * jnp.exp2(tf).T[:, 0:1] + jnp.dot(kd.T.astype(BF16), v, preferred_element_type=F32)
    return o


def _gla_fwd_kernel(q_ref, k_ref, v_ref, lr_ref, wgf_ref, bgf_ref, wgb_ref, bgb_ref, o_ref, s_scr):
    @pl.when(pl.program_id(1) == 0)
    def _():
        s_scr[...] = jnp.zeros(s_scr.shape, F32)

    lr = lr_ref[0]
    gf = _log_gate(lr, wgf_ref, bgf_ref)
    gb = _log_gate(lr, wgb_ref, bgb_ref)
    for h in range(N_HEADS_B):
        sl = slice(h * KEY_DIM_B, (h + 1) * KEY_DIM_B)
        vs = slice(h * V_DIM_B, (h + 1) * V_DIM_B)
        q = q_ref[0, :, sl].astype(F32) * KEY_DIM_B ** -0.5
        k = k_ref[0, :, sl].astype(F32)
        o_ref[0, :, vs] = _gla_fwd_chunk(q, k, v_ref[0, :, vs], gf[:, sl], gb[:, sl], s_scr, h)


def _gla(main3, lr3, wgf, bgf, wgb, bgb, gn):
    b, s, _ = main3.shape
    c = GLA_CHUNK
    nc = s // c
    hk = N_HEADS_B * KEY_DIM_B
    state = pltpu.VMEM((N_HEADS_B, KEY_DIM_B, V_DIM_B), F32)
    full = lambda shape: pl.BlockSpec(shape, lambda bi, i: (0,) * len(shape))
    fwd = lambda w, j: pl.BlockSpec((1, c, w), lambda bi, i: (bi, i, j))
    of3 = pl.pallas_call(
        _gla_fwd_kernel,
        grid=(b, nc),
        in_specs=[fwd(hk, 0), fwd(hk, 1), fwd(D_MODEL, 1), fwd(LR_W, 0),
                  full((LR_W, hk)), full((1, hk)), full((LR_W, hk)), full((1, hk))],
        out_specs=fwd(D_MODEL, 0),
        out_shape=jax.ShapeDtypeStruct((b, s, D_MODEL), F32),
        scratch_shapes=[state],
        compiler_params=_cparams(("parallel", "arbitrary")),
        name="gla_fwd",
    )(main3, main3, main3, lr3, wgf, bgf, wgb, bgb)

    rev = lambda i: nc - 1 - i
    blk = lambda w, j: pl.BlockSpec((1, c, w), lambda bi, i: (bi, rev(i), j))
    return pl.pallas_call(
        _gla_bwd_kernel,
        grid=(b, nc),
        in_specs=[blk(hk, 0), blk(hk, 1), blk(D_MODEL, 1), blk(D_MODEL, 2), blk(LR_W, 0), blk(D_MODEL, 0),
                  full((LR_W, hk)), full((1, hk)), full((1, V_DIM_B))],
        out_specs=blk(D_MODEL, 0),
        out_shape=jax.ShapeDtypeStruct((b, s, D_MODEL), BF16),
        scratch_shapes=[state],
        compiler_params=_cparams(("parallel", "arbitrary")),
        name="gla_bwd",
    )(main3, main3, main3, main3, lr3, of3, wgb, bgb, gn)


def _tail_kernel(x_ref, oa_ref, ob_ref, ga_ref, gb_ref, mod_ref, g2_ref, wa_ref, wb_ref, wo_ref,
                 wu_ref, wd_ref, o_ref):
    ya = jnp.dot(oa_ref[...], wa_ref[...], preferred_element_type=F32)
    yb = jnp.dot(ob_ref[...], wb_ref[...], preferred_element_type=F32)
    sig = lambda r: 1.0 / (1.0 + jnp.exp(-r[...].astype(F32)))
    merged = sig(ga_ref) * ya + sig(gb_ref) * yb
    x1 = x_ref[...] + mod_ref[0, 2:3, :] * jnp.dot(merged.astype(BF16), wo_ref[...],
                                                   preferred_element_type=F32)
    h = _modulated_norm(x1, g2_ref[...], mod_ref[0, 3:4, :], mod_ref[0, 4:5, :]).astype(BF16)
    acc = jnp.zeros(x1.shape, F32)
    for j in range(D_FF // D_MODEL):
        cs = slice(j * D_MODEL, (j + 1) * D_MODEL)
        u = jnp.maximum(jnp.dot(h, wu_ref[:, cs], preferred_element_type=F32), 0.0)
        acc = acc + jnp.dot((u * u).astype(BF16), wd_ref[cs, :], preferred_element_type=F32)
    o_ref[...] = x1 + mod_ref[0, 5:6, :] * acc


def _tail(x2, oa2, ob2, main2, mod8, g2, wa, wb, wo, wu, wd, seq):
    t = x2.shape[0]
    tm = ROW_TILE
    row = lambda j: pl.BlockSpec((tm, D_MODEL), lambda i: (i, j))
    once = lambda shape: pl.BlockSpec(shape, lambda i: (0,) * len(shape), pipeline_mode=pl.Buffered(1))
    return pl.pallas_call(
        _tail_kernel,
        grid=(t // tm,),
        in_specs=[row(0), row(0), row(0), row(3), row(4),
                  pl.BlockSpec((1, 8, D_MODEL), lambda i: ((i * tm) // seq, 0, 0)),
                  once((1, D_MODEL)),
                  once((D_MODEL, D_MODEL)), once((D_MODEL, D_MODEL)), once((D_MODEL, D_MODEL)),
                  once((D_MODEL, D_FF)), once((D_FF, D_MODEL))],
        out_specs=row(0),
        out_shape=jax.ShapeDtypeStruct((t, D_MODEL), F32),
        compiler_params=_cparams(("parallel",)),
        name="tail",
    )(x2, oa2, ob2, main2, main2, mod8, g2, wa, wb, wo, wu, wd)


def _split_w_in(w_in):
    o = np.cumsum([0, 1024, 1024, 1024, 512, 512, 1024, 1024, GATE_RANK, GATE_RANK, 1024, 1024])
    main = jnp.concatenate([w_in[:, o[0]:o[7]], w_in[:, o[9]:o[11]]], axis=1)
    lr = jnp.pad(w_in[:, o[7]:o[9]], ((0, 0), (0, LR_W - 2 * GATE_RANK)))
    return main.astype(BF16), lr.astype(BF16)


def _pad_gate_w(w_gate, row0):
    return jnp.pad(w_gate, ((row0, LR_W - GATE_RANK - row0), (0, 0))).astype(BF16)


def _layer(x, mod8, layer_idx, p):
    b, s, _ = x.shape
    t = b * s
    lambda_init = 0.8 - 0.6 * math.exp(-0.3 * layer_idx)
    x2 = x.reshape(t, D_MODEL)
    row = lambda v: v.reshape(1, -1).astype(F32)

    qg2, kg2 = row(jnp.tile(p["q_norm_g"], 2)), row(jnp.tile(p["k_norm_g"], 2))
    qt, kn2, vt, main2, lr2 = _proj(x2, mod8, row(p["norm1_g"]), p["w_main"], p["w_lr"], qg2, kg2, s)
    kn = kn2.reshape(b, s, D_MODEL)
    main3 = main2.reshape(b, s, N_REST)
    lr3 = lr2.reshape(b, s, LR_W)
    lamv = jnp.zeros((8, HEAD_W_A), F32).at[0:4, :HEAD_DIM_A].set(
        jnp.stack([p["lam_q1"], p["lam_k1"], p["lam_q2"], p["lam_k2"]]).astype(F32))
    bias_t, range_ok = _bias_tiles(p["rel_bias"], qg2, kg2)
    oa = _attn(range_ok, qt, kn, vt, bias_t, lamv, row(p["subln_g"]), lambda_init)

    ob = _gla(main3, lr3, p["wgf"], row(p["b_gate_f"]), p["wgb"], row(p["b_gate_b"]), row(p["gla_norm_g"]))

    y = _tail(x2, oa.reshape(t, D_MODEL), ob.reshape(t, D_MODEL), main2, mod8, row(p["norm2_g"]),
              p["w_branch_a"], p["w_branch_b"], p["w_out"], p["w_up"], p["w_down"], s)
    return y.reshape(b, s, D_MODEL)


def kernel(x_prompt, x_sample, c_prompt, c_sample, rel_bias, w_ada, b_ada, norm1_g, w_in, q_norm_g, k_norm_g, lam_q1, lam_k1, lam_q2, lam_k2, subln_g, w_gate_f, b_gate_f, w_gate_b, b_gate_b, gla_norm_g, w_branch_a, w_branch_b, w_out, norm2_g, w_up, w_down):
    depth = w_in.shape[0]
    nb_p, nb_s = c_prompt.shape[0], c_sample.shape[0]
    rows = -(-(nb_p + nb_s) // 8) * 8
    c_all = jnp.pad(jnp.concatenate([c_prompt, c_sample], axis=0), ((0, rows - nb_p - nb_s), (0, 0)))

    xp, xs = x_prompt, x_sample
    for l in range(depth):
        mod = _ada(c_all, w_ada[l], b_ada[l]).reshape(rows, 6, D_MODEL)
        mod8 = jnp.pad(mod, ((0, 0), (0, 2), (0, 0)))
        w_main, w_lr = _split_w_in(w_in[l])
        p = dict(rel_bias=rel_bias.astype(F32), norm1_g=norm1_g[l], w_main=w_main, w_lr=w_lr,
                 q_norm_g=q_norm_g[l], k_norm_g=k_norm_g[l], lam_q1=lam_q1[l], lam_k1=lam_k1[l],
                 lam_q2=lam_q2[l], lam_k2=lam_k2[l], subln_g=subln_g[l],
                 wgf=_pad_gate_w(w_gate_f[l], 0), b_gate_f=b_gate_f[l],
                 wgb=_pad_gate_w(w_gate_b[l], GATE_RANK), b_gate_b=b_gate_b[l],
                 gla_norm_g=gla_norm_g[l], w_branch_a=w_branch_a[l].astype(BF16),
                 w_branch_b=w_branch_b[l].astype(BF16), w_out=w_out[l].astype(BF16),
                 norm2_g=norm2_g[l], w_up=w_up[l].astype(BF16), w_down=w_down[l].astype(BF16))
        xp = _layer(xp, mod8[:nb_p], l, p)
        xs = _layer(xs, mod8[nb_p:nb_p + nb_s], l, p)
    return (xp, xs)
```

```python
import functools
import math

import jax
import jax.numpy as jnp
import numpy as np
from jax import lax
from jax.experimental import pallas as pl
from jax.experimental.pallas import tpu as pltpu

F32 = jnp.float32
BF16 = jnp.bfloat16

D_MODEL = 1024
HEAD_DIM_A = 64
N_HEADS_A = 8
HEAD_W_A = 2 * HEAD_DIM_A
N_HEADS_B = 4
KEY_DIM_B = 128
V_DIM_B = 256
GATE_RANK = 16
GATE_NORM = 16.0
N_BUCKETS = 32
MAX_DISTANCE = 128
D_FF = 4 * D_MODEL
EPS = 1e-6
LOG2E = math.log2(math.e)

N_MAIN = 8 * D_MODEL
N_REST = 5 * D_MODEL
LR_W = 128

VMEM_LIMIT = 56 * 1024 * 1024
ROW_TILE = 512
ATT_TILE = 256
ATT_BLOCK = 32
NEAR_TILES = 5
EXP2_RANGE = 100.0
GLA_CHUNK = 128
SUBLANES = 8


def _cparams(sem):
    return pltpu.CompilerParams(dimension_semantics=sem, vmem_limit_bytes=VMEM_LIMIT)


def _ada_kernel(c_ref, w_ref, b_ref, o_ref):
    c = c_ref[...]
    a = c / (1.0 + jnp.exp(-c))
    o_ref[...] = jnp.dot(a, w_ref[...], preferred_element_type=F32,
                         precision=lax.Precision.HIGHEST) + b_ref[...]


def _ada(c_all, w_ada, b_ada):
    r = c_all.shape[0]
    tn = 1024
    return pl.pallas_call(
        _ada_kernel,
        grid=(6 * D_MODEL // tn,),
        in_specs=[pl.BlockSpec((r, D_MODEL), lambda n: (0, 0)),
                  pl.BlockSpec((D_MODEL, tn), lambda n: (0, n)),
                  pl.BlockSpec((1, tn), lambda n: (0, n))],
        out_specs=pl.BlockSpec((r, tn), lambda n: (0, n)),
        out_shape=jax.ShapeDtypeStruct((r, 6 * D_MODEL), F32),
        compiler_params=_cparams(("parallel",)),
        name="ada",
    )(c_all, w_ada, b_ada.reshape(1, -1))


def _modulated_norm(x, g, shift, scale):
    ms = jnp.mean(x * x, axis=-1, keepdims=True)
    return x * lax.rsqrt(ms + EPS) * g * (1.0 + scale) + shift


def _pair_rms(x, g):
    lane = lax.broadcasted_iota(jnp.int32, x.shape, 1)
    x2 = x * x
    s_all = jnp.sum(x2, axis=-1, keepdims=True)
    s_lo = jnp.sum(jnp.where(lane < HEAD_DIM_A, x2, 0.0), axis=-1, keepdims=True)
    ss = jnp.where(lane < HEAD_DIM_A, s_lo, s_all - s_lo)
    return x * lax.rsqrt(ss * (1.0 / HEAD_DIM_A) + EPS) * g


def _proj_kernel(x_ref, mod_ref, g_ref, w_ref, wlr_ref, qg_ref, kg_ref,
                 qt_ref, kn_ref, vt_ref, o_ref, lr_ref):
    h = _modulated_norm(x_ref[...], g_ref[...], mod_ref[0, 0:1, :], mod_ref[0, 1:2, :]).astype(BF16)
    lr_ref[...] = jnp.dot(h, wlr_ref[...], preferred_element_type=F32)
    piece = lambda j: jnp.dot(h, w_ref[:, j * D_MODEL:(j + 1) * D_MODEL], preferred_element_type=F32)
    heads = [slice(i * HEAD_W_A, (i + 1) * HEAD_W_A) for i in range(N_HEADS_A)]

    qa = piece(0)
    for i, sl in enumerate(heads):
        qn = _pair_rms(qa[:, sl], qg_ref[...]) * (HEAD_DIM_A ** -0.5 * LOG2E)
        qt_ref[0, i] = qn.T.astype(BF16)
    ka = piece(1)
    for sl in heads:
        kn_ref[:, sl] = _pair_rms(ka[:, sl], kg_ref[...]).astype(BF16)
    va = piece(2)
    for i, sl in enumerate(heads):
        vt_ref[0, i] = va[:, sl].T.astype(BF16)
    for j in range(3, N_MAIN // D_MODEL):
        o_ref[:, (j - 3) * D_MODEL:(j - 2) * D_MODEL] = piece(j).astype(o_ref.dtype)


def _proj(x2, mod8, g1, w_main, w_lr, qg2, kg2, seq):
    t = x2.shape[0]
    tm = ROW_TILE
    spt = seq // tm
    once = pl.Buffered(1)
    const = lambda shape, **kw: pl.BlockSpec(shape, lambda i: (0,) * len(shape), **kw)
    tspec = pl.BlockSpec((1, N_HEADS_A, HEAD_W_A, tm), lambda i: (i // spt, 0, 0, i % spt))
    tshape = jax.ShapeDtypeStruct((t // seq, N_HEADS_A, HEAD_W_A, seq), BF16)
    return pl.pallas_call(
        _proj_kernel,
        grid=(t // tm,),
        in_specs=[pl.BlockSpec((tm, D_MODEL), lambda i: (i, 0)),
                  pl.BlockSpec((1, 8, D_MODEL), lambda i: (i // spt, 0, 0)),
                  const((1, D_MODEL)),
                  const((D_MODEL, N_MAIN), pipeline_mode=once),
                  const((D_MODEL, LR_W), pipeline_mode=once),
                  const((1, HEAD_W_A)), const((1, HEAD_W_A))],
        out_specs=[tspec,
                   pl.BlockSpec((tm, D_MODEL), lambda i: (i, 0)),
                   tspec,
                   pl.BlockSpec((tm, N_REST), lambda i: (i, 0)),
                   pl.BlockSpec((tm, LR_W), lambda i: (i, 0))],
        out_shape=[tshape,
                   jax.ShapeDtypeStruct((t, D_MODEL), BF16),
                   tshape,
                   jax.ShapeDtypeStruct((t, N_REST), BF16),
                   jax.ShapeDtypeStruct((t, LR_W), F32)],
        compiler_params=_cparams(("parallel",)),
        name="proj",
    )(x2, mod8, g1, w_main, w_lr, qg2, kg2)


def _t5_bucket(rel):
    nb = N_BUCKETS // 2
    max_exact = nb // 2
    ret = (rel > 0).astype(jnp.int32) * nb
    n = jnp.abs(rel)
    nf = jnp.maximum(n, 1).astype(F32)
    large = max_exact + (jnp.log(nf / max_exact) / math.log(MAX_DISTANCE / max_exact)
                         * (nb - max_exact)).astype(jnp.int32)
    large = jnp.minimum(large, nb - 1)
    return ret + jnp.where(n < max_exact, n, large)


def _bias_kernel(tbl_ref, bkt_ref, qg_ref, kg_ref, o_ref, ok_ref):
    h = pl.program_id(0)
    bkt = bkt_ref[...]
    acc = jnp.zeros(bkt.shape, F32)
    babs = jnp.float32(0.0)
    for i in range(N_BUCKETS):
        acc = jnp.where(bkt == i, tbl_ref[i, h], acc)
        babs = jnp.maximum(babs, jnp.abs(tbl_ref[i, h]))
    o_ref[0] = acc * LOG2E
    gmax = lambda r: jnp.max(jnp.abs(r[...]), axis=1, keepdims=True)
    bound = gmax(qg_ref) * gmax(kg_ref) * (1.02 * HEAD_DIM_A * HEAD_DIM_A ** -0.5 * LOG2E) + babs * LOG2E
    ok_ref[0] = jnp.broadcast_to((bound <= EXP2_RANGE).astype(jnp.int32), ok_ref.shape[1:])


def _bias_tiles(rel_bias, qg2, kg2):
    t = ATT_TILE
    kk = jnp.arange(t, dtype=jnp.int32)[:, None]
    qq = jnp.arange(t, dtype=jnp.int32)[None, :]
    rel = jnp.stack([(d - NEAR_TILES // 2) * t + kk - qq for d in range(NEAR_TILES)])
    gspec = pl.BlockSpec((1, HEAD_W_A), lambda h: (0, 0))
    tiles, ok = pl.pallas_call(
        _bias_kernel,
        grid=(N_HEADS_A,),
        in_specs=[pl.BlockSpec(memory_space=pltpu.SMEM),
                  pl.BlockSpec((NEAR_TILES, t, t), lambda h: (0, 0, 0)), gspec, gspec],
        out_specs=[pl.BlockSpec((1, NEAR_TILES, t, t), lambda h: (h, 0, 0, 0)),
                   pl.BlockSpec((1, 8, HEAD_W_A), lambda h: (h, 0, 0))],
        out_shape=[jax.ShapeDtypeStruct((N_HEADS_A, NEAR_TILES, t, t), F32),
                   jax.ShapeDtypeStruct((N_HEADS_A, 8, HEAD_W_A), jnp.int32)],
        compiler_params=_cparams(("parallel",)),
        name="bias",
    )(rel_bias, _t5_bucket(rel), qg2, kg2)
    return tiles, ok[:, 0, 0]


def _attn_kernel(ok_ref, qt_ref, k_ref, vt_ref, bias_ref, lamv_ref, subg_ref, o_ref,
                 w_scr, m_scr, l_scr, acc_scr, l8_scr, *, nk, qsub, lambda_init):
    t = ATT_TILE
    blk = min(ATT_BLOCK, nk)
    bt = blk * t
    h = pl.program_id(1)
    tiles = [(u, pl.program_id(2) * qsub + u) for u in range(qsub)]

    z = jnp.zeros((HEAD_DIM_A, t), BF16)
    for u, _ in tiles:
        q = qt_ref[0, 0, :, u * t:(u + 1) * t]
        w_scr[u, :, :t] = jnp.concatenate([q[:HEAD_DIM_A], z], axis=0)
        w_scr[u, :, t:] = jnp.concatenate([z, q[HEAD_DIM_A:]], axis=0)

    def chunk_bias(kj, qi):
        b = bias_ref[0, jnp.clip(kj - qi, -2, 2) + 2]
        return jnp.concatenate([b, b], axis=1)

    @pl.when(ok_ref[h] != 0)
    def _():
        l8_scr[...] = jnp.zeros(l8_scr.shape, F32)
        acc_scr[...] = jnp.zeros(acc_scr.shape, F32)

        def block(j, carry):
            k0 = pl.multiple_of(j * bt, bt)
            for u, qi in tiles:
                s = jnp.dot(k_ref[0, pl.ds(k0, bt), :], w_scr[u], preferred_element_type=F32)
                p = [jnp.exp2(s[c * t:(c + 1) * t] + chunk_bias(j * blk + c, qi)) for c in range(blk)]
                p = jnp.concatenate(p, axis=0)
                l8_scr[u] += jnp.sum(p.reshape(bt // 8, 8, 2 * t), axis=0)
                acc_scr[u] += jnp.dot(vt_ref[0, 0, :, pl.ds(k0, bt)], p.astype(BF16),
                                      preferred_element_type=F32)
            return carry

        lax.fori_loop(0, nk // blk, block, 0)
        l_scr[...] = jnp.sum(l8_scr[...], axis=1, keepdims=True)

    @pl.when(ok_ref[h] == 0)
    def _():
        m_scr[...] = jnp.full(m_scr.shape, -jnp.inf, F32)
        l_scr[...] = jnp.zeros(l_scr.shape, F32)
        acc_scr[...] = jnp.zeros(acc_scr.shape, F32)

        def online_step(kj, carry):
            k0 = pl.multiple_of(kj * t, t)
            for u, qi in tiles:
                s = jnp.dot(k_ref[0, pl.ds(k0, t), :], w_scr[u], preferred_element_type=F32)
                s = s + chunk_bias(kj, qi)
                m_old = m_scr[u]
                m_new = jnp.maximum(m_old, jnp.max(s, axis=0, keepdims=True))
                alpha = jnp.exp2(m_old - m_new)
                p = jnp.exp2(s - m_new)
                l_scr[u] = alpha * l_scr[u] + jnp.sum(p, axis=0, keepdims=True)
                pv = jnp.dot(vt_ref[0, 0, :, pl.ds(k0, t)], p.astype(BF16), preferred_element_type=F32)
                acc_scr[u] = alpha * acc_scr[u] + pv
                m_scr[u] = m_new
            return carry

        lax.fori_loop(0, nk, online_step, 0)

    lv = lamv_ref[...]
    lam = (jnp.exp(jnp.sum(lv[0:1] * lv[1:2], axis=-1, keepdims=True))
           - jnp.exp(jnp.sum(lv[2:3] * lv[3:4], axis=-1, keepdims=True)) + lambda_init)
    for u, _ in tiles:
        acc = acc_scr[u]
        inv = 1.0 / l_scr[u]
        o = acc[:, :t] * inv[:, :t] - lam * (acc[:, t:] * inv[:, t:])
        ms = jnp.mean(o * o, axis=0, keepdims=True)
        y = (o * lax.rsqrt(ms + EPS)).T
        o_ref[0, u * t:(u + 1) * t, :] = (y * subg_ref[...] * (1.0 - lambda_init)).astype(o_ref.dtype)


def _attn(range_ok, qt, kn, vt, bias_t, lamv, subg, lambda_init):
    b, s, _ = kn.shape
    t = ATT_TILE
    nk = s // t
    assert nk % min(ATT_BLOCK, nk) == 0 and t >= MAX_DISTANCE
    qsub = 8 if nk <= ATT_BLOCK and nk % 8 == 0 else 2
    kern = functools.partial(_attn_kernel, nk=nk, qsub=qsub, lambda_init=lambda_init)
    const = lambda shape: pl.BlockSpec(shape, lambda bi, h, qi: (0,) * len(shape))
    tq = qsub * t
    return pl.pallas_call(
        kern,
        grid=(b, N_HEADS_A, s // tq),
        in_specs=[pl.BlockSpec(memory_space=pltpu.SMEM),
                  pl.BlockSpec((1, 1, HEAD_W_A, tq), lambda bi, h, qi: (bi, h, 0, qi)),
                  pl.BlockSpec((1, s, HEAD_W_A), lambda bi, h, qi: (bi, 0, h)),
                  pl.BlockSpec((1, 1, HEAD_W_A, s), lambda bi, h, qi: (bi, h, 0, 0)),
                  pl.BlockSpec((1, NEAR_TILES, t, t), lambda bi, h, qi: (h, 0, 0, 0)),
                  const((8, HEAD_W_A)), const((1, HEAD_W_A))],
        out_specs=pl.BlockSpec((1, tq, HEAD_W_A), lambda bi, h, qi: (bi, qi, h)),
        out_shape=jax.ShapeDtypeStruct((b, s, D_MODEL), BF16),
        scratch_shapes=[pltpu.VMEM((qsub, HEAD_W_A, 2 * t), BF16),
                        pltpu.VMEM((qsub, 1, 2 * t), F32),
                        pltpu.VMEM((qsub, 1, 2 * t), F32),
                        pltpu.VMEM((qsub, HEAD_W_A, 2 * t), F32),
                        pltpu.VMEM((qsub, 8, 2 * t), F32)],
        compiler_params=_cparams(("parallel", "parallel", "parallel")),
        name="attn",
    )(range_ok, qt, kn, vt, bias_t, lamv, subg)


def _log_gate(lr, wg_ref, bg_ref):
    x = jnp.dot(lr.astype(BF16), wg_ref[...], preferred_element_type=F32) + bg_ref[...]
    return (jnp.minimum(x, 0.0) - jnp.log1p(jnp.exp(-jnp.abs(x)))) * (LOG2E / GATE_NORM)


def _gla_bwd_kernel(qd_ref, kd_ref, tot_ref, v_ref, og_ref, of_ref, gn_ref, o_ref, s_scr):
    c = GLA_CHUNK

    @pl.when(pl.program_id(1) == 0)
    def _():
        s_scr[...] = jnp.zeros(s_scr.shape, F32)

    for h in range(N_HEADS_B):
        sl = slice(h * KEY_DIM_B, (h + 1) * KEY_DIM_B)
        vs = slice(h * V_DIM_B, (h + 1) * V_DIM_B)
        st = s_scr[h]
        o = jnp.dot(qd_ref[0, :, sl], st.astype(BF16), preferred_element_type=F32) + of_ref[0, :, vs]
        dec = jnp.broadcast_to(jnp.exp2(tot_ref[0, 0, 0:1, sl]), (c, KEY_DIM_B)).T[:, 0:1]
        s_scr[h] = st * dec + jnp.dot(kd_ref[0, :, sl].astype(F32).T.astype(BF16), v_ref[0, :, vs],
                                      preferred_element_type=F32)
        ms = jnp.mean(o * o, axis=-1, keepdims=True)
        og = og_ref[0, :, vs].astype(F32)
        o_ref[0, :, vs] = (o * lax.rsqrt(ms + EPS) * gn_ref[...] * (og / (1.0 + jnp.exp(-og)))
                           ).astype(o_ref.dtype)


def _gla_chunk_scores(q, k, gf, gb):
    c = GLA_CHUNK
    ri = lax.broadcasted_iota(jnp.int32, (c, c), 0)
    ci = lax.broadcasted_iota(jnp.int32, (c, c), 1)
    xr = jnp.bitwise_xor(ri, ci)
    nt = (((1,), (1,)), ((), ()))
    pre, tf, suf, tb = gf, gf, gb, gb
    a = jnp.where(ri == ci, 2.0 * lax.dot_general(q.astype(BF16), k.astype(BF16), nt,
                                                  preferred_element_type=F32), 0.0)

    def level_scores(a, blk, q_arg, k_arg):
        r = lax.dot_general((q * jnp.exp2(q_arg)).astype(BF16), (k * jnp.exp2(k_arg)).astype(BF16),
                            nt, preferred_element_type=F32)
        return jnp.where(jnp.logical_and(xr >= blk, xr < 2 * blk), r, a)

    blk = 1
    while blk < SUBLANES:
        hi = jnp.bitwise_and(ri, blk) != 0
        a = level_scores(a, blk, jnp.where(hi, pre, suf), jnp.where(hi, tb - suf, tf - pre))
        tf_dn, tf_up = pltpu.roll(tf, blk, 0), pltpu.roll(tf, c - blk, 0)
        tb_dn, tb_up = pltpu.roll(tb, blk, 0), pltpu.roll(tb, c - blk, 0)
        pre = pre + jnp.where(hi, tf_dn, 0.0)
        suf = suf + jnp.where(hi, 0.0, tb_up)
        tf = tf + jnp.where(hi, tf_dn, tf_up)
        tb = tb + jnp.where(hi, tb_dn, tb_up)
        blk *= 2
    ns = c // SUBLANES
    slabs = lambda x: [x[SUBLANES * r:SUBLANES * (r + 1)] for r in range(ns)]
    pre, suf, tf, tb = slabs(pre), slabs(suf), slabs(tf), slabs(tb)
    while blk < c:
        g = blk // SUBLANES
        up = [bool(r & g) for r in range(ns)]
        q_arg = [pre[r] if up[r] else suf[r] for r in range(ns)]
        k_arg = [tb[r] - suf[r] if up[r] else tf[r] - pre[r] for r in range(ns)]
        a = level_scores(a, blk, jnp.concatenate(q_arg, axis=0), jnp.concatenate(k_arg, axis=0))
        pre = [pre[r] + tf[r - g] if up[r] else pre[r] for r in range(ns)]
        suf = [suf[r] if up[r] else suf[r] + tb[r + g] for r in range(ns)]
        tf_pair = {r: tf[r] + tf[r + g] for r in range(ns) if not up[r]}
        tb_pair = {r: tb[r] + tb[r + g] for r in range(ns) if not up[r]}
        tf = [tf_pair[r - g] if up[r] else tf_pair[r] for r in range(ns)]
        tb = [tb_pair[r - g] if up[r] else tb_pair[r] for r in range(ns)]
        blk *= 2
    cat = lambda x: jnp.concatenate(x, axis=0)
    return a, cat(pre), cat(tf), cat(suf), cat(tb)


def _gla_fwd_chunk(q, k, v, gf, gb, s_scr, h):
    a, pre, tf, suf, tb = _gla_chunk_scores(q, k, gf, gb)
    st = s_scr[h]
    o = (jnp.dot(a.astype(BF16), v, preferred_element_type=F32)
         + jnp.dot((q * jnp.exp2(pre)).astype(BF16), st.astype(BF16), preferred_element_type=F32))
    kd = k * jnp.exp2(tf - pre)
    s_scr[h] = st * jnp.exp2(tf).T[:, 0:1] + jnp.dot(kd.T.astype(BF16), v, preferred_element_type=F32)
    return o, (q * jnp.exp2(suf)).astype(BF16), (k * jnp.exp2(tb - suf)).astype(BF16), tb[0:SUBLANES]


def _gla_fwd_kernel(q_ref, k_ref, v_ref, lr_ref, wgf_ref, bgf_ref, wgb_ref, bgb_ref,
                    o_ref, qd_ref, kd_ref, tot_ref, s_scr):
    @pl.when(pl.program_id(1) == 0)
    def _():
        s_scr[...] = jnp.zeros(s_scr.shape, F32)

    lr = lr_ref[0]
    gf = _log_gate(lr, wgf_ref, bgf_ref)
    gb = _log_gate(lr, wgb_ref, bgb_ref)
    for h in range(N_HEADS_B):
        sl = slice(h * KEY_DIM_B, (h + 1) * KEY_DIM_B)
        vs = slice(h * V_DIM_B, (h + 1) * V_DIM_B)
        q = q_ref[0, :, sl].astype(F32) * KEY_DIM_B ** -0.5
        k = k_ref[0, :, sl].astype(F32)
        o_ref[0, :, vs], qd_ref[0, :, sl], kd_ref[0, :, sl], tot_ref[0, 0, :, sl] = _gla_fwd_chunk(
            q, k, v_ref[0, :, vs], gf[:, sl], gb[:, sl], s_scr, h)


def _gla(main3, lr3, wgf, bgf, wgb, bgb, gn):
    b, s, _ = main3.shape
    c = GLA_CHUNK
    nc = s // c
    hk = N_HEADS_B * KEY_DIM_B
    state = pltpu.VMEM((N_HEADS_B, KEY_DIM_B, V_DIM_B), F32)
    full = lambda shape: pl.BlockSpec(shape, lambda bi, i: (0,) * len(shape))
    fwd = lambda w, j: pl.BlockSpec((1, c, w), lambda bi, i: (bi, i, j))
    of3, qd, kd, tot = pl.pallas_call(
        _gla_fwd_kernel,
        grid=(b, nc),
        in_specs=[fwd(hk, 0), fwd(hk, 1), fwd(D_MODEL, 1), fwd(LR_W, 0),
                  full((LR_W, hk)), full((1, hk)), full((LR_W, hk)), full((1, hk))],
        out_specs=[fwd(D_MODEL, 0), fwd(hk, 0), fwd(hk, 0),
                   pl.BlockSpec((1, 1, SUBLANES, hk), lambda bi, i: (bi, i, 0, 0))],
        out_shape=[jax.ShapeDtypeStruct((b, s, D_MODEL), F32),
                   jax.ShapeDtypeStruct((b, s, hk), BF16),
                   jax.ShapeDtypeStruct((b, s, hk), BF16),
                   jax.ShapeDtypeStruct((b, nc, SUBLANES, hk), F32)],
        scratch_shapes=[state],
        compiler_params=_cparams(("parallel", "arbitrary")),
        name="gla_fwd",
    )(main3, main3, main3, lr3, wgf, bgf, wgb, bgb)

    rev = lambda i: nc - 1 - i
    blk = lambda w, j: pl.BlockSpec((1, c, w), lambda bi, i: (bi, rev(i), j))
    return pl.pallas_call(
        _gla_bwd_kernel,
        grid=(b, nc),
        in_specs=[blk(hk, 0), blk(hk, 0),
                  pl.BlockSpec((1, 1, SUBLANES, hk), lambda bi, i: (bi, rev(i), 0, 0)),
                  blk(D_MODEL, 1), blk(D_MODEL, 2), blk(D_MODEL, 0), full((1, V_DIM_B))],
        out_specs=blk(D_MODEL, 0),
        out_shape=jax.ShapeDtypeStruct((b, s, D_MODEL), BF16),
        scratch_shapes=[state],
        compiler_params=_cparams(("parallel", "arbitrary")),
        name="gla_bwd",
    )(qd, kd, tot, main3, main3, of3, gn)


def _tail_kernel(x_ref, oa_ref, ob_ref, ga_ref, gb_ref, mod_ref, g2_ref, wa_ref, wb_ref, wo_ref,
                 wu_ref, wd_ref, o_ref):
    ya = jnp.dot(oa_ref[...], wa_ref[...], preferred_element_type=F32)
    yb = jnp.dot(ob_ref[...], wb_ref[...], preferred_element_type=F32)
    sig = lambda r: 1.0 / (1.0 + jnp.exp(-r[...].astype(F32)))
    merged = sig(ga_ref) * ya + sig(gb_ref) * yb
    x1 = x_ref[...] + mod_ref[0, 2:3, :] * jnp.dot(merged.astype(BF16), wo_ref[...],
                                                   preferred_element_type=F32)
    h = _modulated_norm(x1, g2_ref[...], mod_ref[0, 3:4, :], mod_ref[0, 4:5, :]).astype(BF16)
    acc = jnp.zeros(x1.shape, F32)
    for j in range(D_FF // D_MODEL):
        cs = slice(j * D_MODEL, (j + 1) * D_MODEL)
        u = jnp.maximum(jnp.dot(h, wu_ref[:, cs], preferred_element_type=F32), 0.0)
        acc = acc + jnp.dot((u * u).astype(BF16), wd_ref[cs, :], preferred_element_type=F32)
    o_ref[...] = x1 + mod_ref[0, 5:6, :] * acc


def _tail(x2, oa2, ob2, main2, mod8, g2, wa, wb, wo, wu, wd, seq):
    t = x2.shape[0]
    tm = ROW_TILE
    row = lambda j: pl.BlockSpec((tm, D_MODEL), lambda i: (i, j))
    once = lambda shape: pl.BlockSpec(shape, lambda i: (0,) * len(shape), pipeline_mode=pl.Buffered(1))
    return pl.pallas_call(
        _tail_kernel,
        grid=(t // tm,),
        in_specs=[row(0), row(0), row(0), row(3), row(4),
                  pl.BlockSpec((1, 8, D_MODEL), lambda i: ((i * tm) // seq, 0, 0)),
                  once((1, D_MODEL)),
                  once((D_MODEL, D_MODEL)), once((D_MODEL, D_MODEL)), once((D_MODEL, D_MODEL)),
                  once((D_MODEL, D_FF)), once((D_FF, D_MODEL))],
        out_specs=row(0),
        out_shape=jax.ShapeDtypeStruct((t, D_MODEL), F32),
        compiler_params=_cparams(("parallel",)),
        name="tail",
    )(x2, oa2, ob2, main2, main2, mod8, g2, wa, wb, wo, wu, wd)


def _split_w_in(w_in):
    o = np.cumsum([0, 1024, 1024, 1024, 512, 512, 1024, 1024, GATE_RANK, GATE_RANK, 1024, 1024])
    main = jnp.concatenate([w_in[:, o[0]:o[7]], w_in[:, o[9]:o[11]]], axis=1)
    lr = jnp.pad(w_in[:, o[7]:o[9]], ((0, 0), (0, LR_W - 2 * GATE_RANK)))
    return main.astype(BF16), lr.astype(BF16)


def _pad_gate_w(w_gate, row0):
    return jnp.pad(w_gate, ((row0, LR_W - GATE_RANK - row0), (0, 0))).astype(BF16)


def _layer(x, mod8, layer_idx, p):
    b, s, _ = x.shape
    t = b * s
    lambda_init = 0.8 - 0.6 * math.exp(-0.3 * layer_idx)
    x2 = x.reshape(t, D_MODEL)
    row = lambda v: v.reshape(1, -1).astype(F32)

    qg2, kg2 = row(jnp.tile(p["q_norm_g"], 2)), row(jnp.tile(p["k_norm_g"], 2))
    qt, kn2, vt, main2, lr2 = _proj(x2, mod8, row(p["norm1_g"]), p["w_main"], p["w_lr"], qg2, kg2, s)
    kn = kn2.reshape(b, s, D_MODEL)
    main3 = main2.reshape(b, s, N_REST)
    lr3 = lr2.reshape(b, s, LR_W)
    lamv = jnp.zeros((8, HEAD_W_A), F32).at[0:4, :HEAD_DIM_A].set(
        jnp.stack([p["lam_q1"], p["lam_k1"], p["lam_q2"], p["lam_k2"]]).astype(F32))
    bias_t, range_ok = _bias_tiles(p["rel_bias"], qg2, kg2)
    oa = _attn(range_ok, qt, kn, vt, bias_t, lamv, row(p["subln_g"]), lambda_init)

    ob = _gla(main3, lr3, p["wgf"], row(p["b_gate_f"]), p["wgb"], row(p["b_gate_b"]), row(p["gla_norm_g"]))

    y = _tail(x2, oa.reshape(t, D_MODEL), ob.reshape(t, D_MODEL), main2, mod8, row(p["norm2_g"]),
              p["w_branch_a"], p["w_branch_b"], p["w_out"], p["w_up"], p["w_down"], s)
    return y.reshape(b, s, D_MODEL)


def kernel(x_prompt, x_sample, c_prompt, c_sample, rel_bias, w_ada, b_ada, norm1_g, w_in, q_norm_g, k_norm_g, lam_q1, lam_k1, lam_q2, lam_k2, subln_g, w_gate_f, b_gate_f, w_gate_b, b_gate_b, gla_norm_g, w_branch_a, w_branch_b, w_out, norm2_g, w_up, w_down):
    depth = w_in.shape[0]
    nb_p, nb_s = c_prompt.shape[0], c_sample.shape[0]
    rows = -(-(nb_p + nb_s) // 8) * 8
    c_all = jnp.pad(jnp.concatenate([c_prompt, c_sample], axis=0), ((0, rows - nb_p - nb_s), (0, 0)))

    xp, xs = x_prompt, x_sample
    for l in range(depth):
        mod = _ada(c_all, w_ada[l], b_ada[l]).reshape(rows, 6, D_MODEL)
        mod8 = jnp.pad(mod, ((0, 0), (0, 2), (0, 0)))
        w_main, w_lr = _split_w_in(w_in[l])
        p = dict(rel_bias=rel_bias.astype(F32), norm1_g=norm1_g[l], w_main=w_main, w_lr=w_lr,
                 q_norm_g=q_norm_g[l], k_norm_g=k_norm_g[l], lam_q1=lam_q1[l], lam_k1=lam_k1[l],
                 lam_q2=lam_q2[l], lam_k2=lam_k2[l], subln_g=subln_g[l],
                 wgf=_pad_gate_w(w_gate_f[l], 0), b_gate_f=b_gate_f[l],
                 wgb=_pad_gate_w(w_gate_b[l], GATE_RANK), b_gate_b=b_gate_b[l],
                 gla_norm_g=gla_norm_g[l], w_branch_a=w_branch_a[l].astype(BF16),
                 w_branch_b=w_branch_b[l].astype(BF16), w_out=w_out[l].astype(BF16),
                 norm2_g=norm2_g[l], w_up=w_up[l].astype(BF16), w_down=w_down[l].astype(BF16))
        xp = _layer(xp, mod8[:nb_p], l, p)
        xs = _layer(xs, mod8[nb_p:nb_p + nb_s], l, p)
    return (xp, xs)
```

```python
import functools
import math

import jax
import jax.numpy as jnp
import numpy as np
from jax import lax
from jax.experimental import pallas as pl
from jax.experimental.pallas import tpu as pltpu

F32 = jnp.float32
BF16 = jnp.bfloat16

D_MODEL = 1024
HEAD_DIM_A = 64
N_HEADS_A = 8
HEAD_W_A = 2 * HEAD_DIM_A
N_HEADS_B = 4
KEY_DIM_B = 128
V_DIM_B = 256
GATE_RANK = 16
GATE_NORM = 16.0
N_BUCKETS = 32
MAX_DISTANCE = 128
D_FF = 4 * D_MODEL
EPS = 1e-6
LOG2E = math.log2(math.e)

N_MAIN = 8 * D_MODEL
N_REST = 5 * D_MODEL
LR_W = 128

VMEM_LIMIT = 56 * 1024 * 1024
ROW_TILE = 512
ATT_TILE = 256
ATT_BLOCK = 32
NEAR_TILES = 5
EXP2_RANGE = 100.0
GLA_CHUNK = 128
GLA_BWD_CHUNKS = 4
SUBLANES = 8


def _cparams(sem):
    return pltpu.CompilerParams(dimension_semantics=sem, vmem_limit_bytes=VMEM_LIMIT)


def _ada_kernel(c_ref, w_ref, b_ref, o_ref):
    c = c_ref[...]
    a = c / (1.0 + jnp.exp(-c))
    o_ref[...] = jnp.dot(a, w_ref[...], preferred_element_type=F32,
                         precision=lax.Precision.HIGHEST) + b_ref[...]


def _ada(c_all, w_ada, b_ada):
    r = c_all.shape[0]
    tn = 1024
    return pl.pallas_call(
        _ada_kernel,
        grid=(6 * D_MODEL // tn,),
        in_specs=[pl.BlockSpec((r, D_MODEL), lambda n: (0, 0)),
                  pl.BlockSpec((D_MODEL, tn), lambda n: (0, n)),
                  pl.BlockSpec((1, tn), lambda n: (0, n))],
        out_specs=pl.BlockSpec((r, tn), lambda n: (0, n)),
        out_shape=jax.ShapeDtypeStruct((r, 6 * D_MODEL), F32),
        compiler_params=_cparams(("parallel",)),
        name="ada",
    )(c_all, w_ada, b_ada.reshape(1, -1))


def _modulated_norm(x, g, shift, scale):
    ms = jnp.mean(x * x, axis=-1, keepdims=True)
    return x * lax.rsqrt(ms + EPS) * g * (1.0 + scale) + shift


def _pair_rms(x, g):
    lane = lax.broadcasted_iota(jnp.int32, x.shape, 1)
    x2 = x * x
    s_all = jnp.sum(x2, axis=-1, keepdims=True)
    s_lo = jnp.sum(jnp.where(lane < HEAD_DIM_A, x2, 0.0), axis=-1, keepdims=True)
    ss = jnp.where(lane < HEAD_DIM_A, s_lo, s_all - s_lo)
    return x * lax.rsqrt(ss * (1.0 / HEAD_DIM_A) + EPS) * g


def _proj_kernel(x_ref, mod_ref, g_ref, w_ref, wlr_ref, qg_ref, kg_ref,
                 qt_ref, kn_ref, vt_ref, o_ref, lr_ref):
    h = _modulated_norm(x_ref[...], g_ref[...], mod_ref[0, 0:1, :], mod_ref[0, 1:2, :]).astype(BF16)
    lr_ref[...] = jnp.dot(h, wlr_ref[...], preferred_element_type=F32)
    piece = lambda j: jnp.dot(h, w_ref[:, j * D_MODEL:(j + 1) * D_MODEL], preferred_element_type=F32)
    heads = [slice(i * HEAD_W_A, (i + 1) * HEAD_W_A) for i in range(N_HEADS_A)]

    qa = piece(0)
    for i, sl in enumerate(heads):
        qn = _pair_rms(qa[:, sl], qg_ref[...]) * (HEAD_DIM_A ** -0.5 * LOG2E)
        qt_ref[0, i] = qn.T.astype(BF16)
    ka = piece(1)
    for sl in heads:
        kn_ref[:, sl] = _pair_rms(ka[:, sl], kg_ref[...]).astype(BF16)
    va = piece(2)
    for i, sl in enumerate(heads):
        vt_ref[0, i] = va[:, sl].T.astype(BF16)
    for j in range(3, N_MAIN // D_MODEL):
        o_ref[:, (j - 3) * D_MODEL:(j - 2) * D_MODEL] = piece(j).astype(o_ref.dtype)


def _proj(x2, mod8, g1, w_main, w_lr, qg2, kg2, seq):
    t = x2.shape[0]
    tm = ROW_TILE
    spt = seq // tm
    once = pl.Buffered(1)
    const = lambda shape, **kw: pl.BlockSpec(shape, lambda i: (0,) * len(shape), **kw)
    tspec = pl.BlockSpec((1, N_HEADS_A, HEAD_W_A, tm), lambda i: (i // spt, 0, 0, i % spt))
    tshape = jax.ShapeDtypeStruct((t // seq, N_HEADS_A, HEAD_W_A, seq), BF16)
    return pl.pallas_call(
        _proj_kernel,
        grid=(t // tm,),
        in_specs=[pl.BlockSpec((tm, D_MODEL), lambda i: (i, 0)),
                  pl.BlockSpec((1, 8, D_MODEL), lambda i: (i // spt, 0, 0)),
                  const((1, D_MODEL)),
                  const((D_MODEL, N_MAIN), pipeline_mode=once),
                  const((D_MODEL, LR_W), pipeline_mode=once),
                  const((1, HEAD_W_A)), const((1, HEAD_W_A))],
        out_specs=[tspec,
                   pl.BlockSpec((tm, D_MODEL), lambda i: (i, 0)),
                   tspec,
                   pl.BlockSpec((tm, N_REST), lambda i: (i, 0)),
                   pl.BlockSpec((tm, LR_W), lambda i: (i, 0))],
        out_shape=[tshape,
                   jax.ShapeDtypeStruct((t, D_MODEL), BF16),
                   tshape,
                   jax.ShapeDtypeStruct((t, N_REST), BF16),
                   jax.ShapeDtypeStruct((t, LR_W), F32)],
        compiler_params=_cparams(("parallel",)),
        name="proj",
    )(x2, mod8, g1, w_main, w_lr, qg2, kg2)


def _t5_bucket(rel):
    nb = N_BUCKETS // 2
    max_exact = nb // 2
    ret = (rel > 0).astype(jnp.int32) * nb
    n = jnp.abs(rel)
    nf = jnp.maximum(n, 1).astype(F32)
    large = max_exact + (jnp.log(nf / max_exact) / math.log(MAX_DISTANCE / max_exact)
                         * (nb - max_exact)).astype(jnp.int32)
    large = jnp.minimum(large, nb - 1)
    return ret + jnp.where(n < max_exact, n, large)


def _bias_kernel(tbl_ref, bkt_ref, qg_ref, kg_ref, o_ref, ok_ref):
    h = pl.program_id(0)
    bkt = bkt_ref[...]
    acc = jnp.zeros(bkt.shape, F32)
    babs = jnp.float32(0.0)
    for i in range(N_BUCKETS):
        acc = jnp.where(bkt == i, tbl_ref[i, h], acc)
        babs = jnp.maximum(babs, jnp.abs(tbl_ref[i, h]))
    o_ref[0] = acc * LOG2E
    gmax = lambda r: jnp.max(jnp.abs(r[...]), axis=1, keepdims=True)
    bound = gmax(qg_ref) * gmax(kg_ref) * (1.02 * HEAD_DIM_A * HEAD_DIM_A ** -0.5 * LOG2E) + babs * LOG2E
    ok_ref[0] = jnp.broadcast_to((bound <= EXP2_RANGE).astype(jnp.int32), ok_ref.shape[1:])


def _bias_tiles(rel_bias, qg2, kg2):
    t = ATT_TILE
    kk = jnp.arange(t, dtype=jnp.int32)[:, None]
    qq = jnp.arange(t, dtype=jnp.int32)[None, :]
    rel = jnp.stack([(d - NEAR_TILES // 2) * t + kk - qq for d in range(NEAR_TILES)])
    gspec = pl.BlockSpec((1, HEAD_W_A), lambda h: (0, 0))
    tiles, ok = pl.pallas_call(
        _bias_kernel,
        grid=(N_HEADS_A,),
        in_specs=[pl.BlockSpec(memory_space=pltpu.SMEM),
                  pl.BlockSpec((NEAR_TILES, t, t), lambda h: (0, 0, 0)), gspec, gspec],
        out_specs=[pl.BlockSpec((1, NEAR_TILES, t, t), lambda h: (h, 0, 0, 0)),
                   pl.BlockSpec((1, 8, HEAD_W_A), lambda h: (h, 0, 0))],
        out_shape=[jax.ShapeDtypeStruct((N_HEADS_A, NEAR_TILES, t, t), F32),
                   jax.ShapeDtypeStruct((N_HEADS_A, 8, HEAD_W_A), jnp.int32)],
        compiler_params=_cparams(("parallel",)),
        name="bias",
    )(rel_bias, _t5_bucket(rel), qg2, kg2)
    return tiles, ok[:, 0, 0]


def _attn_kernel(ok_ref, qt_ref, k_ref, vt_ref, bias_ref, lamv_ref, subg_ref, o_ref,
                 w_scr, m_scr, l_scr, acc_scr, l8_scr, *, nk, qsub, lambda_init):
    t = ATT_TILE
    blk = min(ATT_BLOCK, nk)
    bt = blk * t
    h = pl.program_id(1)
    tiles = [(u, pl.program_id(2) * qsub + u) for u in range(qsub)]

    z = jnp.zeros((HEAD_DIM_A, t), BF16)
    for u, _ in tiles:
        q = qt_ref[0, 0, :, u * t:(u + 1) * t]
        w_scr[u, :, :t] = jnp.concatenate([q[:HEAD_DIM_A], z], axis=0)
        w_scr[u, :, t:] = jnp.concatenate([z, q[HEAD_DIM_A:]], axis=0)

    def chunk_bias(kj, qi):
        b = bias_ref[0, jnp.clip(kj - qi, -2, 2) + 2]
        return jnp.concatenate([b, b], axis=1)

    @pl.when(ok_ref[h] != 0)
    def _():
        l8_scr[...] = jnp.zeros(l8_scr.shape, F32)
        acc_scr[...] = jnp.zeros(acc_scr.shape, F32)

        def block(j, carry):
            k0 = pl.multiple_of(j * bt, bt)
            for u, qi in tiles:
                s = jnp.dot(k_ref[0, pl.ds(k0, bt), :], w_scr[u], preferred_element_type=F32)
                p = [jnp.exp2(s[c * t:(c + 1) * t] + chunk_bias(j * blk + c, qi)) for c in range(blk)]
                p = jnp.concatenate(p, axis=0)
                l8_scr[u] += jnp.sum(p.reshape(bt // 8, 8, 2 * t), axis=0)
                acc_scr[u] += jnp.dot(vt_ref[0, 0, :, pl.ds(k0, bt)], p.astype(BF16),
                                      preferred_element_type=F32)
            return carry

        lax.fori_loop(0, nk // blk, block, 0)
        l_scr[...] = jnp.sum(l8_scr[...], axis=1, keepdims=True)

    @pl.when(ok_ref[h] == 0)
    def _():
        m_scr[...] = jnp.full(m_scr.shape, -jnp.inf, F32)
        l_scr[...] = jnp.zeros(l_scr.shape, F32)
        acc_scr[...] = jnp.zeros(acc_scr.shape, F32)

        def online_step(kj, carry):
            k0 = pl.multiple_of(kj * t, t)
            for u, qi in tiles:
                s = jnp.dot(k_ref[0, pl.ds(k0, t), :], w_scr[u], preferred_element_type=F32)
                s = s + chunk_bias(kj, qi)
                m_old = m_scr[u]
                m_new = jnp.maximum(m_old, jnp.max(s, axis=0, keepdims=True))
                alpha = jnp.exp2(m_old - m_new)
                p = jnp.exp2(s - m_new)
                l_scr[u] = alpha * l_scr[u] + jnp.sum(p, axis=0, keepdims=True)
                pv = jnp.dot(vt_ref[0, 0, :, pl.ds(k0, t)], p.astype(BF16), preferred_element_type=F32)
                acc_scr[u] = alpha * acc_scr[u] + pv
                m_scr[u] = m_new
            return carry

        lax.fori_loop(0, nk, online_step, 0)

    lv = lamv_ref[...]
    lam = (jnp.exp(jnp.sum(lv[0:1] * lv[1:2], axis=-1, keepdims=True))
           - jnp.exp(jnp.sum(lv[2:3] * lv[3:4], axis=-1, keepdims=True)) + lambda_init)
    for u, _ in tiles:
        acc = acc_scr[u]
        inv = 1.0 / l_scr[u]
        o = acc[:, :t] * inv[:, :t] - lam * (acc[:, t:] * inv[:, t:])
        ms = jnp.mean(o * o, axis=0, keepdims=True)
        y = (o * lax.rsqrt(ms + EPS)).T
        o_ref[0, u * t:(u + 1) * t, :] = (y * subg_ref[...] * (1.0 - lambda_init)).astype(o_ref.dtype)


def _attn(range_ok, qt, kn, vt, bias_t, lamv, subg, lambda_init):
    b, s, _ = kn.shape
    t = ATT_TILE
    nk = s // t
    assert nk % min(ATT_BLOCK, nk) == 0 and t >= MAX_DISTANCE
    qsub = 8 if nk <= ATT_BLOCK and nk % 8 == 0 else 2
    kern = functools.partial(_attn_kernel, nk=nk, qsub=qsub, lambda_init=lambda_init)
    const = lambda shape: pl.BlockSpec(shape, lambda bi, h, qi: (0,) * len(shape))
    tq = qsub * t
    return pl.pallas_call(
        kern,
        grid=(b, N_HEADS_A, s // tq),
        in_specs=[pl.BlockSpec(memory_space=pltpu.SMEM),
                  pl.BlockSpec((1, 1, HEAD_W_A, tq), lambda bi, h, qi: (bi, h, 0, qi)),
                  pl.BlockSpec((1, s, HEAD_W_A), lambda bi, h, qi: (bi, 0, h)),
                  pl.BlockSpec((1, 1, HEAD_W_A, s), lambda bi, h, qi: (bi, h, 0, 0)),
                  pl.BlockSpec((1, NEAR_TILES, t, t), lambda bi, h, qi: (h, 0, 0, 0)),
                  const((8, HEAD_W_A)), const((1, HEAD_W_A))],
        out_specs=pl.BlockSpec((1, tq, HEAD_W_A), lambda bi, h, qi: (bi, qi, h)),
        out_shape=jax.ShapeDtypeStruct((b, s, D_MODEL), BF16),
        scratch_shapes=[pltpu.VMEM((qsub, HEAD_W_A, 2 * t), BF16),
                        pltpu.VMEM((qsub, 1, 2 * t), F32),
                        pltpu.VMEM((qsub, 1, 2 * t), F32),
                        pltpu.VMEM((qsub, HEAD_W_A, 2 * t), F32),
                        pltpu.VMEM((qsub, 8, 2 * t), F32)],
        compiler_params=_cparams(("parallel", "parallel", "parallel")),
        name="attn",
    )(range_ok, qt, kn, vt, bias_t, lamv, subg)


def _log_gate(lr, wg_ref, bg_ref):
    x = jnp.dot(lr.astype(BF16), wg_ref[...], preferred_element_type=F32) + bg_ref[...]
    return (jnp.minimum(x, 0.0) - jnp.log1p(jnp.exp(-jnp.abs(x)))) * (LOG2E / GATE_NORM)


def _gla_bwd_kernel(qd_ref, kd_ref, tot_ref, v_ref, og_ref, of_ref, gn_ref, o_ref, s_scr):
    c = GLA_CHUNK

    @pl.when(pl.program_id(1) == 0)
    def _():
        s_scr[...] = jnp.zeros(s_scr.shape, F32)

    for j in reversed(range(qd_ref.shape[1] // c)):
        rows = slice(j * c, (j + 1) * c)
        for h in range(N_HEADS_B):
            sl = slice(h * KEY_DIM_B, (h + 1) * KEY_DIM_B)
            vs = slice(h * V_DIM_B, (h + 1) * V_DIM_B)
            st = s_scr[h]
            o = jnp.dot(qd_ref[0, rows, sl], st.astype(BF16), preferred_element_type=F32) + of_ref[0, rows, vs]
            dec = jnp.broadcast_to(jnp.exp2(tot_ref[0, j, 0:1, sl]), (c, KEY_DIM_B)).T[:, 0:1]
            s_scr[h] = st * dec + jnp.dot(kd_ref[0, rows, sl].astype(F32).T.astype(BF16), v_ref[0, rows, vs],
                                          preferred_element_type=F32)
            ms = jnp.mean(o * o, axis=-1, keepdims=True)
            og = og_ref[0, rows, vs].astype(F32)
            o_ref[0, rows, vs] = (o * lax.rsqrt(ms + EPS) * gn_ref[...] * (og / (1.0 + jnp.exp(-og)))
                                  ).astype(o_ref.dtype)


def _gla_chunk_scores(q, k, gf, gb):
    c = GLA_CHUNK
    ri = lax.broadcasted_iota(jnp.int32, (c, c), 0)
    ci = lax.broadcasted_iota(jnp.int32, (c, c), 1)
    xr = jnp.bitwise_xor(ri, ci)
    nt = (((1,), (1,)), ((), ()))
    pre, tf, suf, tb = gf, gf, gb, gb
    a = jnp.where(ri == ci, 2.0 * lax.dot_general(q.astype(BF16), k.astype(BF16), nt,
                                                  preferred_element_type=F32), 0.0)

    def level_scores(a, blk, q_arg, k_arg):
        r = lax.dot_general((q * jnp.exp2(q_arg)).astype(BF16), (k * jnp.exp2(k_arg)).astype(BF16),
                            nt, preferred_element_type=F32)
        return jnp.where(jnp.logical_and(xr >= blk, xr < 2 * blk), r, a)

    blk = 1
    while blk < SUBLANES:
        hi = jnp.bitwise_and(ri, blk) != 0
        a = level_scores(a, blk, jnp.where(hi, pre, suf), jnp.where(hi, tb - suf, tf - pre))
        tf_dn, tf_up = pltpu.roll(tf, blk, 0), pltpu.roll(tf, c - blk, 0)
        tb_dn, tb_up = pltpu.roll(tb, blk, 0), pltpu.roll(tb, c - blk, 0)
        pre = pre + jnp.where(hi, tf_dn, 0.0)
        suf = suf + jnp.where(hi, 0.0, tb_up)
        tf = tf + jnp.where(hi, tf_dn, tf_up)
        tb = tb + jnp.where(hi, tb_dn, tb_up)
        blk *= 2
    ns = c // SUBLANES
    slabs = lambda x: [x[SUBLANES * r:SUBLANES * (r + 1)] for r in range(ns)]
    pre, suf, tf, tb = slabs(pre), slabs(suf), slabs(tf), slabs(tb)
    while blk < c:
        g = blk // SUBLANES
        up = [bool(r & g) for r in range(ns)]
        q_arg = [pre[r] if up[r] else suf[r] for r in range(ns)]
        k_arg = [tb[r] - suf[r] if up[r] else tf[r] - pre[r] for r in range(ns)]
        a = level_scores(a, blk, jnp.concatenate(q_arg, axis=0), jnp.concatenate(k_arg, axis=0))
        pre = [pre[r] + tf[r - g] if up[r] else pre[r] for r in range(ns)]
        suf = [suf[r] if up[r] else suf[r] + tb[r + g] for r in range(ns)]
        tf_pair = {r: tf[r] + tf[r + g] for r in range(ns) if not up[r]}
        tb_pair = {r: tb[r] + tb[r + g] for r in range(ns) if not up[r]}
        tf = [tf_pair[r - g] if up[r] else tf_pair[r] for r in range(ns)]
        tb = [tb_pair[r - g] if up[r] else tb_pair[r] for r in range(ns)]
        blk *= 2
    cat = lambda x: jnp.concatenate(x, axis=0)
    return a, cat(pre), cat(tf), cat(suf), cat(tb)


def _gla_fwd_chunk(q, k, v, gf, gb, s_scr, h):
    a, pre, tf, suf, tb = _gla_chunk_scores(q, k, gf, gb)
    st = s_scr[h]
    o = (jnp.dot(a.astype(BF16), v, preferred_element_type=F32)
         + jnp.dot((q * jnp.exp2(pre)).astype(BF16), st.astype(BF16), preferred_element_type=F32))
    kd = k * jnp.exp2(tf - pre)
    s_scr[h] = st * jnp.exp2(tf).T[:, 0:1] + jnp.dot(kd.T.astype(BF16), v, preferred_element_type=F32)
    return o, (q * jnp.exp2(suf)).astype(BF16), (k * jnp.exp2(tb - suf)).astype(BF16), tb[0:SUBLANES]


def _gla_fwd_kernel(q_ref, k_ref, v_ref, lr_ref, wgf_ref, bgf_ref, wgb_ref, bgb_ref,
                    o_ref, qd_ref, kd_ref, tot_ref, s_scr):
    @pl.when(pl.program_id(1) == 0)
    def _():
        s_scr[...] = jnp.zeros(s_scr.shape, F32)

    lr = lr_ref[0]
    gf = _log_gate(lr, wgf_ref, bgf_ref)
    gb = _log_gate(lr, wgb_ref, bgb_ref)
    for h in range(N_HEADS_B):
        sl = slice(h * KEY_DIM_B, (h + 1) * KEY_DIM_B)
        vs = slice(h * V_DIM_B, (h + 1) * V_DIM_B)
        q = q_ref[0, :, sl].astype(F32) * KEY_DIM_B ** -0.5
        k = k_ref[0, :, sl].astype(F32)
        o_ref[0, :, vs], qd_ref[0, :, sl], kd_ref[0, :, sl], tot_ref[0, 0, :, sl] = _gla_fwd_chunk(
            q, k, v_ref[0, :, vs], gf[:, sl], gb[:, sl], s_scr, h)


def _gla(main3, lr3, wgf, bgf, wgb, bgb, gn):
    b, s, _ = main3.shape
    c = GLA_CHUNK
    nc = s // c
    hk = N_HEADS_B * KEY_DIM_B
    state = pltpu.VMEM((N_HEADS_B, KEY_DIM_B, V_DIM_B), F32)
    full = lambda shape: pl.BlockSpec(shape, lambda bi, i: (0,) * len(shape))
    fwd = lambda w, j: pl.BlockSpec((1, c, w), lambda bi, i: (bi, i, j))
    of3, qd, kd, tot = pl.pallas_call(
        _gla_fwd_kernel,
        grid=(b, nc),
        in_specs=[fwd(hk, 0), fwd(hk, 1), fwd(D_MODEL, 1), fwd(LR_W, 0),
                  full((LR_W, hk)), full((1, hk)), full((LR_W, hk)), full((1, hk))],
        out_specs=[fwd(D_MODEL, 0), fwd(hk, 0), fwd(hk, 0),
                   pl.BlockSpec((1, 1, SUBLANES, hk), lambda bi, i: (bi, i, 0, 0))],
        out_shape=[jax.ShapeDtypeStruct((b, s, D_MODEL), F32),
                   jax.ShapeDtypeStruct((b, s, hk), BF16),
                   jax.ShapeDtypeStruct((b, s, hk), BF16),
                   jax.ShapeDtypeStruct((b, nc, SUBLANES, hk), F32)],
        scratch_shapes=[state],
        compiler_params=_cparams(("parallel", "arbitrary")),
        name="gla_fwd",
    )(main3, main3, main3, lr3, wgf, bgf, wgb, bgb)

    cps = math.gcd(nc, GLA_BWD_CHUNKS)
    rev = lambda i: nc // cps - 1 - i
    blk = lambda w, j: pl.BlockSpec((1, cps * c, w), lambda bi, i: (bi, rev(i), j))
    return pl.pallas_call(
        _gla_bwd_kernel,
        grid=(b, nc // cps),
        in_specs=[blk(hk, 0), blk(hk, 0),
                  pl.BlockSpec((1, cps, SUBLANES, hk), lambda bi, i: (bi, rev(i), 0, 0)),
                  blk(D_MODEL, 1), blk(D_MODEL, 2), blk(D_MODEL, 0), full((1, V_DIM_B))],
        out_specs=blk(D_MODEL, 0),
        out_shape=jax.ShapeDtypeStruct((b, s, D_MODEL), BF16),
        scratch_shapes=[state],
        compiler_params=_cparams(("parallel", "arbitrary")),
        name="gla_bwd",
    )(qd, kd, tot, main3, main3, of3, gn)


def _tail_kernel(x_ref, oa_ref, ob_ref, ga_ref, gb_ref, mod_ref, g2_ref, wa_ref, wb_ref, wo_ref,
                 wu_ref, wd_ref, o_ref):
    ya = jnp.dot(oa_ref[...], wa_ref[...], preferred_element_type=F32)
    yb = jnp.dot(ob_ref[...], wb_ref[...], preferred_element_type=F32)
    sig = lambda r: 1.0 / (1.0 + jnp.exp(-r[...].astype(F32)))
    merged = sig(ga_ref) * ya + sig(gb_ref) * yb
    x1 = x_ref[...] + mod_ref[0, 2:3, :] * jnp.dot(merged.astype(BF16), wo_ref[...],
                                                   preferred_element_type=F32)
    h = _modulated_norm(x1, g2_ref[...], mod_ref[0, 3:4, :], mod_ref[0, 4:5, :]).astype(BF16)
    acc = jnp.zeros(x1.shape, F32)
    for j in range(D_FF // D_MODEL):
        cs = slice(j * D_MODEL, (j + 1) * D_MODEL)
        u = jnp.maximum(jnp.dot(h, wu_ref[:, cs], preferred_element_type=F32), 0.0)
        acc = acc + jnp.dot((u * u).astype(BF16), wd_ref[cs, :], preferred_element_type=F32)
    o_ref[...] = x1 + mod_ref[0, 5:6, :] * acc


def _tail(x2, oa2, ob2, main2, mod8, g2, wa, wb, wo, wu, wd, seq):
    t = x2.shape[0]
    tm = ROW_TILE
    row = lambda j: pl.BlockSpec((tm, D_MODEL), lambda i: (i, j))
    once = lambda shape: pl.BlockSpec(shape, lambda i: (0,) * len(shape), pipeline_mode=pl.Buffered(1))
    return pl.pallas_call(
        _tail_kernel,
        grid=(t // tm,),
        in_specs=[row(0), row(0), row(0), row(3), row(4),
                  pl.BlockSpec((1, 8, D_MODEL), lambda i: ((i * tm) // seq, 0, 0)),
                  once((1, D_MODEL)),
                  once((D_MODEL, D_MODEL)), once((D_MODEL, D_MODEL)), once((D_MODEL, D_MODEL)),
                  once((D_MODEL, D_FF)), once((D_FF, D_MODEL))],
        out_specs=row(0),
        out_shape=jax.ShapeDtypeStruct((t, D_MODEL), F32),
        compiler_params=_cparams(("parallel",)),
        name="tail",
    )(x2, oa2, ob2, main2, main2, mod8, g2, wa, wb, wo, wu, wd)


def _split_w_in(w_in):
    o = np.cumsum([0, 1024, 1024, 1024, 512, 512, 1024, 1024, GATE_RANK, GATE_RANK, 1024, 1024])
    main = jnp.concatenate([w_in[:, o[0]:o[7]], w_in[:, o[9]:o[11]]], axis=1)
    lr = jnp.pad(w_in[:, o[7]:o[9]], ((0, 0), (0, LR_W - 2 * GATE_RANK)))
    return main.astype(BF16), lr.astype(BF16)


def _pad_gate_w(w_gate, row0):
    return jnp.pad(w_gate, ((row0, LR_W - GATE_RANK - row0), (0, 0))).astype(BF16)


def _layer(x, mod8, layer_idx, p):
    b, s, _ = x.shape
    t = b * s
    lambda_init = 0.8 - 0.6 * math.exp(-0.3 * layer_idx)
    x2 = x.reshape(t, D_MODEL)
    row = lambda v: v.reshape(1, -1).astype(F32)

    qg2, kg2 = row(jnp.tile(p["q_norm_g"], 2)), row(jnp.tile(p["k_norm_g"], 2))
    qt, kn2, vt, main2, lr2 = _proj(x2, mod8, row(p["norm1_g"]), p["w_main"], p["w_lr"], qg2, kg2, s)
    kn = kn2.reshape(b, s, D_MODEL)
    main3 = main2.reshape(b, s, N_REST)
    lr3 = lr2.reshape(b, s, LR_W)
    lamv = jnp.zeros((8, HEAD_W_A), F32).at[0:4, :HEAD_DIM_A].set(
        jnp.stack([p["lam_q1"], p["lam_k1"], p["lam_q2"], p["lam_k2"]]).astype(F32))
    bias_t, range_ok = _bias_tiles(p["rel_bias"], qg2, kg2)
    oa = _attn(range_ok, qt, kn, vt, bias_t, lamv, row(p["subln_g"]), lambda_init)

    ob = _gla(main3, lr3, p["wgf"], row(p["b_gate_f"]), p["wgb"], row(p["b_gate_b"]), row(p["gla_norm_g"]))

    y = _tail(x2, oa.reshape(t, D_MODEL), ob.reshape(t, D_MODEL), main2, mod8, row(p["norm2_g"]),
              p["w_branch_a"], p["w_branch_b"], p["w_out"], p["w_up"], p["w_down"], s)
    return y.reshape(b, s, D_MODEL)


def kernel(x_prompt, x_sample, c_prompt, c_sample, rel_bias, w_ada, b_ada, norm1_g, w_in, q_norm_g, k_norm_g, lam_q1, lam_k1, lam_q2, lam_k2, subln_g, w_gate_f, b_gate_f, w_gate_b, b_gate_b, gla_norm_g, w_branch_a, w_branch_b, w_out, norm2_g, w_up, w_down):
    depth = w_in.shape[0]
    nb_p, nb_s = c_prompt.shape[0], c_sample.shape[0]
    rows = -(-(nb_p + nb_s) // 8) * 8
    c_all = jnp.pad(jnp.concatenate([c_prompt, c_sample], axis=0), ((0, rows - nb_p - nb_s), (0, 0)))

    xp, xs = x_prompt, x_sample
    for l in range(depth):
        mod = _ada(c_all, w_ada[l], b_ada[l]).reshape(rows, 6, D_MODEL)
        mod8 = jnp.pad(mod, ((0, 0), (0, 2), (0, 0)))
        w_main, w_lr = _split_w_in(w_in[l])
        p = dict(rel_bias=rel_bias.astype(F32), norm1_g=norm1_g[l], w_main=w_main, w_lr=w_lr,
                 q_norm_g=q_norm_g[l], k_norm_g=k_norm_g[l], lam_q1=lam_q1[l], lam_k1=lam_k1[l],
                 lam_q2=lam_q2[l], lam_k2=lam_k2[l], subln_g=subln_g[l],
                 wgf=_pad_gate_w(w_gate_f[l], 0), b_gate_f=b_gate_f[l],
                 wgb=_pad_gate_w(w_gate_b[l], GATE_RANK), b_gate_b=b_gate_b[l],
                 gla_norm_g=gla_norm_g[l], w_branch_a=w_branch_a[l].astype(BF16),
                 w_branch_b=w_branch_b[l].astype(BF16), w_out=w_out[l].astype(BF16),
                 norm2_g=norm2_g[l], w_up=w_up[l].astype(BF16), w_down=w_down[l].astype(BF16))
        xp = _layer(xp, mod8[:nb_p], l, p)
        xs = _layer(xs, mod8[nb_p:nb_p + nb_s], l, p)
    return (xp, xs)
```

```python
import functools
import math

import jax
import jax.numpy as jnp
import numpy as np
from jax import lax
from jax.experimental import pallas as pl
from jax.experimental.pallas import tpu as pltpu

F32 = jnp.float32
BF16 = jnp.bfloat16

D_MODEL = 1024
HEAD_DIM_A = 64
N_HEADS_A = 8
HEAD_W_A = 2 * HEAD_DIM_A
N_HEADS_B = 4
KEY_DIM_B = 128
V_DIM_B = 256
GATE_RANK = 16
GATE_NORM = 16.0
N_BUCKETS = 32
MAX_DISTANCE = 128
D_FF = 4 * D_MODEL
EPS = 1e-6
LOG2E = math.log2(math.e)

N_MAIN = 8 * D_MODEL
N_REST = 5 * D_MODEL
LR_W = 128

VMEM_LIMIT = 56 * 1024 * 1024
ROW_TILE = 512
ATT_TILE = 256
ATT_BLOCK = 32
NEAR_TILES = 5
EXP2_RANGE = 100.0
GLA_CHUNK = 128
GLA_BWD_CHUNKS = 4
SUBLANES = 8


def _cparams(sem):
    return pltpu.CompilerParams(dimension_semantics=sem, vmem_limit_bytes=VMEM_LIMIT)


def _ada_kernel(c_ref, w_ref, b_ref, o_ref):
    c = c_ref[...]
    a = c / (1.0 + jnp.exp(-c))
    o_ref[...] = jnp.dot(a, w_ref[...], preferred_element_type=F32,
                         precision=lax.Precision.HIGHEST) + b_ref[...]


def _ada(c_all, w_ada, b_ada):
    r = c_all.shape[0]
    tn = 1024
    return pl.pallas_call(
        _ada_kernel,
        grid=(6 * D_MODEL // tn,),
        in_specs=[pl.BlockSpec((r, D_MODEL), lambda n: (0, 0)),
                  pl.BlockSpec((D_MODEL, tn), lambda n: (0, n)),
                  pl.BlockSpec((1, tn), lambda n: (0, n))],
        out_specs=pl.BlockSpec((r, tn), lambda n: (0, n)),
        out_shape=jax.ShapeDtypeStruct((r, 6 * D_MODEL), F32),
        compiler_params=_cparams(("parallel",)),
        name="ada",
    )(c_all, w_ada, b_ada.reshape(1, -1))


def _modulated_norm(x, g, shift, scale):
    ms = jnp.mean(x * x, axis=-1, keepdims=True)
    return x * lax.rsqrt(ms + EPS) * g * (1.0 + scale) + shift


def _pair_rms(x, g):
    lane = lax.broadcasted_iota(jnp.int32, x.shape, 1)
    x2 = x * x
    s_all = jnp.sum(x2, axis=-1, keepdims=True)
    s_lo = jnp.sum(jnp.where(lane < HEAD_DIM_A, x2, 0.0), axis=-1, keepdims=True)
    ss = jnp.where(lane < HEAD_DIM_A, s_lo, s_all - s_lo)
    return x * lax.rsqrt(ss * (1.0 / HEAD_DIM_A) + EPS) * g


def _proj_kernel(x_ref, mod_ref, g_ref, w_ref, wlr_ref, qg_ref, kg_ref,
                 qt_ref, kn_ref, vt_ref, o_ref, lr_ref):
    h = _modulated_norm(x_ref[...], g_ref[...], mod_ref[0, 0:1, :], mod_ref[0, 1:2, :]).astype(BF16)
    lr_ref[...] = jnp.dot(h, wlr_ref[...], preferred_element_type=F32)
    piece = lambda j: jnp.dot(h, w_ref[:, j * D_MODEL:(j + 1) * D_MODEL], preferred_element_type=F32)
    heads = [slice(i * HEAD_W_A, (i + 1) * HEAD_W_A) for i in range(N_HEADS_A)]

    qa = piece(0)
    for i, sl in enumerate(heads):
        qn = _pair_rms(qa[:, sl], qg_ref[...]) * (HEAD_DIM_A ** -0.5 * LOG2E)
        qt_ref[0, i] = qn.T.astype(BF16)
    ka = piece(1)
    for sl in heads:
        kn_ref[:, sl] = _pair_rms(ka[:, sl], kg_ref[...]).astype(BF16)
    va = piece(2)
    for i, sl in enumerate(heads):
        vt_ref[0, i] = va[:, sl].T.astype(BF16)
    for j in range(3, N_MAIN // D_MODEL):
        o_ref[:, (j - 3) * D_MODEL:(j - 2) * D_MODEL] = piece(j).astype(o_ref.dtype)


def _proj(x2, mod8, g1, w_main, w_lr, qg2, kg2, seq):
    t = x2.shape[0]
    tm = ROW_TILE
    spt = seq // tm
    once = pl.Buffered(1)
    const = lambda shape, **kw: pl.BlockSpec(shape, lambda i: (0,) * len(shape), **kw)
    tspec = pl.BlockSpec((1, N_HEADS_A, HEAD_W_A, tm), lambda i: (i // spt, 0, 0, i % spt))
    tshape = jax.ShapeDtypeStruct((t // seq, N_HEADS_A, HEAD_W_A, seq), BF16)
    return pl.pallas_call(
        _proj_kernel,
        grid=(t // tm,),
        in_specs=[pl.BlockSpec((tm, D_MODEL), lambda i: (i, 0)),
                  pl.BlockSpec((1, SUBLANES, D_MODEL), lambda i: (i // spt, 0, 0)),
                  const((1, D_MODEL)),
                  const((D_MODEL, N_MAIN), pipeline_mode=once),
                  const((D_MODEL, LR_W), pipeline_mode=once),
                  const((1, HEAD_W_A)), const((1, HEAD_W_A))],
        out_specs=[tspec,
                   pl.BlockSpec((tm, D_MODEL), lambda i: (i, 0)),
                   tspec,
                   pl.BlockSpec((tm, N_REST), lambda i: (i, 0)),
                   pl.BlockSpec((tm, LR_W), lambda i: (i, 0))],
        out_shape=[tshape,
                   jax.ShapeDtypeStruct((t, D_MODEL), BF16),
                   tshape,
                   jax.ShapeDtypeStruct((t, N_REST), BF16),
                   jax.ShapeDtypeStruct((t, LR_W), F32)],
        compiler_params=_cparams(("parallel",)),
        name="proj",
    )(x2, mod8, g1, w_main, w_lr, qg2, kg2)


def _t5_bucket(rel):
    nb = N_BUCKETS // 2
    max_exact = nb // 2
    ret = (rel > 0).astype(jnp.int32) * nb
    n = jnp.abs(rel)
    nf = jnp.maximum(n, 1).astype(F32)
    large = max_exact + (jnp.log(nf / max_exact) / math.log(MAX_DISTANCE / max_exact)
                         * (nb - max_exact)).astype(jnp.int32)
    large = jnp.minimum(large, nb - 1)
    return ret + jnp.where(n < max_exact, n, large)


def _bias_kernel(tbl_ref, bkt_ref, qg_ref, kg_ref, o_ref, ok_ref):
    h = pl.program_id(0)
    bkt = bkt_ref[1:NEAR_TILES - 1]
    acc = jnp.zeros(bkt.shape, F32)
    babs = jnp.float32(0.0)
    for i in range(N_BUCKETS):
        acc = jnp.where(bkt == i, tbl_ref[i, h], acc)
        babs = jnp.maximum(babs, jnp.abs(tbl_ref[i, h]))
    o_ref[0, 1:NEAR_TILES - 1] = acc * LOG2E
    o_ref[0, 0] = jnp.full(o_ref.shape[2:], tbl_ref[N_BUCKETS // 2 - 1, h] * LOG2E, F32)
    o_ref[0, NEAR_TILES - 1] = jnp.full(o_ref.shape[2:], tbl_ref[N_BUCKETS - 1, h] * LOG2E, F32)
    gmax = lambda r: jnp.max(jnp.abs(r[...]), axis=1, keepdims=True)
    bound = gmax(qg_ref) * gmax(kg_ref) * (1.02 * HEAD_DIM_A * HEAD_DIM_A ** -0.5 * LOG2E) + babs * LOG2E
    ok_ref[0] = jnp.broadcast_to((bound <= EXP2_RANGE).astype(jnp.int32), ok_ref.shape[1:])


def _bias_tiles(rel_bias, qg2, kg2):
    t = ATT_TILE
    kk = jnp.arange(t, dtype=jnp.int32)[:, None]
    qq = jnp.arange(t, dtype=jnp.int32)[None, :]
    rel = jnp.stack([(d - NEAR_TILES // 2) * t + kk - qq for d in range(NEAR_TILES)])
    gspec = pl.BlockSpec((1, HEAD_W_A), lambda h: (0, 0))
    tiles, ok = pl.pallas_call(
        _bias_kernel,
        grid=(N_HEADS_A,),
        in_specs=[pl.BlockSpec(memory_space=pltpu.SMEM),
                  pl.BlockSpec((NEAR_TILES, t, t), lambda h: (0, 0, 0)), gspec, gspec],
        out_specs=[pl.BlockSpec((1, NEAR_TILES, t, t), lambda h: (h, 0, 0, 0)),
                   pl.BlockSpec((1, SUBLANES, HEAD_W_A), lambda h: (h, 0, 0))],
        out_shape=[jax.ShapeDtypeStruct((N_HEADS_A, NEAR_TILES, t, t), F32),
                   jax.ShapeDtypeStruct((N_HEADS_A, SUBLANES, HEAD_W_A), jnp.int32)],
        compiler_params=_cparams(("parallel",)),
        name="bias",
    )(rel_bias, _t5_bucket(rel), qg2, kg2)
    return tiles, ok[:, 0, 0]


def _attn_kernel(ok_ref, qt_ref, k_ref, vt_ref, bias_ref, lamv_ref, subg_ref, o_ref,
                 w_scr, m_scr, l_scr, acc_scr, l8_scr, *, nk, qsub, lambda_init):
    t = ATT_TILE
    blk = min(ATT_BLOCK, nk)
    bt = blk * t
    h = pl.program_id(1)
    tiles = [(u, pl.program_id(2) * qsub + u) for u in range(qsub)]

    z = jnp.zeros((HEAD_DIM_A, t), BF16)
    for u, _ in tiles:
        q = qt_ref[0, 0, :, u * t:(u + 1) * t]
        w_scr[u, :, :t] = jnp.concatenate([q[:HEAD_DIM_A], z], axis=0)
        w_scr[u, :, t:] = jnp.concatenate([z, q[HEAD_DIM_A:]], axis=0)

    def chunk_bias(kj, qi):
        b = bias_ref[0, jnp.clip(kj - qi, -2, 2) + 2]
        return jnp.concatenate([b, b], axis=1)

    @pl.when(ok_ref[h] != 0)
    def _():
        l8_scr[...] = jnp.zeros(l8_scr.shape, F32)
        acc_scr[...] = jnp.zeros(acc_scr.shape, F32)

        def block(j, carry):
            k0 = pl.multiple_of(j * bt, bt)
            for u, qi in tiles:
                s = jnp.dot(k_ref[0, pl.ds(k0, bt), :], w_scr[u], preferred_element_type=F32)
                p = [jnp.exp2(s[c * t:(c + 1) * t] + chunk_bias(j * blk + c, qi)) for c in range(blk)]
                p = jnp.concatenate(p, axis=0)
                l8_scr[u] += jnp.sum(p.reshape(bt // SUBLANES, SUBLANES, 2 * t), axis=0)
                acc_scr[u] += jnp.dot(vt_ref[0, 0, :, pl.ds(k0, bt)], p.astype(BF16),
                                      preferred_element_type=F32)
            return carry

        lax.fori_loop(0, nk // blk, block, 0)
        l_scr[...] = jnp.sum(l8_scr[...], axis=1, keepdims=True)

    @pl.when(ok_ref[h] == 0)
    def _():
        m_scr[...] = jnp.full(m_scr.shape, -jnp.inf, F32)
        l_scr[...] = jnp.zeros(l_scr.shape, F32)
        acc_scr[...] = jnp.zeros(acc_scr.shape, F32)

        def online_step(kj, carry):
            k0 = pl.multiple_of(kj * t, t)
            for u, qi in tiles:
                s = jnp.dot(k_ref[0, pl.ds(k0, t), :], w_scr[u], preferred_element_type=F32)
                s = s + chunk_bias(kj, qi)
                m_old = m_scr[u]
                m_new = jnp.maximum(m_old, jnp.max(s, axis=0, keepdims=True))
                alpha = jnp.exp2(m_old - m_new)
                p = jnp.exp2(s - m_new)
                l_scr[u] = alpha * l_scr[u] + jnp.sum(p, axis=0, keepdims=True)
                pv = jnp.dot(vt_ref[0, 0, :, pl.ds(k0, t)], p.astype(BF16), preferred_element_type=F32)
                acc_scr[u] = alpha * acc_scr[u] + pv
                m_scr[u] = m_new
            return carry

        lax.fori_loop(0, nk, online_step, 0)

    lv = lamv_ref[...]
    lam = (jnp.exp(jnp.sum(lv[0:1] * lv[1:2], axis=-1, keepdims=True))
           - jnp.exp(jnp.sum(lv[2:3] * lv[3:4], axis=-1, keepdims=True)) + lambda_init)
    for u, _ in tiles:
        acc = acc_scr[u]
        inv = 1.0 / l_scr[u]
        o = acc[:, :t] * inv[:, :t] - lam * (acc[:, t:] * inv[:, t:])
        ms = jnp.mean(o * o, axis=0, keepdims=True)
        y = (o * lax.rsqrt(ms + EPS)).T
        o_ref[0, u * t:(u + 1) * t, :] = (y * subg_ref[...] * (1.0 - lambda_init)).astype(o_ref.dtype)


def _attn(range_ok, qt, kn, vt, bias_t, lamv, subg, lambda_init):
    b, s, _ = kn.shape
    t = ATT_TILE
    nk = s // t
    assert nk % min(ATT_BLOCK, nk) == 0 and t >= MAX_DISTANCE
    qsub = 8 if nk <= ATT_BLOCK and nk % 8 == 0 else 2
    kern = functools.partial(_attn_kernel, nk=nk, qsub=qsub, lambda_init=lambda_init)
    const = lambda shape: pl.BlockSpec(shape, lambda bi, h, qi: (0,) * len(shape))
    tq = qsub * t
    return pl.pallas_call(
        kern,
        grid=(b, N_HEADS_A, s // tq),
        in_specs=[pl.BlockSpec(memory_space=pltpu.SMEM),
                  pl.BlockSpec((1, 1, HEAD_W_A, tq), lambda bi, h, qi: (bi, h, 0, qi)),
                  pl.BlockSpec((1, s, HEAD_W_A), lambda bi, h, qi: (bi, 0, h)),
                  pl.BlockSpec((1, 1, HEAD_W_A, s), lambda bi, h, qi: (bi, h, 0, 0)),
                  pl.BlockSpec((1, NEAR_TILES, t, t), lambda bi, h, qi: (h, 0, 0, 0)),
                  const((SUBLANES, HEAD_W_A)), const((1, HEAD_W_A))],
        out_specs=pl.BlockSpec((1, tq, HEAD_W_A), lambda bi, h, qi: (bi, qi, h)),
        out_shape=jax.ShapeDtypeStruct((b, s, D_MODEL), BF16),
        scratch_shapes=[pltpu.VMEM((qsub, HEAD_W_A, 2 * t), BF16),
                        pltpu.VMEM((qsub, 1, 2 * t), F32),
                        pltpu.VMEM((qsub, 1, 2 * t), F32),
                        pltpu.VMEM((qsub, HEAD_W_A, 2 * t), F32),
                        pltpu.VMEM((qsub, SUBLANES, 2 * t), F32)],
        compiler_params=_cparams(("parallel", "parallel", "parallel")),
        name="attn",
    )(range_ok, qt, kn, vt, bias_t, lamv, subg)


def _log_gate(lr, wg_ref, bg_ref):
    x = jnp.dot(lr.astype(BF16), wg_ref[...], preferred_element_type=F32) + bg_ref[...]
    return (jnp.minimum(x, 0.0) - jnp.log1p(jnp.exp(-jnp.abs(x)))) * (LOG2E / GATE_NORM)


def _gla_bwd_kernel(qd_ref, kd_ref, tot_ref, v_ref, og_ref, of_ref, gn_ref, o_ref, s_scr):
    c = GLA_CHUNK

    @pl.when(pl.program_id(1) == 0)
    def _():
        s_scr[...] = jnp.zeros(s_scr.shape, F32)

    for j in reversed(range(qd_ref.shape[1] // c)):
        rows = slice(j * c, (j + 1) * c)
        for h in range(N_HEADS_B):
            sl = slice(h * KEY_DIM_B, (h + 1) * KEY_DIM_B)
            vs = slice(h * V_DIM_B, (h + 1) * V_DIM_B)
            st = s_scr[h]
            o = jnp.dot(qd_ref[0, rows, sl], st.astype(BF16), preferred_element_type=F32) + of_ref[0, rows, vs]
            dec = jnp.broadcast_to(jnp.exp2(tot_ref[0, j, 0:1, sl]), (c, KEY_DIM_B)).T[:, 0:1]
            s_scr[h] = st * dec + jnp.dot(kd_ref[0, rows, sl].astype(F32).T.astype(BF16), v_ref[0, rows, vs],
                                          preferred_element_type=F32)
            ms = jnp.mean(o * o, axis=-1, keepdims=True)
            og = og_ref[0, rows, vs].astype(F32)
            o_ref[0, rows, vs] = (o * lax.rsqrt(ms + EPS) * gn_ref[...] * (og / (1.0 + jnp.exp(-og)))
                                  ).astype(o_ref.dtype)


def _gla_chunk_scores(q, k, gf, gb):
    c = GLA_CHUNK
    ri = lax.broadcasted_iota(jnp.int32, (c, c), 0)
    ci = lax.broadcasted_iota(jnp.int32, (c, c), 1)
    xr = jnp.bitwise_xor(ri, ci)
    nt = (((1,), (1,)), ((), ()))
    pre, tf, suf, tb = gf, gf, gb, gb
    a = jnp.where(ri == ci, 2.0 * lax.dot_general(q.astype(BF16), k.astype(BF16), nt,
                                                  preferred_element_type=F32), 0.0)

    def level_scores(a, blk, q_arg, k_arg):
        r = lax.dot_general((q * jnp.exp2(q_arg)).astype(BF16), (k * jnp.exp2(k_arg)).astype(BF16),
                            nt, preferred_element_type=F32)
        return jnp.where(jnp.logical_and(xr >= blk, xr < 2 * blk), r, a)

    blk = 1
    while blk < SUBLANES:
        hi = jnp.bitwise_and(ri, blk) != 0
        a = level_scores(a, blk, jnp.where(hi, pre, suf), jnp.where(hi, tb - suf, tf - pre))
        tf_dn, tf_up = pltpu.roll(tf, blk, 0), pltpu.roll(tf, c - blk, 0)
        tb_dn, tb_up = pltpu.roll(tb, blk, 0), pltpu.roll(tb, c - blk, 0)
        pre = pre + jnp.where(hi, tf_dn, 0.0)
        suf = suf + jnp.where(hi, 0.0, tb_up)
        tf = tf + jnp.where(hi, tf_dn, tf_up)
        tb = tb + jnp.where(hi, tb_dn, tb_up)
        blk *= 2
    ns = c // SUBLANES
    slabs = lambda x: [x[SUBLANES * r:SUBLANES * (r + 1)] for r in range(ns)]
    pre, suf, tf, tb = slabs(pre), slabs(suf), slabs(tf), slabs(tb)
    while blk < c:
        g = blk // SUBLANES
        up = [bool(r & g) for r in range(ns)]
        q_arg = [pre[r] if up[r] else suf[r] for r in range(ns)]
        k_arg = [tb[r] - suf[r] if up[r] else tf[r] - pre[r] for r in range(ns)]
        a = level_scores(a, blk, jnp.concatenate(q_arg, axis=0), jnp.concatenate(k_arg, axis=0))
        pre = [pre[r] + tf[r - g] if up[r] else pre[r] for r in range(ns)]
        suf = [suf[r] if up[r] else suf[r] + tb[r + g] for r in range(ns)]
        tf_pair = {r: tf[r] + tf[r + g] for r in range(ns) if not up[r]}
        tb_pair = {r: tb[r] + tb[r + g] for r in range(ns) if not up[r]}
        tf = [tf_pair[r - g] if up[r] else tf_pair[r] for r in range(ns)]
        tb = [tb_pair[r - g] if up[r] else tb_pair[r] for r in range(ns)]
        blk *= 2
    cat = lambda x: jnp.concatenate(x, axis=0)
    return a, cat(pre), cat(tf), cat(suf), cat(tb)


def _gla_fwd_chunk(q, k, v, gf, gb, s_scr, h):
    a, pre, tf, suf, tb = _gla_chunk_scores(q, k, gf, gb)
    st = s_scr[h]
    o = (jnp.dot(a.astype(BF16), v, preferred_element_type=F32)
         + jnp.dot((q * jnp.exp2(pre)).astype(BF16), st.astype(BF16), preferred_element_type=F32))
    kd = k * jnp.exp2(tf - pre)
    s_scr[h] = st * jnp.exp2(tf).T[:, 0:1] + jnp.dot(kd.T.astype(BF16), v, preferred_element_type=F32)
    return o, (q * jnp.exp2(suf)).astype(BF16), (k * jnp.exp2(tb - suf)).astype(BF16), tb[0:SUBLANES]


def _gla_fwd_kernel(q_ref, k_ref, v_ref, lr_ref, wgf_ref, bgf_ref, wgb_ref, bgb_ref,
                    o_ref, qd_ref, kd_ref, tot_ref, s_scr):
    @pl.when(pl.program_id(1) == 0)
    def _():
        s_scr[...] = jnp.zeros(s_scr.shape, F32)

    lr = lr_ref[0]
    gf = _log_gate(lr, wgf_ref, bgf_ref)
    gb = _log_gate(lr, wgb_ref, bgb_ref)
    for h in range(N_HEADS_B):
        sl = slice(h * KEY_DIM_B, (h + 1) * KEY_DIM_B)
        vs = slice(h * V_DIM_B, (h + 1) * V_DIM_B)
        q = q_ref[0, :, sl].astype(F32) * KEY_DIM_B ** -0.5
        k = k_ref[0, :, sl].astype(F32)
        o_ref[0, :, vs], qd_ref[0, :, sl], kd_ref[0, :, sl], tot_ref[0, 0, :, sl] = _gla_fwd_chunk(
            q, k, v_ref[0, :, vs], gf[:, sl], gb[:, sl], s_scr, h)


def _gla(main3, lr3, wgf, bgf, wgb, bgb, gn):
    b, s, _ = main3.shape
    c = GLA_CHUNK
    nc = s // c
    hk = N_HEADS_B * KEY_DIM_B
    state = pltpu.VMEM((N_HEADS_B, KEY_DIM_B, V_DIM_B), F32)
    full = lambda shape: pl.BlockSpec(shape, lambda bi, i: (0,) * len(shape))
    fwd = lambda w, j: pl.BlockSpec((1, c, w), lambda bi, i: (bi, i, j))
    of3, qd, kd, tot = pl.pallas_call(
        _gla_fwd_kernel,
        grid=(b, nc),
        in_specs=[fwd(hk, 0), fwd(hk, 1), fwd(D_MODEL, 1), fwd(LR_W, 0),
                  full((LR_W, hk)), full((1, hk)), full((LR_W, hk)), full((1, hk))],
        out_specs=[fwd(D_MODEL, 0), fwd(hk, 0), fwd(hk, 0),
                   pl.BlockSpec((1, 1, SUBLANES, hk), lambda bi, i: (bi, i, 0, 0))],
        out_shape=[jax.ShapeDtypeStruct((b, s, D_MODEL), F32),
                   jax.ShapeDtypeStruct((b, s, hk), BF16),
                   jax.ShapeDtypeStruct((b, s, hk), BF16),
                   jax.ShapeDtypeStruct((b, nc, SUBLANES, hk), F32)],
        scratch_shapes=[state],
        compiler_params=_cparams(("parallel", "arbitrary")),
        name="gla_fwd",
    )(main3, main3, main3, lr3, wgf, bgf, wgb, bgb)

    cps = math.gcd(nc, GLA_BWD_CHUNKS)
    rev = lambda i: nc // cps - 1 - i
    blk = lambda w, j: pl.BlockSpec((1, cps * c, w), lambda bi, i: (bi, rev(i), j))
    return pl.pallas_call(
        _gla_bwd_kernel,
        grid=(b, nc // cps),
        in_specs=[blk(hk, 0), blk(hk, 0),
                  pl.BlockSpec((1, cps, SUBLANES, hk), lambda bi, i: (bi, rev(i), 0, 0)),
                  blk(D_MODEL, 1), blk(D_MODEL, 2), blk(D_MODEL, 0), full((1, V_DIM_B))],
        out_specs=blk(D_MODEL, 0),
        out_shape=jax.ShapeDtypeStruct((b, s, D_MODEL), BF16),
        scratch_shapes=[state],
        compiler_params=_cparams(("parallel", "arbitrary")),
        name="gla_bwd",
    )(qd, kd, tot, main3, main3, of3, gn)


def _tail_kernel(x_ref, oa_ref, ob_ref, ga_ref, gb_ref, mod_ref, g2_ref, wa_ref, wb_ref, wo_ref,
                 wu_ref, wd_ref, o_ref):
    ya = jnp.dot(oa_ref[...], wa_ref[...], preferred_element_type=F32)
    yb = jnp.dot(ob_ref[...], wb_ref[...], preferred_element_type=F32)
    sig = lambda r: 1.0 / (1.0 + jnp.exp(-r[...].astype(F32)))
    merged = sig(ga_ref) * ya + sig(gb_ref) * yb
    x1 = x_ref[...] + mod_ref[0, 2:3, :] * jnp.dot(merged.astype(BF16), wo_ref[...],
                                                   preferred_element_type=F32)
    h = _modulated_norm(x1, g2_ref[...], mod_ref[0, 3:4, :], mod_ref[0, 4:5, :]).astype(BF16)
    acc = jnp.zeros(x1.shape, F32)
    for j in range(D_FF // D_MODEL):
        cs = slice(j * D_MODEL, (j + 1) * D_MODEL)
        u = jnp.maximum(jnp.dot(h, wu_ref[:, cs], preferred_element_type=F32), 0.0)
        acc = acc + jnp.dot((u * u).astype(BF16), wd_ref[cs, :], preferred_element_type=F32)
    o_ref[...] = x1 + mod_ref[0, 5:6, :] * acc


def _tail(x2, oa2, ob2, main2, mod8, g2, wa, wb, wo, wu, wd, seq):
    t = x2.shape[0]
    tm = ROW_TILE
    row = lambda j: pl.BlockSpec((tm, D_MODEL), lambda i: (i, j))
    once = lambda shape: pl.BlockSpec(shape, lambda i: (0,) * len(shape), pipeline_mode=pl.Buffered(1))
    return pl.pallas_call(
        _tail_kernel,
        grid=(t // tm,),
        in_specs=[row(0), row(0), row(0), row(3), row(4),
                  pl.BlockSpec((1, SUBLANES, D_MODEL), lambda i: ((i * tm) // seq, 0, 0)),
                  once((1, D_MODEL)),
                  once((D_MODEL, D_MODEL)), once((D_MODEL, D_MODEL)), once((D_MODEL, D_MODEL)),
                  once((D_MODEL, D_FF)), once((D_FF, D_MODEL))],
        out_specs=row(0),
        out_shape=jax.ShapeDtypeStruct((t, D_MODEL), F32),
        compiler_params=_cparams(("parallel",)),
        name="tail",
    )(x2, oa2, ob2, main2, main2, mod8, g2, wa, wb, wo, wu, wd)


def _split_w_in(w_in):
    o = np.cumsum([0, 1024, 1024, 1024, 512, 512, 1024, 1024, GATE_RANK, GATE_RANK, 1024, 1024])
    main = jnp.concatenate([w_in[:, o[0]:o[7]], w_in[:, o[9]:o[11]]], axis=1)
    lr = jnp.pad(w_in[:, o[7]:o[9]], ((0, 0), (0, LR_W - 2 * GATE_RANK)))
    return main.astype(BF16), lr.astype(BF16)


def _pad_gate_w(w_gate, row0):
    return jnp.pad(w_gate, ((row0, LR_W - GATE_RANK - row0), (0, 0))).astype(BF16)


def _layer(x, mod8, layer_idx, p):
    b, s, _ = x.shape
    t = b * s
    lambda_init = 0.8 - 0.6 * math.exp(-0.3 * layer_idx)
    x2 = x.reshape(t, D_MODEL)
    row = lambda v: v.reshape(1, -1).astype(F32)

    qg2, kg2 = row(jnp.tile(p["q_norm_g"], 2)), row(jnp.tile(p["k_norm_g"], 2))
    qt, kn2, vt, main2, lr2 = _proj(x2, mod8, row(p["norm1_g"]), p["w_main"], p["w_lr"], qg2, kg2, s)
    kn = kn2.reshape(b, s, D_MODEL)
    main3 = main2.reshape(b, s, N_REST)
    lr3 = lr2.reshape(b, s, LR_W)
    lamv = jnp.zeros((SUBLANES, HEAD_W_A), F32).at[0:4, :HEAD_DIM_A].set(
        jnp.stack([p["lam_q1"], p["lam_k1"], p["lam_q2"], p["lam_k2"]]).astype(F32))
    bias_t, range_ok = _bias_tiles(p["rel_bias"], qg2, kg2)
    oa = _attn(range_ok, qt, kn, vt, bias_t, lamv, row(p["subln_g"]), lambda_init)

    ob = _gla(main3, lr3, p["wgf"], row(p["b_gate_f"]), p["wgb"], row(p["b_gate_b"]), row(p["gla_norm_g"]))

    y = _tail(x2, oa.reshape(t, D_MODEL), ob.reshape(t, D_MODEL), main2, mod8, row(p["norm2_g"]),
              p["w_branch_a"], p["w_branch_b"], p["w_out"], p["w_up"], p["w_down"], s)
    return y.reshape(b, s, D_MODEL)


def kernel(x_prompt, x_sample, c_prompt, c_sample, rel_bias, w_ada, b_ada, norm1_g, w_in, q_norm_g, k_norm_g, lam_q1, lam_k1, lam_q2, lam_k2, subln_g, w_gate_f, b_gate_f, w_gate_b, b_gate_b, gla_norm_g, w_branch_a, w_branch_b, w_out, norm2_g, w_up, w_down):
    depth = w_in.shape[0]
    nb_p, nb_s = c_prompt.shape[0], c_sample.shape[0]
    rows = -(-(nb_p + nb_s) // SUBLANES) * SUBLANES
    c_all = jnp.pad(jnp.concatenate([c_prompt, c_sample], axis=0), ((0, rows - nb_p - nb_s), (0, 0)))

    xp, xs = x_prompt, x_sample
    for l in range(depth):
        mod = _ada(c_all, w_ada[l], b_ada[l]).reshape(rows, 6, D_MODEL)
        mod8 = jnp.pad(mod, ((0, 0), (0, SUBLANES - 6), (0, 0)))
        w_main, w_lr = _split_w_in(w_in[l])
        p = dict(rel_bias=rel_bias.astype(F32), norm1_g=norm1_g[l], w_main=w_main, w_lr=w_lr,
                 q_norm_g=q_norm_g[l], k_norm_g=k_norm_g[l], lam_q1=lam_q1[l], lam_k1=lam_k1[l],
                 lam_q2=lam_q2[l], lam_k2=lam_k2[l], subln_g=subln_g[l],
                 wgf=_pad_gate_w(w_gate_f[l], 0), b_gate_f=b_gate_f[l],
                 wgb=_pad_gate_w(w_gate_b[l], GATE_RANK), b_gate_b=b_gate_b[l],
                 gla_norm_g=gla_norm_g[l], w_branch_a=w_branch_a[l].astype(BF16),
                 w_branch_b=w_branch_b[l].astype(BF16), w_out=w_out[l].astype(BF16),
                 norm2_g=norm2_g[l], w_up=w_up[l].astype(BF16), w_down=w_down[l].astype(BF16))
        xp = _layer(xp, mod8[:nb_p], l, p)
        xs = _layer(xs, mod8[nb_p:nb_p + nb_s], l, p)
    return (xp, xs)
```

```python
import functools
import math

import jax
import jax.numpy as jnp
import numpy as np
from jax import lax
from jax.experimental import pallas as pl
from jax.experimental.pallas import tpu as pltpu

F32 = jnp.float32
BF16 = jnp.bfloat16

D_MODEL = 1024
HEAD_DIM_A = 64
N_HEADS_A = 8
HEAD_W_A = 2 * HEAD_DIM_A
N_HEADS_B = 4
KEY_DIM_B = 128
V_DIM_B = 256
GATE_RANK = 16
GATE_NORM = 16.0
N_BUCKETS = 32
MAX_DISTANCE = 128
D_FF = 4 * D_MODEL
EPS = 1e-6
LOG2E = math.log2(math.e)

N_MAIN = 8 * D_MODEL
N_REST = 5 * D_MODEL
LR_W = 128

VMEM_LIMIT = 56 * 1024 * 1024
ROW_TILE = 512
ATT_TILE = 256
ATT_BLOCK = 32
NEAR_TILES = 5
EXP2_RANGE = 100.0
GLA_CHUNK = 128
GLA_BWD_CHUNKS = 4
SUBLANES = 8


def _cparams(sem):
    return pltpu.CompilerParams(dimension_semantics=sem, vmem_limit_bytes=VMEM_LIMIT)


def _ada_kernel(c_ref, w_ref, b_ref, o_ref):
    c = c_ref[...]
    a = c / (1.0 + jnp.exp(-c))
    o_ref[...] = jnp.dot(a, w_ref[...], preferred_element_type=F32,
                         precision=lax.Precision.HIGHEST) + b_ref[...]


def _ada(c_all, w_ada, b_ada):
    r = c_all.shape[0]
    tn = 1024
    return pl.pallas_call(
        _ada_kernel,
        grid=(6 * D_MODEL // tn,),
        in_specs=[pl.BlockSpec((r, D_MODEL), lambda n: (0, 0)),
                  pl.BlockSpec((D_MODEL, tn), lambda n: (0, n)),
                  pl.BlockSpec((1, tn), lambda n: (0, n))],
        out_specs=pl.BlockSpec((r, tn), lambda n: (0, n)),
        out_shape=jax.ShapeDtypeStruct((r, 6 * D_MODEL), F32),
        compiler_params=_cparams(("parallel",)),
        name="ada",
    )(c_all, w_ada, b_ada.reshape(1, -1))


def _modulated_norm(x, g, shift, scale):
    ms = jnp.mean(x * x, axis=-1, keepdims=True)
    return x * lax.rsqrt(ms + EPS) * g * (1.0 + scale) + shift


def _pair_rms(x, g):
    lane = lax.broadcasted_iota(jnp.int32, x.shape, 1)
    x2 = x * x
    s_all = jnp.sum(x2, axis=-1, keepdims=True)
    s_lo = jnp.sum(jnp.where(lane < HEAD_DIM_A, x2, 0.0), axis=-1, keepdims=True)
    ss = jnp.where(lane < HEAD_DIM_A, s_lo, s_all - s_lo)
    return x * lax.rsqrt(ss * (1.0 / HEAD_DIM_A) + EPS) * g


def _proj_kernel(x_ref, mod_ref, g_ref, w_ref, wlr_ref, qg_ref, kg_ref,
                 qt_ref, kn_ref, vt_ref, o_ref, lr_ref):
    h = _modulated_norm(x_ref[...], g_ref[...], mod_ref[0, 0:1, :], mod_ref[0, 1:2, :]).astype(BF16)
    lr_ref[...] = jnp.dot(h, wlr_ref[...], preferred_element_type=F32)
    piece = lambda j: jnp.dot(h, w_ref[:, j * D_MODEL:(j + 1) * D_MODEL], preferred_element_type=F32)
    heads = [slice(i * HEAD_W_A, (i + 1) * HEAD_W_A) for i in range(N_HEADS_A)]

    qa = piece(0)
    for i, sl in enumerate(heads):
        qn = _pair_rms(qa[:, sl], qg_ref[...]) * (HEAD_DIM_A ** -0.5 * LOG2E)
        qt_ref[0, i] = qn.T.astype(BF16)
    ka = piece(1)
    for sl in heads:
        kn_ref[:, sl] = _pair_rms(ka[:, sl], kg_ref[...]).astype(BF16)
    va = piece(2)
    for i, sl in enumerate(heads):
        vt_ref[0, i] = va[:, sl].T.astype(BF16)
    for j in range(3, N_MAIN // D_MODEL):
        o_ref[:, (j - 3) * D_MODEL:(j - 2) * D_MODEL] = piece(j).astype(o_ref.dtype)


def _proj(x2, mod8, g1, w_main, w_lr, qg2, kg2, seq):
    t = x2.shape[0]
    tm = ROW_TILE
    spt = seq // tm
    once = pl.Buffered(1)
    const = lambda shape, **kw: pl.BlockSpec(shape, lambda i: (0,) * len(shape), **kw)
    tspec = pl.BlockSpec((1, N_HEADS_A, HEAD_W_A, tm), lambda i: (i // spt, 0, 0, i % spt))
    tshape = jax.ShapeDtypeStruct((t // seq, N_HEADS_A, HEAD_W_A, seq), BF16)
    return pl.pallas_call(
        _proj_kernel,
        grid=(t // tm,),
        in_specs=[pl.BlockSpec((tm, D_MODEL), lambda i: (i, 0)),
                  pl.BlockSpec((1, SUBLANES, D_MODEL), lambda i: (i // spt, 0, 0)),
                  const((1, D_MODEL)),
                  const((D_MODEL, N_MAIN), pipeline_mode=once),
                  const((D_MODEL, LR_W), pipeline_mode=once),
                  const((1, HEAD_W_A)), const((1, HEAD_W_A))],
        out_specs=[tspec,
                   pl.BlockSpec((tm, D_MODEL), lambda i: (i, 0)),
                   tspec,
                   pl.BlockSpec((tm, N_REST), lambda i: (i, 0)),
                   pl.BlockSpec((tm, LR_W), lambda i: (i, 0))],
        out_shape=[tshape,
                   jax.ShapeDtypeStruct((t, D_MODEL), BF16),
                   tshape,
                   jax.ShapeDtypeStruct((t, N_REST), BF16),
                   jax.ShapeDtypeStruct((t, LR_W), F32)],
        compiler_params=_cparams(("parallel",)),
        name="proj",
    )(x2, mod8, g1, w_main, w_lr, qg2, kg2)


def _t5_bucket(rel):
    nb = N_BUCKETS // 2
    max_exact = nb // 2
    ret = (rel > 0).astype(jnp.int32) * nb
    n = jnp.abs(rel)
    nf = jnp.maximum(n, 1).astype(F32)
    large = max_exact + (jnp.log(nf / max_exact) / math.log(MAX_DISTANCE / max_exact)
                         * (nb - max_exact)).astype(jnp.int32)
    large = jnp.minimum(large, nb - 1)
    return ret + jnp.where(n < max_exact, n, large)


def _bias_kernel(tbl_ref, bkt_ref, qg_ref, kg_ref, o_ref, ok_ref):
    h = pl.program_id(0)
    bkt = bkt_ref[1:NEAR_TILES - 1]
    acc = jnp.zeros(bkt.shape, F32)
    babs = jnp.float32(0.0)
    for i in range(N_BUCKETS):
        acc = jnp.where(bkt == i, tbl_ref[i, h], acc)
        babs = jnp.maximum(babs, jnp.abs(tbl_ref[i, h]))
    o_ref[0, 1:NEAR_TILES - 1] = acc * LOG2E
    o_ref[0, 0] = jnp.full(o_ref.shape[2:], tbl_ref[N_BUCKETS // 2 - 1, h] * LOG2E, F32)
    o_ref[0, NEAR_TILES - 1] = jnp.full(o_ref.shape[2:], tbl_ref[N_BUCKETS - 1, h] * LOG2E, F32)
    gmax = lambda r: jnp.max(jnp.abs(r[...]), axis=1, keepdims=True)
    bound = gmax(qg_ref) * gmax(kg_ref) * (1.02 * HEAD_DIM_A * HEAD_DIM_A ** -0.5 * LOG2E) + babs * LOG2E
    ok_ref[0] = jnp.broadcast_to((bound <= EXP2_RANGE).astype(jnp.int32), ok_ref.shape[1:])


def _bias_tiles(rel_bias, qg2, kg2):
    t = ATT_TILE
    kk = jnp.arange(t, dtype=jnp.int32)[:, None]
    qq = jnp.arange(t, dtype=jnp.int32)[None, :]
    rel = jnp.stack([(d - NEAR_TILES // 2) * t + kk - qq for d in range(NEAR_TILES)])
    gspec = pl.BlockSpec((1, HEAD_W_A), lambda h: (0, 0))
    tiles, ok = pl.pallas_call(
        _bias_kernel,
        grid=(N_HEADS_A,),
        in_specs=[pl.BlockSpec(memory_space=pltpu.SMEM),
                  pl.BlockSpec((NEAR_TILES, t, t), lambda h: (0, 0, 0)), gspec, gspec],
        out_specs=[pl.BlockSpec((1, NEAR_TILES, t, t), lambda h: (h, 0, 0, 0)),
                   pl.BlockSpec((1, SUBLANES, HEAD_W_A), lambda h: (h, 0, 0))],
        out_shape=[jax.ShapeDtypeStruct((N_HEADS_A, NEAR_TILES, t, t), F32),
                   jax.ShapeDtypeStruct((N_HEADS_A, SUBLANES, HEAD_W_A), jnp.int32)],
        compiler_params=_cparams(("parallel",)),
        name="bias",
    )(rel_bias, _t5_bucket(rel), qg2, kg2)
    return tiles, ok[:, 0, 0]


def _attn_kernel(ok_ref, qt_ref, k_ref, vt_ref, bias_ref, lamv_ref, subg_ref, o_ref,
                 w_scr, m_scr, l_scr, acc_scr, l8_scr, *, nk, qsub, lambda_init):
    t = ATT_TILE
    blk = min(ATT_BLOCK, nk)
    bt = blk * t
    h = pl.program_id(1)
    tiles = [(u, pl.program_id(2) * qsub + u) for u in range(qsub)]

    z = jnp.zeros((HEAD_DIM_A, t), BF16)
    for u, _ in tiles:
        q = qt_ref[0, 0, :, u * t:(u + 1) * t]
        w_scr[u, :, :t] = jnp.concatenate([q[:HEAD_DIM_A], z], axis=0)
        w_scr[u, :, t:] = jnp.concatenate([z, q[HEAD_DIM_A:]], axis=0)

    def chunk_bias(kj, qi):
        b = bias_ref[0, jnp.clip(kj - qi, -2, 2) + 2]
        return jnp.concatenate([b, b], axis=1)

    @pl.when(ok_ref[h] != 0)
    def _():
        l8_scr[...] = jnp.zeros(l8_scr.shape, F32)
        acc_scr[...] = jnp.zeros(acc_scr.shape, F32)

        def block(j, carry):
            k0 = pl.multiple_of(j * bt, bt)
            for u, qi in tiles:
                s = jnp.dot(k_ref[0, pl.ds(k0, bt), :], w_scr[u], preferred_element_type=F32)
                p = [jnp.exp2(s[c * t:(c + 1) * t] + chunk_bias(j * blk + c, qi)) for c in range(blk)]
                p = jnp.concatenate(p, axis=0)
                l8_scr[u] += jnp.sum(p.reshape(bt // SUBLANES, SUBLANES, 2 * t), axis=0)
                acc_scr[u] += jnp.dot(vt_ref[0, 0, :, pl.ds(k0, bt)], p.astype(BF16),
                                      preferred_element_type=F32)
            return carry

        lax.fori_loop(0, nk // blk, block, 0)
        l_scr[...] = jnp.sum(l8_scr[...], axis=1, keepdims=True)

    @pl.when(ok_ref[h] == 0)
    def _():
        m_scr[...] = jnp.full(m_scr.shape, -jnp.inf, F32)
        l_scr[...] = jnp.zeros(l_scr.shape, F32)
        acc_scr[...] = jnp.zeros(acc_scr.shape, F32)

        def online_step(kj, carry):
            k0 = pl.multiple_of(kj * t, t)
            for u, qi in tiles:
                s = jnp.dot(k_ref[0, pl.ds(k0, t), :], w_scr[u], preferred_element_type=F32)
                s = s + chunk_bias(kj, qi)
                m_old = m_scr[u]
                m_new = jnp.maximum(m_old, jnp.max(s, axis=0, keepdims=True))
                alpha = jnp.exp2(m_old - m_new)
                p = jnp.exp2(s - m_new)
                l_scr[u] = alpha * l_scr[u] + jnp.sum(p, axis=0, keepdims=True)
                pv = jnp.dot(vt_ref[0, 0, :, pl.ds(k0, t)], p.astype(BF16), preferred_element_type=F32)
                acc_scr[u] = alpha * acc_scr[u] + pv
                m_scr[u] = m_new
            return carry

        lax.fori_loop(0, nk, online_step, 0)

    lv = lamv_ref[...]
    lam = (jnp.exp(jnp.sum(lv[0:1] * lv[1:2], axis=-1, keepdims=True))
           - jnp.exp(jnp.sum(lv[2:3] * lv[3:4], axis=-1, keepdims=True)) + lambda_init)
    for u, _ in tiles:
        acc = acc_scr[u]
        inv = 1.0 / l_scr[u]
        o = acc[:, :t] * inv[:, :t] - lam * (acc[:, t:] * inv[:, t:])
        ms = jnp.mean(o * o, axis=0, keepdims=True)
        y = (o * lax.rsqrt(ms + EPS)).T
        o_ref[0, u * t:(u + 1) * t, :] = (y * subg_ref[...] * (1.0 - lambda_init)).astype(o_ref.dtype)


def _attn(range_ok, qt, kn, vt, bias_t, lamv, subg, lambda_init):
    b, s, _ = kn.shape
    t = ATT_TILE
    nk = s // t
    assert nk % min(ATT_BLOCK, nk) == 0 and t >= MAX_DISTANCE
    qsub = 8 if nk <= ATT_BLOCK and nk % 8 == 0 else 2
    kern = functools.partial(_attn_kernel, nk=nk, qsub=qsub, lambda_init=lambda_init)
    const = lambda shape: pl.BlockSpec(shape, lambda bi, h, qi: (0,) * len(shape))
    tq = qsub * t
    return pl.pallas_call(
        kern,
        grid=(b, N_HEADS_A, s // tq),
        in_specs=[pl.BlockSpec(memory_space=pltpu.SMEM),
                  pl.BlockSpec((1, 1, HEAD_W_A, tq), lambda bi, h, qi: (bi, h, 0, qi)),
                  pl.BlockSpec((1, s, HEAD_W_A), lambda bi, h, qi: (bi, 0, h)),
                  pl.BlockSpec((1, 1, HEAD_W_A, s), lambda bi, h, qi: (bi, h, 0, 0)),
                  pl.BlockSpec((1, NEAR_TILES, t, t), lambda bi, h, qi: (h, 0, 0, 0)),
                  const((SUBLANES, HEAD_W_A)), const((1, HEAD_W_A))],
        out_specs=pl.BlockSpec((1, tq, HEAD_W_A), lambda bi, h, qi: (bi, qi, h)),
        out_shape=jax.ShapeDtypeStruct((b, s, D_MODEL), BF16),
        scratch_shapes=[pltpu.VMEM((qsub, HEAD_W_A, 2 * t), BF16),
                        pltpu.VMEM((qsub, 1, 2 * t), F32),
                        pltpu.VMEM((qsub, 1, 2 * t), F32),
                        pltpu.VMEM((qsub, HEAD_W_A, 2 * t), F32),
                        pltpu.VMEM((qsub, SUBLANES, 2 * t), F32)],
        compiler_params=_cparams(("parallel", "parallel", "parallel")),
        name="attn",
    )(range_ok, qt, kn, vt, bias_t, lamv, subg)


def _log_gate(lr, wg_ref, bg_ref):
    x = jnp.dot(lr.astype(BF16), wg_ref[...], preferred_element_type=F32) + bg_ref[...]
    xl = x * LOG2E
    return (jnp.minimum(xl, 0.0) - jnp.log2(1.0 + jnp.exp2(-jnp.abs(xl)))) * (1.0 / GATE_NORM)


def _gla_bwd_kernel(qd_ref, kd_ref, tot_ref, v_ref, og_ref, of_ref, gn_ref, o_ref, s_scr):
    c = GLA_CHUNK

    @pl.when(pl.program_id(1) == 0)
    def _():
        s_scr[...] = jnp.zeros(s_scr.shape, F32)

    for j in reversed(range(qd_ref.shape[1] // c)):
        rows = slice(j * c, (j + 1) * c)
        for h in range(N_HEADS_B):
            sl = slice(h * KEY_DIM_B, (h + 1) * KEY_DIM_B)
            vs = slice(h * V_DIM_B, (h + 1) * V_DIM_B)
            st = s_scr[h]
            o = jnp.dot(qd_ref[0, rows, sl], st.astype(BF16), preferred_element_type=F32) + of_ref[0, rows, vs]
            dec = jnp.broadcast_to(jnp.exp2(tot_ref[0, j, 0:1, sl]), (c, KEY_DIM_B)).T[:, 0:1]
            s_scr[h] = st * dec + jnp.dot(kd_ref[0, rows, sl].astype(F32).T.astype(BF16), v_ref[0, rows, vs],
                                          preferred_element_type=F32)
            ms = jnp.mean(o * o, axis=-1, keepdims=True)
            og = og_ref[0, rows, vs].astype(F32)
            o_ref[0, rows, vs] = (o * lax.rsqrt(ms + EPS) * gn_ref[...] * (og / (1.0 + jnp.exp(-og)))
                                  ).astype(o_ref.dtype)


def _gla_chunk_scores(q, k, gf, gb):
    c = GLA_CHUNK
    ri = lax.broadcasted_iota(jnp.int32, (c, c), 0)
    ci = lax.broadcasted_iota(jnp.int32, (c, c), 1)
    xr = jnp.bitwise_xor(ri, ci)
    nt = (((1,), (1,)), ((), ()))
    pre, tf, suf, tb = gf, gf, gb, gb
    a = jnp.where(ri == ci, 2.0 * lax.dot_general(q.astype(BF16), k.astype(BF16), nt,
                                                  preferred_element_type=F32), 0.0)

    def level_scores(a, blk, q_arg, k_arg):
        r = lax.dot_general((q * jnp.exp2(q_arg)).astype(BF16), (k * jnp.exp2(k_arg)).astype(BF16),
                            nt, preferred_element_type=F32)
        return jnp.where(jnp.logical_and(xr >= blk, xr < 2 * blk), r, a)

    blk = 1
    while blk < SUBLANES:
        hi = jnp.bitwise_and(ri, blk) != 0
        a = level_scores(a, blk, jnp.where(hi, pre, suf), jnp.where(hi, tb - suf, tf - pre))
        tf_dn, tf_up = pltpu.roll(tf, blk, 0), pltpu.roll(tf, c - blk, 0)
        tb_dn, tb_up = pltpu.roll(tb, blk, 0), pltpu.roll(tb, c - blk, 0)
        pre = pre + jnp.where(hi, tf_dn, 0.0)
        suf = suf + jnp.where(hi, 0.0, tb_up)
        tf = tf + jnp.where(hi, tf_dn, tf_up)
        tb = tb + jnp.where(hi, tb_dn, tb_up)
        blk *= 2
    ns = c // SUBLANES
    slabs = lambda x: [x[SUBLANES * r:SUBLANES * (r + 1)] for r in range(ns)]
    pre, suf, tf, tb = slabs(pre), slabs(suf), slabs(tf), slabs(tb)
    while blk < c:
        g = blk // SUBLANES
        up = [bool(r & g) for r in range(ns)]
        q_arg = [pre[r] if up[r] else suf[r] for r in range(ns)]
        k_arg = [tb[r] - suf[r] if up[r] else tf[r] - pre[r] for r in range(ns)]
        a = level_scores(a, blk, jnp.concatenate(q_arg, axis=0), jnp.concatenate(k_arg, axis=0))
        pre = [pre[r] + tf[r - g] if up[r] else pre[r] for r in range(ns)]
        suf = [suf[r] if up[r] else suf[r] + tb[r + g] for r in range(ns)]
        tf_pair = {r: tf[r] + tf[r + g] for r in range(ns) if not up[r]}
        tb_pair = {r: tb[r] + tb[r + g] for r in range(ns) if not up[r]}
        tf = [tf_pair[r - g] if up[r] else tf_pair[r] for r in range(ns)]
        tb = [tb_pair[r - g] if up[r] else tb_pair[r] for r in range(ns)]
        blk *= 2
    cat = lambda x: jnp.concatenate(x, axis=0)
    return a, cat(pre), cat(tf), cat(suf), cat(tb)


def _gla_fwd_chunk(q, k, v, gf, gb, s_scr, h):
    a, pre, tf, suf, tb = _gla_chunk_scores(q, k, gf, gb)
    st = s_scr[h]
    o = (jnp.dot(a.astype(BF16), v, preferred_element_type=F32)
         + jnp.dot((q * jnp.exp2(pre)).astype(BF16), st.astype(BF16), preferred_element_type=F32))
    kd = k * jnp.exp2(tf - pre)
    s_scr[h] = st * jnp.exp2(tf).T[:, 0:1] + jnp.dot(kd.T.astype(BF16), v, preferred_element_type=F32)
    return o, (q * jnp.exp2(suf)).astype(BF16), (k * jnp.exp2(tb - suf)).astype(BF16), tb[0:SUBLANES]


def _gla_fwd_kernel(q_ref, k_ref, v_ref, lr_ref, wgf_ref, bgf_ref, wgb_ref, bgb_ref,
                    o_ref, qd_ref, kd_ref, tot_ref, s_scr):
    @pl.when(pl.program_id(1) == 0)
    def _():
        s_scr[...] = jnp.zeros(s_scr.shape, F32)

    lr = lr_ref[0]
    gf = _log_gate(lr, wgf_ref, bgf_ref)
    gb = _log_gate(lr, wgb_ref, bgb_ref)
    for h in range(N_HEADS_B):
        sl = slice(h * KEY_DIM_B, (h + 1) * KEY_DIM_B)
        vs = slice(h * V_DIM_B, (h + 1) * V_DIM_B)
        q = q_ref[0, :, sl].astype(F32) * KEY_DIM_B ** -0.5
        k = k_ref[0, :, sl].astype(F32)
        o_ref[0, :, vs], qd_ref[0, :, sl], kd_ref[0, :, sl], tot_ref[0, 0, :, sl] = _gla_fwd_chunk(
            q, k, v_ref[0, :, vs], gf[:, sl], gb[:, sl], s_scr, h)


def _gla(main3, lr3, wgf, bgf, wgb, bgb, gn):
    b, s, _ = main3.shape
    c = GLA_CHUNK
    nc = s // c
    hk = N_HEADS_B * KEY_DIM_B
    state = pltpu.VMEM((N_HEADS_B, KEY_DIM_B, V_DIM_B), F32)
    full = lambda shape: pl.BlockSpec(shape, lambda bi, i: (0,) * len(shape))
    fwd = lambda w, j: pl.BlockSpec((1, c, w), lambda bi, i: (bi, i, j))
    of3, qd, kd, tot = pl.pallas_call(
        _gla_fwd_kernel,
        grid=(b, nc),
        in_specs=[fwd(hk, 0), fwd(hk, 1), fwd(D_MODEL, 1), fwd(LR_W, 0),
                  full((LR_W, hk)), full((1, hk)), full((LR_W, hk)), full((1, hk))],
        out_specs=[fwd(D_MODEL, 0), fwd(hk, 0), fwd(hk, 0),
                   pl.BlockSpec((1, 1, SUBLANES, hk), lambda bi, i: (bi, i, 0, 0))],
        out_shape=[jax.ShapeDtypeStruct((b, s, D_MODEL), F32),
                   jax.ShapeDtypeStruct((b, s, hk), BF16),
                   jax.ShapeDtypeStruct((b, s, hk), BF16),
                   jax.ShapeDtypeStruct((b, nc, SUBLANES, hk), F32)],
        scratch_shapes=[state],
        compiler_params=_cparams(("parallel", "arbitrary")),
        name="gla_fwd",
    )(main3, main3, main3, lr3, wgf, bgf, wgb, bgb)

    cps = math.gcd(nc, GLA_BWD_CHUNKS)
    rev = lambda i: nc // cps - 1 - i
    blk = lambda w, j: pl.BlockSpec((1, cps * c, w), lambda bi, i: (bi, rev(i), j))
    return pl.pallas_call(
        _gla_bwd_kernel,
        grid=(b, nc // cps),
        in_specs=[blk(hk, 0), blk(hk, 0),
                  pl.BlockSpec((1, cps, SUBLANES, hk), lambda bi, i: (bi, rev(i), 0, 0)),
                  blk(D_MODEL, 1), blk(D_MODEL, 2), blk(D_MODEL, 0), full((1, V_DIM_B))],
        out_specs=blk(D_MODEL, 0),
        out_shape=jax.ShapeDtypeStruct((b, s, D_MODEL), BF16),
        scratch_shapes=[state],
        compiler_params=_cparams(("parallel", "arbitrary")),
        name="gla_bwd",
    )(qd, kd, tot, main3, main3, of3, gn)


def _tail_kernel(x_ref, oa_ref, ob_ref, ga_ref, gb_ref, mod_ref, g2_ref, wa_ref, wb_ref, wo_ref,
                 wu_ref, wd_ref, o_ref):
    ya = jnp.dot(oa_ref[...], wa_ref[...], preferred_element_type=F32)
    yb = jnp.dot(ob_ref[...], wb_ref[...], preferred_element_type=F32)
    sig = lambda r: 1.0 / (1.0 + jnp.exp(-r[...].astype(F32)))
    merged = sig(ga_ref) * ya + sig(gb_ref) * yb
    x1 = x_ref[...] + mod_ref[0, 2:3, :] * jnp.dot(merged.astype(BF16), wo_ref[...],
                                                   preferred_element_type=F32)
    h = _modulated_norm(x1, g2_ref[...], mod_ref[0, 3:4, :], mod_ref[0, 4:5, :]).astype(BF16)
    acc = jnp.zeros(x1.shape, F32)
    for j in range(D_FF // D_MODEL):
        cs = slice(j * D_MODEL, (j + 1) * D_MODEL)
        u = jnp.maximum(jnp.dot(h, wu_ref[:, cs], preferred_element_type=F32), 0.0)
        acc = acc + jnp.dot((u * u).astype(BF16), wd_ref[cs, :], preferred_element_type=F32)
    o_ref[...] = x1 + mod_ref[0, 5:6, :] * acc


def _tail(x2, oa2, ob2, main2, mod8, g2, wa, wb, wo, wu, wd, seq):
    t = x2.shape[0]
    tm = ROW_TILE
    row = lambda j: pl.BlockSpec((tm, D_MODEL), lambda i: (i, j))
    once = lambda shape: pl.BlockSpec(shape, lambda i: (0,) * len(shape), pipeline_mode=pl.Buffered(1))
    return pl.pallas_call(
        _tail_kernel,
        grid=(t // tm,),
        in_specs=[row(0), row(0), row(0), row(3), row(4),
                  pl.BlockSpec((1, SUBLANES, D_MODEL), lambda i: ((i * tm) // seq, 0, 0)),
                  once((1, D_MODEL)),
                  once((D_MODEL, D_MODEL)), once((D_MODEL, D_MODEL)), once((D_MODEL, D_MODEL)),
                  once((D_MODEL, D_FF)), once((D_FF, D_MODEL))],
        out_specs=row(0),
        out_shape=jax.ShapeDtypeStruct((t, D_MODEL), F32),
        compiler_params=_cparams(("parallel",)),
        name="tail",
    )(x2, oa2, ob2, main2, main2, mod8, g2, wa, wb, wo, wu, wd)


def _split_w_in(w_in):
    o = np.cumsum([0, 1024, 1024, 1024, 512, 512, 1024, 1024, GATE_RANK, GATE_RANK, 1024, 1024])
    main = jnp.concatenate([w_in[:, o[0]:o[7]], w_in[:, o[9]:o[11]]], axis=1)
    lr = jnp.pad(w_in[:, o[7]:o[9]], ((0, 0), (0, LR_W - 2 * GATE_RANK)))
    return main.astype(BF16), lr.astype(BF16)


def _pad_gate_w(w_gate, row0):
    return jnp.pad(w_gate, ((row0, LR_W - GATE_RANK - row0), (0, 0))).astype(BF16)


def _layer(x, mod8, layer_idx, p):
    b, s, _ = x.shape
    t = b * s
    lambda_init = 0.8 - 0.6 * math.exp(-0.3 * layer_idx)
    x2 = x.reshape(t, D_MODEL)
    row = lambda v: v.reshape(1, -1).astype(F32)

    qg2, kg2 = row(jnp.tile(p["q_norm_g"], 2)), row(jnp.tile(p["k_norm_g"], 2))
    qt, kn2, vt, main2, lr2 = _proj(x2, mod8, row(p["norm1_g"]), p["w_main"], p["w_lr"], qg2, kg2, s)
    kn = kn2.reshape(b, s, D_MODEL)
    main3 = main2.reshape(b, s, N_REST)
    lr3 = lr2.reshape(b, s, LR_W)
    lamv = jnp.zeros((SUBLANES, HEAD_W_A), F32).at[0:4, :HEAD_DIM_A].set(
        jnp.stack([p["lam_q1"], p["lam_k1"], p["lam_q2"], p["lam_k2"]]).astype(F32))
    bias_t, range_ok = _bias_tiles(p["rel_bias"], qg2, kg2)
    oa = _attn(range_ok, qt, kn, vt, bias_t, lamv, row(p["subln_g"]), lambda_init)

    ob = _gla(main3, lr3, p["wgf"], row(p["b_gate_f"]), p["wgb"], row(p["b_gate_b"]), row(p["gla_norm_g"]))

    y = _tail(x2, oa.reshape(t, D_MODEL), ob.reshape(t, D_MODEL), main2, mod8, row(p["norm2_g"]),
              p["w_branch_a"], p["w_branch_b"], p["w_out"], p["w_up"], p["w_down"], s)
    return y.reshape(b, s, D_MODEL)


def kernel(x_prompt, x_sample, c_prompt, c_sample, rel_bias, w_ada, b_ada, norm1_g, w_in, q_norm_g, k_norm_g, lam_q1, lam_k1, lam_q2, lam_k2, subln_g, w_gate_f, b_gate_f, w_gate_b, b_gate_b, gla_norm_g, w_branch_a, w_branch_b, w_out, norm2_g, w_up, w_down):
    depth = w_in.shape[0]
    nb_p, nb_s = c_prompt.shape[0], c_sample.shape[0]
    rows = -(-(nb_p + nb_s) // SUBLANES) * SUBLANES
    c_all = jnp.pad(jnp.concatenate([c_prompt, c_sample], axis=0), ((0, rows - nb_p - nb_s), (0, 0)))

    xp, xs = x_prompt, x_sample
    for l in range(depth):
        mod = _ada(c_all, w_ada[l], b_ada[l]).reshape(rows, 6, D_MODEL)
        mod8 = jnp.pad(mod, ((0, 0), (0, SUBLANES - 6), (0, 0)))
        w_main, w_lr = _split_w_in(w_in[l])
        p = dict(rel_bias=rel_bias.astype(F32), norm1_g=norm1_g[l], w_main=w_main, w_lr=w_lr,
                 q_norm_g=q_norm_g[l], k_norm_g=k_norm_g[l], lam_q1=lam_q1[l], lam_k1=lam_k1[l],
                 lam_q2=lam_q2[l], lam_k2=lam_k2[l], subln_g=subln_g[l],
                 wgf=_pad_gate_w(w_gate_f[l], 0), b_gate_f=b_gate_f[l],
                 wgb=_pad_gate_w(w_gate_b[l], GATE_RANK), b_gate_b=b_gate_b[l],
                 gla_norm_g=gla_norm_g[l], w_branch_a=w_branch_a[l].astype(BF16),
                 w_branch_b=w_branch_b[l].astype(BF16), w_out=w_out[l].astype(BF16),
                 norm2_g=norm2_g[l], w_up=w_up[l].astype(BF16), w_down=w_down[l].astype(BF16))
        xp = _layer(xp, mod8[:nb_p], l, p)
        xs = _layer(xs, mod8[nb_p:nb_p + nb_s], l, p)
    return (xp, xs)
```

```python
import functools
import math

import jax
import jax.numpy as jnp
import numpy as np
from jax import lax
from jax.experimental import pallas as pl
from jax.experimental.pallas import tpu as pltpu

F32 = jnp.float32
BF16 = jnp.bfloat16

D_MODEL = 1024
HEAD_DIM_A = 64
N_HEADS_A = 8
HEAD_W_A = 2 * HEAD_DIM_A
N_HEADS_B = 4
KEY_DIM_B = 128
V_DIM_B = 256
GATE_RANK = 16
GATE_NORM = 16.0
N_BUCKETS = 32
MAX_DISTANCE = 128
D_FF = 4 * D_MODEL
EPS = 1e-6
LOG2E = math.log2(math.e)

N_MAIN = 8 * D_MODEL
N_REST = 5 * D_MODEL
LR_W = 128

VMEM_LIMIT = 56 * 1024 * 1024
ROW_TILE = 512
ATT_TILE = 256
ATT_BLOCK = 32
NEAR_TILES = 5
EXP2_RANGE = 100.0
GLA_CHUNK = 128
GLA_BWD_CHUNKS = 4
SUBLANES = 8


def _cparams(sem):
    return pltpu.CompilerParams(dimension_semantics=sem, vmem_limit_bytes=VMEM_LIMIT)


def _ada_kernel(c_ref, w_ref, b_ref, o_ref):
    c = c_ref[...]
    a = c / (1.0 + jnp.exp(-c))
    o_ref[...] = jnp.dot(a, w_ref[...], preferred_element_type=F32,
                         precision=lax.Precision.HIGHEST) + b_ref[...]


def _ada(c_all, w_ada, b_ada):
    r = c_all.shape[0]
    tn = 1024
    return pl.pallas_call(
        _ada_kernel,
        grid=(6 * D_MODEL // tn,),
        in_specs=[pl.BlockSpec((r, D_MODEL), lambda n: (0, 0)),
                  pl.BlockSpec((D_MODEL, tn), lambda n: (0, n)),
                  pl.BlockSpec((1, tn), lambda n: (0, n))],
        out_specs=pl.BlockSpec((r, tn), lambda n: (0, n)),
        out_shape=jax.ShapeDtypeStruct((r, 6 * D_MODEL), F32),
        compiler_params=_cparams(("parallel",)),
        name="ada",
    )(c_all, w_ada, b_ada.reshape(1, -1))


def _modulated_norm(x, g, shift, scale):
    ms = jnp.mean(x * x, axis=-1, keepdims=True)
    return x * lax.rsqrt(ms + EPS) * g * (1.0 + scale) + shift


def _pair_rms(x, g):
    lane = lax.broadcasted_iota(jnp.int32, x.shape, 1)
    x2 = x * x
    s_all = jnp.sum(x2, axis=-1, keepdims=True)
    s_lo = jnp.sum(jnp.where(lane < HEAD_DIM_A, x2, 0.0), axis=-1, keepdims=True)
    ss = jnp.where(lane < HEAD_DIM_A, s_lo, s_all - s_lo)
    return x * lax.rsqrt(ss * (1.0 / HEAD_DIM_A) + EPS) * g


def _proj_kernel(x_ref, mod_ref, g_ref, w_ref, wlr_ref, qg_ref, kg_ref,
                 qt_ref, kn_ref, vt_ref, o_ref, lr_ref):
    h = _modulated_norm(x_ref[...], g_ref[...], mod_ref[0, 0:1, :], mod_ref[0, 1:2, :]).astype(BF16)
    lr_ref[...] = jnp.dot(h, wlr_ref[...], preferred_element_type=F32)
    piece = lambda j: jnp.dot(h, w_ref[:, j * D_MODEL:(j + 1) * D_MODEL], preferred_element_type=F32)
    heads = [slice(i * HEAD_W_A, (i + 1) * HEAD_W_A) for i in range(N_HEADS_A)]

    qa = piece(0)
    for i, sl in enumerate(heads):
        qn = _pair_rms(qa[:, sl], qg_ref[...]) * (HEAD_DIM_A ** -0.5 * LOG2E)
        qt_ref[0, i] = qn.T.astype(BF16)
    ka = piece(1)
    for sl in heads:
        kn_ref[:, sl] = _pair_rms(ka[:, sl], kg_ref[...]).astype(BF16)
    va = piece(2)
    for i, sl in enumerate(heads):
        vt_ref[0, i] = va[:, sl].T.astype(BF16)
    for j in range(3, N_MAIN // D_MODEL):
        o_ref[:, (j - 3) * D_MODEL:(j - 2) * D_MODEL] = piece(j).astype(o_ref.dtype)


def _proj(x2, mod8, g1, w_main, w_lr, qg2, kg2, seq):
    t = x2.shape[0]
    tm = ROW_TILE
    spt = seq // tm
    once = pl.Buffered(1)
    const = lambda shape, **kw: pl.BlockSpec(shape, lambda i: (0,) * len(shape), **kw)
    tspec = pl.BlockSpec((1, N_HEADS_A, HEAD_W_A, tm), lambda i: (i // spt, 0, 0, i % spt))
    tshape = jax.ShapeDtypeStruct((t // seq, N_HEADS_A, HEAD_W_A, seq), BF16)
    return pl.pallas_call(
        _proj_kernel,
        grid=(t // tm,),
        in_specs=[pl.BlockSpec((tm, D_MODEL), lambda i: (i, 0)),
                  pl.BlockSpec((1, SUBLANES, D_MODEL), lambda i: (i // spt, 0, 0)),
                  const((1, D_MODEL)),
                  const((D_MODEL, N_MAIN), pipeline_mode=once),
                  const((D_MODEL, LR_W), pipeline_mode=once),
                  const((1, HEAD_W_A)), const((1, HEAD_W_A))],
        out_specs=[tspec,
                   pl.BlockSpec((tm, D_MODEL), lambda i: (i, 0)),
                   tspec,
                   pl.BlockSpec((tm, N_REST), lambda i: (i, 0)),
                   pl.BlockSpec((tm, LR_W), lambda i: (i, 0))],
        out_shape=[tshape,
                   jax.ShapeDtypeStruct((t, D_MODEL), BF16),
                   tshape,
                   jax.ShapeDtypeStruct((t, N_REST), BF16),
                   jax.ShapeDtypeStruct((t, LR_W), F32)],
        compiler_params=_cparams(("parallel",)),
        name="proj",
    )(x2, mod8, g1, w_main, w_lr, qg2, kg2)


def _t5_bucket(rel):
    nb = N_BUCKETS // 2
    max_exact = nb // 2
    ret = (rel > 0).astype(jnp.int32) * nb
    n = jnp.abs(rel)
    nf = jnp.maximum(n, 1).astype(F32)
    large = max_exact + (jnp.log(nf / max_exact) / math.log(MAX_DISTANCE / max_exact)
                         * (nb - max_exact)).astype(jnp.int32)
    large = jnp.minimum(large, nb - 1)
    return ret + jnp.where(n < max_exact, n, large)


def _bias_kernel(tbl_ref, bkt_ref, qg_ref, kg_ref, o_ref, ok_ref):
    h = pl.program_id(0)
    bkt = bkt_ref[1:NEAR_TILES - 1]
    acc = jnp.zeros(bkt.shape, F32)
    babs = jnp.float32(0.0)
    for i in range(N_BUCKETS):
        acc = jnp.where(bkt == i, tbl_ref[i, h], acc)
        babs = jnp.maximum(babs, jnp.abs(tbl_ref[i, h]))
    o_ref[0, 1:NEAR_TILES - 1] = acc * LOG2E
    o_ref[0, 0] = jnp.full(o_ref.shape[2:], tbl_ref[N_BUCKETS // 2 - 1, h] * LOG2E, F32)
    o_ref[0, NEAR_TILES - 1] = jnp.full(o_ref.shape[2:], tbl_ref[N_BUCKETS - 1, h] * LOG2E, F32)
    gmax = lambda r: jnp.max(jnp.abs(r[...]), axis=1, keepdims=True)
    bound = gmax(qg_ref) * gmax(kg_ref) * (1.02 * HEAD_DIM_A * HEAD_DIM_A ** -0.5 * LOG2E) + babs * LOG2E
    ok_ref[0] = jnp.broadcast_to((bound <= EXP2_RANGE).astype(jnp.int32), ok_ref.shape[1:])


def _bias_tiles(rel_bias, qg2, kg2):
    t = ATT_TILE
    kk = jnp.arange(t, dtype=jnp.int32)[:, None]
    qq = jnp.arange(t, dtype=jnp.int32)[None, :]
    rel = jnp.stack([(d - NEAR_TILES // 2) * t + kk - qq for d in range(NEAR_TILES)])
    gspec = pl.BlockSpec((1, HEAD_W_A), lambda h: (0, 0))
    tiles, ok = pl.pallas_call(
        _bias_kernel,
        grid=(N_HEADS_A,),
        in_specs=[pl.BlockSpec(memory_space=pltpu.SMEM),
                  pl.BlockSpec((NEAR_TILES, t, t), lambda h: (0, 0, 0)), gspec, gspec],
        out_specs=[pl.BlockSpec((1, NEAR_TILES, t, t), lambda h: (h, 0, 0, 0)),
                   pl.BlockSpec((1, SUBLANES, HEAD_W_A), lambda h: (h, 0, 0))],
        out_shape=[jax.ShapeDtypeStruct((N_HEADS_A, NEAR_TILES, t, t), F32),
                   jax.ShapeDtypeStruct((N_HEADS_A, SUBLANES, HEAD_W_A), jnp.int32)],
        compiler_params=_cparams(("parallel",)),
        name="bias",
    )(rel_bias, _t5_bucket(rel), qg2, kg2)
    return tiles, ok[:, 0, 0]


def _attn_kernel(ok_ref, qt_ref, k_ref, vt_ref, bias_ref, lamv_ref, subg_ref, o_ref,
                 w_scr, m_scr, l_scr, acc_scr, l8_scr, *, nk, qsub, lambda_init):
    t = ATT_TILE
    blk = min(ATT_BLOCK, nk)
    bt = blk * t
    h = pl.program_id(1)
    tiles = [(u, pl.program_id(2) * qsub + u) for u in range(qsub)]

    z = jnp.zeros((HEAD_DIM_A, t), BF16)
    for u, _ in tiles:
        q = qt_ref[0, 0, :, u * t:(u + 1) * t]
        w_scr[u, :, :t] = jnp.concatenate([q[:HEAD_DIM_A], z], axis=0)
        w_scr[u, :, t:] = jnp.concatenate([z, q[HEAD_DIM_A:]], axis=0)

    def chunk_bias(kj, qi):
        b = bias_ref[0, jnp.clip(kj - qi, -2, 2) + 2]
        return jnp.concatenate([b, b], axis=1)

    @pl.when(ok_ref[h] != 0)
    def _():
        l8_scr[...] = jnp.zeros(l8_scr.shape, F32)
        acc_scr[...] = jnp.zeros(acc_scr.shape, F32)

        def block(j, carry):
            k0 = pl.multiple_of(j * bt, bt)
            for u, qi in tiles:
                l8, pb = l8_scr[u], []
                for c in range(blk):
                    kc = k_ref[0, pl.ds(pl.multiple_of(k0 + c * t, t), t), :]
                    p = jnp.exp2(jnp.dot(kc, w_scr[u], preferred_element_type=F32)
                                 + chunk_bias(j * blk + c, qi))
                    l8 = l8 + jnp.sum(p.reshape(t // SUBLANES, SUBLANES, 2 * t), axis=0)
                    pb.append(p.astype(BF16))
                l8_scr[u] = l8
                acc_scr[u] += jnp.dot(vt_ref[0, 0, :, pl.ds(k0, bt)], jnp.concatenate(pb, axis=0),
                                      preferred_element_type=F32)
            return carry

        lax.fori_loop(0, nk // blk, block, 0)
        l_scr[...] = jnp.sum(l8_scr[...], axis=1, keepdims=True)

    @pl.when(ok_ref[h] == 0)
    def _():
        m_scr[...] = jnp.full(m_scr.shape, -jnp.inf, F32)
        l_scr[...] = jnp.zeros(l_scr.shape, F32)
        acc_scr[...] = jnp.zeros(acc_scr.shape, F32)

        def online_step(kj, carry):
            k0 = pl.multiple_of(kj * t, t)
            for u, qi in tiles:
                s = jnp.dot(k_ref[0, pl.ds(k0, t), :], w_scr[u], preferred_element_type=F32)
                s = s + chunk_bias(kj, qi)
                m_old = m_scr[u]
                m_new = jnp.maximum(m_old, jnp.max(s, axis=0, keepdims=True))
                alpha = jnp.exp2(m_old - m_new)
                p = jnp.exp2(s - m_new)
                l_scr[u] = alpha * l_scr[u] + jnp.sum(p, axis=0, keepdims=True)
                pv = jnp.dot(vt_ref[0, 0, :, pl.ds(k0, t)], p.astype(BF16), preferred_element_type=F32)
                acc_scr[u] = alpha * acc_scr[u] + pv
                m_scr[u] = m_new
            return carry

        lax.fori_loop(0, nk, online_step, 0)

    lv = lamv_ref[...]
    lam = (jnp.exp(jnp.sum(lv[0:1] * lv[1:2], axis=-1, keepdims=True))
           - jnp.exp(jnp.sum(lv[2:3] * lv[3:4], axis=-1, keepdims=True)) + lambda_init)
    for u, _ in tiles:
        acc = acc_scr[u]
        inv = 1.0 / l_scr[u]
        o = acc[:, :t] * inv[:, :t] - lam * (acc[:, t:] * inv[:, t:])
        ms = jnp.mean(o * o, axis=0, keepdims=True)
        y = (o * lax.rsqrt(ms + EPS)).T
        o_ref[0, u * t:(u + 1) * t, :] = (y * subg_ref[...] * (1.0 - lambda_init)).astype(o_ref.dtype)


def _attn(range_ok, qt, kn, vt, bias_t, lamv, subg, lambda_init):
    b, s, _ = kn.shape
    t = ATT_TILE
    nk = s // t
    assert nk % min(ATT_BLOCK, nk) == 0 and t >= MAX_DISTANCE
    qsub = 8 if nk <= ATT_BLOCK and nk % 8 == 0 else 2
    kern = functools.partial(_attn_kernel, nk=nk, qsub=qsub, lambda_init=lambda_init)
    const = lambda shape: pl.BlockSpec(shape, lambda bi, h, qi: (0,) * len(shape))
    tq = qsub * t
    return pl.pallas_call(
        kern,
        grid=(b, N_HEADS_A, s // tq),
        in_specs=[pl.BlockSpec(memory_space=pltpu.SMEM),
                  pl.BlockSpec((1, 1, HEAD_W_A, tq), lambda bi, h, qi: (bi, h, 0, qi)),
                  pl.BlockSpec((1, s, HEAD_W_A), lambda bi, h, qi: (bi, 0, h)),
                  pl.BlockSpec((1, 1, HEAD_W_A, s), lambda bi, h, qi: (bi, h, 0, 0)),
                  pl.BlockSpec((1, NEAR_TILES, t, t), lambda bi, h, qi: (h, 0, 0, 0)),
                  const((SUBLANES, HEAD_W_A)), const((1, HEAD_W_A))],
        out_specs=pl.BlockSpec((1, tq, HEAD_W_A), lambda bi, h, qi: (bi, qi, h)),
        out_shape=jax.ShapeDtypeStruct((b, s, D_MODEL), BF16),
        scratch_shapes=[pltpu.VMEM((qsub, HEAD_W_A, 2 * t), BF16),
                        pltpu.VMEM((qsub, 1, 2 * t), F32),
                        pltpu.VMEM((qsub, 1, 2 * t), F32),
                        pltpu.VMEM((qsub, HEAD_W_A, 2 * t), F32),
                        pltpu.VMEM((qsub, SUBLANES, 2 * t), F32)],
        compiler_params=_cparams(("parallel", "parallel", "parallel")),
        name="attn",
    )(range_ok, qt, kn, vt, bias_t, lamv, subg)


def _log_gate(lr, wg_ref, bg_ref):
    x = jnp.dot(lr.astype(BF16), wg_ref[...], preferred_element_type=F32) + bg_ref[...]
    xl = x * LOG2E
    return (jnp.minimum(xl, 0.0) - jnp.log2(1.0 + jnp.exp2(-jnp.abs(xl)))) * (1.0 / GATE_NORM)


def _gla_bwd_kernel(qd_ref, kd_ref, tot_ref, v_ref, og_ref, of_ref, gn_ref, o_ref, s_scr):
    c = GLA_CHUNK

    @pl.when(pl.program_id(1) == 0)
    def _():
        s_scr[...] = jnp.zeros(s_scr.shape, F32)

    for j in reversed(range(qd_ref.shape[1] // c)):
        rows = slice(j * c, (j + 1) * c)
        for h in range(N_HEADS_B):
            sl = slice(h * KEY_DIM_B, (h + 1) * KEY_DIM_B)
            vs = slice(h * V_DIM_B, (h + 1) * V_DIM_B)
            st = s_scr[h]
            o = jnp.dot(qd_ref[0, rows, sl], st.astype(BF16), preferred_element_type=F32) + of_ref[0, rows, vs]
            dec = jnp.broadcast_to(jnp.exp2(tot_ref[0, j, 0:1, sl]), (c, KEY_DIM_B)).T[:, 0:1]
            s_scr[h] = st * dec + jnp.dot(kd_ref[0, rows, sl].astype(F32).T.astype(BF16), v_ref[0, rows, vs],
                                          preferred_element_type=F32)
            ms = jnp.mean(o * o, axis=-1, keepdims=True)
            og = og_ref[0, rows, vs].astype(F32)
            o_ref[0, rows, vs] = (o * lax.rsqrt(ms + EPS) * gn_ref[...] * (og / (1.0 + jnp.exp(-og)))
                                  ).astype(o_ref.dtype)


def _gla_chunk_scores(q, k, gf, gb):
    c = GLA_CHUNK
    ri = lax.broadcasted_iota(jnp.int32, (c, c), 0)
    ci = lax.broadcasted_iota(jnp.int32, (c, c), 1)
    xr = jnp.bitwise_xor(ri, ci)
    nt = (((1,), (1,)), ((), ()))
    pre, tf, suf, tb = gf, gf, gb, gb
    a = jnp.where(ri == ci, 2.0 * lax.dot_general(q.astype(BF16), k.astype(BF16), nt,
                                                  preferred_element_type=F32), 0.0)

    def level_scores(a, blk, q_arg, k_arg):
        r = lax.dot_general((q * jnp.exp2(q_arg)).astype(BF16), (k * jnp.exp2(k_arg)).astype(BF16),
                            nt, preferred_element_type=F32)
        return jnp.where(jnp.logical_and(xr >= blk, xr < 2 * blk), r, a)

    blk = 1
    while blk < SUBLANES:
        hi = jnp.bitwise_and(ri, blk) != 0
        a = level_scores(a, blk, jnp.where(hi, pre, suf), jnp.where(hi, tb - suf, tf - pre))
        tf_dn, tf_up = pltpu.roll(tf, blk, 0), pltpu.roll(tf, c - blk, 0)
        tb_dn, tb_up = pltpu.roll(tb, blk, 0), pltpu.roll(tb, c - blk, 0)
        pre = pre + jnp.where(hi, tf_dn, 0.0)
        suf = suf + jnp.where(hi, 0.0, tb_up)
        tf = tf + jnp.where(hi, tf_dn, tf_up)
        tb = tb + jnp.where(hi, tb_dn, tb_up)
        blk *= 2
    ns = c // SUBLANES
    slabs = lambda x: [x[SUBLANES * r:SUBLANES * (r + 1)] for r in range(ns)]
    pre, suf, tf, tb = slabs(pre), slabs(suf), slabs(tf), slabs(tb)
    while blk < c:
        g = blk // SUBLANES
        up = [bool(r & g) for r in range(ns)]
        q_arg = [pre[r] if up[r] else suf[r] for r in range(ns)]
        k_arg = [tb[r] - suf[r] if up[r] else tf[r] - pre[r] for r in range(ns)]
        a = level_scores(a, blk, jnp.concatenate(q_arg, axis=0), jnp.concatenate(k_arg, axis=0))
        pre = [pre[r] + tf[r - g] if up[r] else pre[r] for r in range(ns)]
        suf = [suf[r] if up[r] else suf[r] + tb[r + g] for r in range(ns)]
        tf_pair = {r: tf[r] + tf[r + g] for r in range(ns) if not up[r]}
        tb_pair = {r: tb[r] + tb[r + g] for r in range(ns) if not up[r]}
        tf = [tf_pair[r - g] if up[r] else tf_pair[r] for r in range(ns)]
        tb = [tb_pair[r - g] if up[r] else tb_pair[r] for r in range(ns)]
        blk *= 2
    cat = lambda x: jnp.concatenate(x, axis=0)
    return a, cat(pre), cat(tf), cat(suf), cat(tb)


def _gla_fwd_chunk(q, k, v, gf, gb, s_scr, h):
    a, pre, tf, suf, tb = _gla_chunk_scores(q, k, gf, gb)
    st = s_scr[h]
    o = (jnp.dot(a.astype(BF16), v, preferred_element_type=F32)
         + jnp.dot((q * jnp.exp2(pre)).astype(BF16), st.astype(BF16), preferred_element_type=F32))
    kd = k * jnp.exp2(tf - pre)
    s_scr[h] = st * jnp.exp2(tf).T[:, 0:1] + jnp.dot(kd.T.astype(BF16), v, preferred_element_type=F32)
    return o, (q * jnp.exp2(suf)).astype(BF16), (k * jnp.exp2(tb - suf)).astype(BF16), tb[0:SUBLANES]


def _gla_fwd_kernel(q_ref, k_ref, v_ref, lr_ref, wgf_ref, bgf_ref, wgb_ref, bgb_ref,
                    o_ref, qd_ref, kd_ref, tot_ref, s_scr):
    @pl.when(pl.program_id(1) == 0)
    def _():
        s_scr[...] = jnp.zeros(s_scr.shape, F32)

    lr = lr_ref[0]
    gf = _log_gate(lr, wgf_ref, bgf_ref)
    gb = _log_gate(lr, wgb_ref, bgb_ref)
    for h in range(N_HEADS_B):
        sl = slice(h * KEY_DIM_B, (h + 1) * KEY_DIM_B)
        vs = slice(h * V_DIM_B, (h + 1) * V_DIM_B)
        q = q_ref[0, :, sl].astype(F32) * KEY_DIM_B ** -0.5
        k = k_ref[0, :, sl].astype(F32)
        o_ref[0, :, vs], qd_ref[0, :, sl], kd_ref[0, :, sl], tot_ref[0, 0, :, sl] = _gla_fwd_chunk(
            q, k, v_ref[0, :, vs], gf[:, sl], gb[:, sl], s_scr, h)


def _gla(main3, lr3, wgf, bgf, wgb, bgb, gn):
    b, s, _ = main3.shape
    c = GLA_CHUNK
    nc = s // c
    hk = N_HEADS_B * KEY_DIM_B
    state = pltpu.VMEM((N_HEADS_B, KEY_DIM_B, V_DIM_B), F32)
    full = lambda shape: pl.BlockSpec(shape, lambda bi, i: (0,) * len(shape))
    fwd = lambda w, j: pl.BlockSpec((1, c, w), lambda bi, i: (bi, i, j))
    of3, qd, kd, tot = pl.pallas_call(
        _gla_fwd_kernel,
        grid=(b, nc),
        in_specs=[fwd(hk, 0), fwd(hk, 1), fwd(D_MODEL, 1), fwd(LR_W, 0),
                  full((LR_W, hk)), full((1, hk)), full((LR_W, hk)), full((1, hk))],
        out_specs=[fwd(D_MODEL, 0), fwd(hk, 0), fwd(hk, 0),
                   pl.BlockSpec((1, 1, SUBLANES, hk), lambda bi, i: (bi, i, 0, 0))],
        out_shape=[jax.ShapeDtypeStruct((b, s, D_MODEL), F32),
                   jax.ShapeDtypeStruct((b, s, hk), BF16),
                   jax.ShapeDtypeStruct((b, s, hk), BF16),
                   jax.ShapeDtypeStruct((b, nc, SUBLANES, hk), F32)],
        scratch_shapes=[state],
        compiler_params=_cparams(("parallel", "arbitrary")),
        name="gla_fwd",
    )(main3, main3, main3, lr3, wgf, bgf, wgb, bgb)

    cps = math.gcd(nc, GLA_BWD_CHUNKS)
    rev = lambda i: nc // cps - 1 - i
    blk = lambda w, j: pl.BlockSpec((1, cps * c, w), lambda bi, i: (bi, rev(i), j))
    return pl.pallas_call(
        _gla_bwd_kernel,
        grid=(b, nc // cps),
        in_specs=[blk(hk, 0), blk(hk, 0),
                  pl.BlockSpec((1, cps, SUBLANES, hk), lambda bi, i: (bi, rev(i), 0, 0)),
                  blk(D_MODEL, 1), blk(D_MODEL, 2), blk(D_MODEL, 0), full((1, V_DIM_B))],
        out_specs=blk(D_MODEL, 0),
        out_shape=jax.ShapeDtypeStruct((b, s, D_MODEL), BF16),
        scratch_shapes=[state],
        compiler_params=_cparams(("parallel", "arbitrary")),
        name="gla_bwd",
    )(qd, kd, tot, main3, main3, of3, gn)


def _tail_kernel(x_ref, oa_ref, ob_ref, ga_ref, gb_ref, mod_ref, g2_ref, wa_ref, wb_ref, wo_ref,
                 wu_ref, wd_ref, o_ref):
    ya = jnp.dot(oa_ref[...], wa_ref[...], preferred_element_type=F32)
    yb = jnp.dot(ob_ref[...], wb_ref[...], preferred_element_type=F32)
    sig = lambda r: 1.0 / (1.0 + jnp.exp(-r[...].astype(F32)))
    merged = sig(ga_ref) * ya + sig(gb_ref) * yb
    x1 = x_ref[...] + mod_ref[0, 2:3, :] * jnp.dot(merged.astype(BF16), wo_ref[...],
                                                   preferred_element_type=F32)
    h = _modulated_norm(x1, g2_ref[...], mod_ref[0, 3:4, :], mod_ref[0, 4:5, :]).astype(BF16)
    acc = jnp.zeros(x1.shape, F32)
    for j in range(D_FF // D_MODEL):
        cs = slice(j * D_MODEL, (j + 1) * D_MODEL)
        u = jnp.maximum(jnp.dot(h, wu_ref[:, cs], preferred_element_type=F32), 0.0)
        acc = acc + jnp.dot((u * u).astype(BF16), wd_ref[cs, :], preferred_element_type=F32)
    o_ref[...] = x1 + mod_ref[0, 5:6, :] * acc


def _tail(x2, oa2, ob2, main2, mod8, g2, wa, wb, wo, wu, wd, seq):
    t = x2.shape[0]
    tm = ROW_TILE
    row = lambda j: pl.BlockSpec((tm, D_MODEL), lambda i: (i, j))
    once = lambda shape: pl.BlockSpec(shape, lambda i: (0,) * len(shape), pipeline_mode=pl.Buffered(1))
    return pl.pallas_call(
        _tail_kernel,
        grid=(t // tm,),
        in_specs=[row(0), row(0), row(0), row(3), row(4),
                  pl.BlockSpec((1, SUBLANES, D_MODEL), lambda i: ((i * tm) // seq, 0, 0)),
                  once((1, D_MODEL)),
                  once((D_MODEL, D_MODEL)), once((D_MODEL, D_MODEL)), once((D_MODEL, D_MODEL)),
                  once((D_MODEL, D_FF)), once((D_FF, D_MODEL))],
        out_specs=row(0),
        out_shape=jax.ShapeDtypeStruct((t, D_MODEL), F32),
        compiler_params=_cparams(("parallel",)),
        name="tail",
    )(x2, oa2, ob2, main2, main2, mod8, g2, wa, wb, wo, wu, wd)


def _split_w_in(w_in):
    o = np.cumsum([0, 1024, 1024, 1024, 512, 512, 1024, 1024, GATE_RANK, GATE_RANK, 1024, 1024])
    main = jnp.concatenate([w_in[:, o[0]:o[7]], w_in[:, o[9]:o[11]]], axis=1)
    lr = jnp.pad(w_in[:, o[7]:o[9]], ((0, 0), (0, LR_W - 2 * GATE_RANK)))
    return main.astype(BF16), lr.astype(BF16)


def _pad_gate_w(w_gate, row0):
    return jnp.pad(w_gate, ((row0, LR_W - GATE_RANK - row0), (0, 0))).astype(BF16)


def _layer(x, mod8, layer_idx, p):
    b, s, _ = x.shape
    t = b * s
    lambda_init = 0.8 - 0.6 * math.exp(-0.3 * layer_idx)
    x2 = x.reshape(t, D_MODEL)
    row = lambda v: v.reshape(1, -1).astype(F32)

    qg2, kg2 = row(jnp.tile(p["q_norm_g"], 2)), row(jnp.tile(p["k_norm_g"], 2))
    qt, kn2, vt, main2, lr2 = _proj(x2, mod8, row(p["norm1_g"]), p["w_main"], p["w_lr"], qg2, kg2, s)
    kn = kn2.reshape(b, s, D_MODEL)
    main3 = main2.reshape(b, s, N_REST)
    lr3 = lr2.reshape(b, s, LR_W)
    lamv = jnp.zeros((SUBLANES, HEAD_W_A), F32).at[0:4, :HEAD_DIM_A].set(
        jnp.stack([p["lam_q1"], p["lam_k1"], p["lam_q2"], p["lam_k2"]]).astype(F32))
    bias_t, range_ok = _bias_tiles(p["rel_bias"], qg2, kg2)
    oa = _attn(range_ok, qt, kn, vt, bias_t, lamv, row(p["subln_g"]), lambda_init)

    ob = _gla(main3, lr3, p["wgf"], row(p["b_gate_f"]), p["wgb"], row(p["b_gate_b"]), row(p["gla_norm_g"]))

    y = _tail(x2, oa.reshape(t, D_MODEL), ob.reshape(t, D_MODEL), main2, mod8, row(p["norm2_g"]),
              p["w_branch_a"], p["w_branch_b"], p["w_out"], p["w_up"], p["w_down"], s)
    return y.reshape(b, s, D_MODEL)


def kernel(x_prompt, x_sample, c_prompt, c_sample, rel_bias, w_ada, b_ada, norm1_g, w_in, q_norm_g, k_norm_g, lam_q1, lam_k1, lam_q2, lam_k2, subln_g, w_gate_f, b_gate_f, w_gate_b, b_gate_b, gla_norm_g, w_branch_a, w_branch_b, w_out, norm2_g, w_up, w_down):
    depth = w_in.shape[0]
    nb_p, nb_s = c_prompt.shape[0], c_sample.shape[0]
    rows = -(-(nb_p + nb_s) // SUBLANES) * SUBLANES
    c_all = jnp.pad(jnp.concatenate([c_prompt, c_sample], axis=0), ((0, rows - nb_p - nb_s), (0, 0)))

    xp, xs = x_prompt, x_sample
    for l in range(depth):
        mod = _ada(c_all, w_ada[l], b_ada[l]).reshape(rows, 6, D_MODEL)
        mod8 = jnp.pad(mod, ((0, 0), (0, SUBLANES - 6), (0, 0)))
        w_main, w_lr = _split_w_in(w_in[l])
        p = dict(rel_bias=rel_bias.astype(F32), norm1_g=norm1_g[l], w_main=w_main, w_lr=w_lr,
                 q_norm_g=q_norm_g[l], k_norm_g=k_norm_g[l], lam_q1=lam_q1[l], lam_k1=lam_k1[l],
                 lam_q2=lam_q2[l], lam_k2=lam_k2[l], subln_g=subln_g[l],
                 wgf=_pad_gate_w(w_gate_f[l], 0), b_gate_f=b_gate_f[l],
                 wgb=_pad_gate_w(w_gate_b[l], GATE_RANK), b_gate_b=b_gate_b[l],
                 gla_norm_g=gla_norm_g[l], w_branch_a=w_branch_a[l].astype(BF16),
                 w_branch_b=w_branch_b[l].astype(BF16), w_out=w_out[l].astype(BF16),
                 norm2_g=norm2_g[l], w_up=w_up[l].astype(BF16), w_down=w_down[l].astype(BF16))
        xp = _layer(xp, mod8[:nb_p], l, p)
        xs = _layer(xs, mod8[nb_p:nb_p + nb_s], l, p)
    return (xp, xs)
```
